```python
import jax, jax.numpy as jnp
from jax import lax
import numpy as np

D_MODEL = 1024
BATCH = 16
SEQ = 2048
DEPTH = 1
DEC_BATCH = 128
DEC_SEQ = 1
PAST_LEN = 16384
PAGE_SIZE = 128

D_MIX = D_MODEL
D_A = D_MIX // 2
N_HEADS_A = 8
HEAD_DIM_A = D_A // N_HEADS_A
CHUNK = 128
N_HEADS_B = 8
QK_NOPE_DIM = 64
QK_ROPE_DIM = 32
V_HEAD_DIM = 64
D_B = N_HEADS_B * V_HEAD_DIM
Q_RANK = 384
KV_RANK = 256
D_IN = 2 * D_A + Q_RANK + KV_RANK + QK_ROPE_DIM
D_FF = 2816
CONV_W = 3
ROPE_THETA = 10000.0
EPS = 1e-6
Q_BLOCK = 128
ATTN_SCALE = (QK_NOPE_DIM + QK_ROPE_DIM) ** -0.5

kernel_name = 'hybrid_gmlp_mla_convffn_step'


def rmsnorm(x, g):
    xf = x.astype(jnp.float32)
    r = lax.rsqrt(jnp.mean(xf * xf, axis=-1, keepdims=True) + EPS)
    return (xf * r).astype(x.dtype) * g


def rope(x, pos):
    half = x.shape[-1] // 2
    inv = ROPE_THETA ** (-jnp.arange(half, dtype=jnp.float32) / half)
    ang = pos.astype(jnp.float32)[:, None] * inv
    ang = ang.reshape(ang.shape[:1] + (1,) * (x.ndim - 3) + ang.shape[1:])
    cos = jnp.cos(ang).astype(x.dtype)
    sin = jnp.sin(ang).astype(x.dtype)
    x1, x2 = x[..., :half], x[..., half:]
    return jnp.concatenate([x1 * cos - x2 * sin, x1 * sin + x2 * cos], axis=-1)


def spatial_gate(u, v, w_spatial, b_spatial, chunk_len):
    B, T, _ = v.shape
    L = chunk_len
    vc = v.reshape(B, T // L, L, N_HEADS_A, HEAD_DIM_A)
    ws = w_spatial[:, :L, :L] * jnp.tril(jnp.ones((L, L), v.dtype))
    s = jnp.einsum('hts,bcshd->bcthd', ws, vc) + b_spatial[:, :L].T[None, None, :, :, None]
    return u * s.reshape(B, T, D_A)


def mla_prompt(q_nope, q_rope, c_kv, k_rope, w_uk, w_uv):
    B, S = c_kv.shape[:2]
    nb = S // Q_BLOCK
    k_nope = jnp.einsum('bsr,rhd->bshd', c_kv, w_uk)
    val = jnp.einsum('bsr,rhd->bshd', c_kv, w_uv)
    kpos = jnp.arange(S)
    qn = q_nope.reshape(B, nb, Q_BLOCK, N_HEADS_B, QK_NOPE_DIM).swapaxes(0, 1)
    qr = q_rope.reshape(B, nb, Q_BLOCK, N_HEADS_B, QK_ROPE_DIM).swapaxes(0, 1)

    def one_block(args):
        i, qn_b, qr_b = args
        s = (jnp.einsum('bqhd,bkhd->bhqk', qn_b, k_nope)
             + jnp.einsum('bqhd,bkd->bhqk', qr_b, k_rope)).astype(jnp.float32) * ATTN_SCALE
        qpos = i * Q_BLOCK + jnp.arange(Q_BLOCK)
        s = jnp.where(kpos[None, :] <= qpos[:, None], s, -jnp.inf)
        p = jax.nn.softmax(s, axis=-1).astype(val.dtype)
        return jnp.einsum('bhqk,bkhd->bqhd', p, val)

    o = lax.map(one_block, (jnp.arange(nb), qn, qr))
    return o.swapaxes(0, 1).reshape(B, S, D_B)


def mla_sample(q_nope, q_rope, c_new, kr_new, c_past, kr_past, w_uk, w_uv):
    B, T = c_new.shape[:2]
    P = c_past.shape[1]
    q_lat = jnp.einsum('bthd,rhd->bthr', q_nope, w_uk)
    s_past = jnp.einsum('bthr,bkr->bhtk', q_lat, c_past) + jnp.einsum('bthd,bkd->bhtk', q_rope, kr_past)
    s_new = jnp.einsum('bthr,bkr->bhtk', q_lat, c_new) + jnp.einsum('bthd,bkd->bhtk', q_rope, kr_new)
    s = jnp.concatenate([s_past, s_new], axis=-1).astype(jnp.float32) * ATTN_SCALE
    causal = jnp.arange(T)[None, :] <= jnp.arange(T)[:, None]
    mask = jnp.concatenate([jnp.ones((T, P), bool), causal], axis=-1)
    s = jnp.where(mask, s, -jnp.inf)
    p = jax.nn.softmax(s, axis=-1).astype(c_new.dtype)
    o_lat = (jnp.einsum('bhtk,bkr->bthr', p[..., :P], c_past)
             + jnp.einsum('bhtk,bkr->bthr', p[..., P:], c_new))
    o = jnp.einsum('bthr,rhd->bthd', o_lat, w_uv)
    return o.reshape(B, T, D_B)


def conv_ffn(x, conv_prev, g_ffn, w_up, w_conv, b_conv, w_down):
    h = rmsnorm(x, g_ffn)
    up = h @ w_up
    T = up.shape[1]
    xp = jnp.concatenate([conv_prev, up], axis=1)
    c = b_conv
    for k in range(CONV_W):
        c = c + xp[:, k:k + T] * w_conv[k]
    gate, val = c[..., :D_FF], c[..., D_FF:]
    y = (jax.nn.silu(gate) * val) @ w_down
    return y, xp[:, -(CONV_W - 1):]


def layer(x, pos, conv_prev, chunk_len, attend, p):
    h = rmsnorm(x, p['g_mix'])
    proj = h @ p['w_in']
    o1, o2 = D_A, 2 * D_A
    o3 = o2 + Q_RANK
    o4 = o3 + KV_RANK
    u = jax.nn.gelu(proj[..., :o1], approximate=False)
    v = rmsnorm(jax.nn.gelu(proj[..., o1:o2], approximate=False), p['g_sgu'])
    c_q = rmsnorm(proj[..., o2:o3], p['g_q'])
    c_kv = rmsnorm(proj[..., o3:o4], p['g_kv'])
    k_rope = rope(proj[..., o4:], pos)
    q = jnp.einsum('bsr,rhd->bshd', c_q, p['w_uq'])
    q_nope = q[..., :QK_NOPE_DIM]
    q_rope = rope(q[..., QK_NOPE_DIM:], pos)
    out_a = rmsnorm(spatial_gate(u, v, p['w_spatial'], p['b_spatial'], chunk_len), p['g_out_a'])
    out_b = rmsnorm(attend(q_nope, q_rope, c_kv, k_rope), p['g_out_b'])
    x = x + jnp.concatenate([out_a, out_b], axis=-1) @ p['w_out']
    f, conv_new = conv_ffn(x, conv_prev, p['g_ffn'], p['w_up'], p['w_conv'], p['b_conv'], p['w_down'])
    return x + f, c_kv, k_rope, v, conv_new


def setup_inputs(seed: int = 0) -> dict:
    key = jax.random.key(seed)
    ks = jax.random.split(key, 32)
    f32 = jnp.float32
    n_pages = PAST_LEN // PAGE_SIZE
    n_used = DEC_BATCH * n_pages
    n_pool = n_used + n_used // 4

    def nrm(k, shape, scale):
        return jax.random.normal(k, shape, f32) * scale

    def gain(k, shape):
        return 1.0 + 0.01 * jax.random.normal(k, shape, f32)

    page_table = jax.random.permutation(ks[0], n_pool)[:n_used].reshape(DEC_BATCH, n_pages).astype(jnp.int32)
    return {
        'x_prompt': nrm(ks[1], (BATCH, SEQ, D_MODEL), 1.0),
        'x_sample': nrm(ks[2], (DEC_BATCH, DEC_SEQ, D_MODEL), 1.0),
        'cache_kv_latent': nrm(ks[3], (DEPTH, n_pool, PAGE_SIZE, KV_RANK), 1.0),
        'cache_k_rope': nrm(ks[4], (DEPTH, n_pool, PAGE_SIZE, QK_ROPE_DIM), 1.0),
        'state_ffn_conv': nrm(ks[5], (DEPTH, DEC_BATCH, CONV_W - 1, 2 * D_FF), 1.0),
        'page_table': page_table,
        'g_mix': gain(ks[6], (DEPTH, D_MODEL)),
        'w_in': nrm(ks[7], (DEPTH, D_MODEL, D_IN), D_MODEL ** -0.5),
        'g_sgu': gain(ks[8], (DEPTH, D_A)),
        'w_spatial': nrm(ks[9], (DEPTH, N_HEADS_A, CHUNK, CHUNK), CHUNK ** -0.5),
        'b_spatial': gain(ks[10], (DEPTH, N_HEADS_A, CHUNK)),
        'g_q': gain(ks[11], (DEPTH, Q_RANK)),
        'w_uq': nrm(ks[12], (DEPTH, Q_RANK, N_HEADS_B, QK_NOPE_DIM + QK_ROPE_DIM), Q_RANK ** -0.5),
        'g_kv': gain(ks[13], (DEPTH, KV_RANK)),
        'w_uk': nrm(ks[14], (DEPTH, KV_RANK, N_HEADS_B, QK_NOPE_DIM), KV_RANK ** -0.5),
        'w_uv': nrm(ks[15], (DEPTH, KV_RANK, N_HEADS_B, V_HEAD_DIM), KV_RANK ** -0.5),
        'g_out_a': gain(ks[16], (DEPTH, D_A)),
        'g_out_b': gain(ks[17], (DEPTH, D_B)),
        'w_out': nrm(ks[18], (DEPTH, D_MIX, D_MODEL), D_MIX ** -0.5),
        'g_ffn': gain(ks[19], (DEPTH, D_MODEL)),
        'w_up': nrm(ks[20], (DEPTH, D_MODEL, 2 * D_FF), D_MODEL ** -0.5),
        'w_conv': nrm(ks[21], (DEPTH, CONV_W, 2 * D_FF), CONV_W ** -0.5),
        'b_conv': nrm(ks[22], (DEPTH, 2 * D_FF), 0.02),
        'w_down': nrm(ks[23], (DEPTH, D_FF, D_MODEL), D_FF ** -0.5),
        'g_final': gain(ks[24], (D_MODEL,)),
    }


def reference(x_prompt, x_sample, cache_kv_latent, cache_k_rope, state_ffn_conv, page_table,
              g_mix, w_in, g_sgu, w_spatial, b_spatial, g_q, w_uq, g_kv, w_uk, w_uv,
              g_out_a, g_out_b, w_out, g_ffn, w_up, w_conv, b_conv, w_down, g_final):
    B, S = x_prompt.shape[:2]
    DB, T = x_sample.shape[:2]
    past_len = page_table.shape[1] * cache_kv_latent.shape[2]
    pos_p = jnp.arange(S, dtype=jnp.int32)
    pos_s = past_len + jnp.arange(T, dtype=jnp.int32)
    yp, ys = x_prompt, x_sample
    p_lat, p_kr, p_conv, s_lat, s_kr, s_v, s_conv = [], [], [], [], [], [], []
    for l in range(DEPTH):
        p = {'g_mix': g_mix[l], 'w_in': w_in[l], 'g_sgu': g_sgu[l], 'w_spatial': w_spatial[l],
             'b_spatial': b_spatial[l], 'g_q': g_q[l], 'w_uq': w_uq[l], 'g_kv': g_kv[l],
             'w_uk': w_uk[l], 'w_uv': w_uv[l], 'g_out_a': g_out_a[l], 'g_out_b': g_out_b[l],
             'w_out': w_out[l], 'g_ffn': g_ffn[l], 'w_up': w_up[l], 'w_conv': w_conv[l],
             'b_conv': b_conv[l], 'w_down': w_down[l]}
        attend_p = lambda qn, qr, c, kr, p=p: mla_prompt(qn, qr, c, kr, p['w_uk'], p['w_uv'])
        conv0 = jnp.zeros((B, CONV_W - 1, 2 * D_FF), yp.dtype)
        yp, c_p, kr_p, _, conv_p = layer(yp, pos_p, conv0, CHUNK, attend_p, p)
        c_past = cache_kv_latent[l, page_table].reshape(DB, -1, KV_RANK)
        kr_past = cache_k_rope[l, page_table].reshape(DB, -1, QK_ROPE_DIM)
        attend_s = lambda qn, qr, c, kr, p=p, c_past=c_past, kr_past=kr_past: mla_sample(
            qn, qr, c, kr, c_past, kr_past, p['w_uk'], p['w_uv'])
        ys, c_s, kr_s, v_s, conv_s = layer(ys, pos_s, state_ffn_conv[l], T, attend_s, p)
        p_lat.append(c_p); p_kr.append(kr_p); p_conv.append(conv_p)
        s_lat.append(c_s); s_kr.append(kr_s); s_v.append(v_s); s_conv.append(conv_s)
    y_prompt = rmsnorm(yp, g_final)
    y_sample = rmsnorm(ys, g_final)
    return (y_prompt, y_sample, jnp.stack(p_lat), jnp.stack(p_kr), jnp.stack(p_conv),
            jnp.stack(s_lat), jnp.stack(s_kr), jnp.stack(s_v), jnp.stack(s_conv))
```

```python
import functools

import jax
import jax.numpy as jnp
import numpy as np
from jax import lax
from jax.experimental import pallas as pl
from jax.experimental.pallas import tpu as pltpu

F32 = jnp.float32
BF16 = jnp.bfloat16

EPS = 1e-6
ROPE_THETA = 10000.0
LANE = 128
SUBLANE = 8
HEAD_SLOT = LANE
PROMPT_TILE = 256
ATTN_TILE = 256
FFN_CHUNK = 256
PAGES_PER_STEP = 16
VMEM_LIMIT = 56 * 1024 * 1024


def _rms(x, g):
    r = lax.rsqrt(jnp.mean(x * x, axis=-1, keepdims=True) + EPS)
    return (x * r) * g


def _gelu(x):
    return 0.5 * x * (1.0 + lax.erf(x * np.float32(np.sqrt(0.5))))


def _silu(x):
    return x * (1.0 / (1.0 + jnp.exp(-x)))


def _dot(a, b):
    return jnp.dot(a, b, preferred_element_type=F32)


def _dot_nt(a, b):
    return lax.dot_general(a, b, (((1,), (1,)), ((), ())), preferred_element_type=F32)


def _const_spec(shape, single_buffer=False):
    nd = len(shape)
    mode = pl.Buffered(1) if single_buffer else None
    return pl.BlockSpec(shape, lambda *_: (0,) * nd, pipeline_mode=mode)


def _params(semantics):
    return pltpu.CompilerParams(dimension_semantics=semantics, vmem_limit_bytes=VMEM_LIMIT)


def _front_common(x, g_mix, w_in, g_sgu, g_q, g_kv, w_q12, tqc, tqs, tk, d_a, q_rank, kv_rank, n_heads):
    h = _rms(x, g_mix).astype(BF16)
    proj = _dot(h, w_in)
    o1, o2 = d_a, 2 * d_a
    o3 = o2 + q_rank
    o4 = o3 + kv_rank
    u = _gelu(proj[:, :o1])
    v = _rms(_gelu(proj[:, o1:o2]), g_sgu)
    c_q = _rms(proj[:, o2:o3], g_q)
    c_kv = _rms(proj[:, o3:o4], g_kv)
    y = proj[:, o4:o4 + LANE] * tk
    kr = y + pltpu.roll(y, LANE // 2, 1)
    q12 = _dot(c_q.astype(BF16), w_q12)
    nq = n_heads * HEAD_SLOT
    q_parts = []
    for hd in range(n_heads):
        a = q12[:, hd * HEAD_SLOT:(hd + 1) * HEAD_SLOT]
        b = q12[:, nq + hd * HEAD_SLOT:nq + (hd + 1) * HEAD_SLOT]
        q_parts.append(a * tqc + b * tqs)
    return u, v, jnp.concatenate(q_parts, axis=1), c_kv, kr


def _prompt_front_kernel(x_ref, g_mix_ref, w_in_ref, g_sgu_ref, g_q_ref, g_kv_ref, w_q12_ref,
                         w_ukp_ref, w_uv_ref, w_sp_ref, b_sp_ref, g_out_a_ref,
                         tqc_ref, tqs_ref, tk_ref,
                         out_a_ref, q_ref, k_ref, v_ref, ckv_ref, kr_ref,
                         *, d_a, q_rank, kv_rank, n_heads, n_heads_a, chunk, rope_dim):
    tm = x_ref.shape[0]
    u, v, q_full, c_kv, kr = _front_common(
        x_ref[...], g_mix_ref[...], w_in_ref[...], g_sgu_ref[...], g_q_ref[...], g_kv_ref[...],
        w_q12_ref[...], tqc_ref[...], tqs_ref[...], tk_ref[...], d_a, q_rank, kv_rank, n_heads)
    q_ref[...] = q_full.astype(BF16)
    ckv_ref[...] = c_kv
    kr_ref[...] = kr[:, :rope_dim]

    c_bf = c_kv.astype(BF16)
    lane = lax.broadcasted_iota(jnp.int32, (tm, LANE), 1)
    kr_slot = jnp.where(lane >= LANE // 2, kr, 0.0)
    k_nope = _dot(c_bf, w_ukp_ref[...])
    k_parts = [k_nope[:, hd * HEAD_SLOT:(hd + 1) * HEAD_SLOT] + kr_slot for hd in range(n_heads)]
    k_ref[...] = jnp.concatenate(k_parts, axis=1).astype(BF16)
    v_ref[...] = _dot(c_bf, w_uv_ref[...]).astype(BF16)

    hd_a = d_a // n_heads_a
    heads_per_slab = LANE // hd_a
    row = lax.broadcasted_iota(jnp.int32, (chunk, chunk), 0)
    col = lax.broadcasted_iota(jnp.int32, (chunk, chunk), 1)
    w_causal = [jnp.where(row >= col, w_sp_ref[hd], 0.0).astype(BF16) for hd in range(n_heads_a)]
    v_bf = v.astype(BF16)
    lane_c = lax.broadcasted_iota(jnp.int32, (chunk, LANE), 1)
    gate_rows = []
    for c in range(tm // chunk):
        slabs = []
        for j in range(d_a // LANE):
            x_slab = v_bf[c * chunk:(c + 1) * chunk, j * LANE:(j + 1) * LANE]
            mixed = None
            for t in range(heads_per_slab):
                s_t = _dot(w_causal[j * heads_per_slab + t], x_slab)
                in_head = (lane_c >= t * hd_a) & (lane_c < (t + 1) * hd_a)
                mixed = s_t if mixed is None else jnp.where(in_head, s_t, mixed)
            slabs.append(mixed)
        gate_rows.append(jnp.concatenate(slabs, axis=1) + b_sp_ref[...])
    s = jnp.concatenate(gate_rows, axis=0)
    out_a_ref[...] = _rms(u * s, g_out_a_ref[...]).astype(BF16)


def _prompt_attn_kernel(q_ref, k_ref, v_ref, o_ref, *, scale, v_head_dim):
    tq = q_ref.shape[0]
    tk = tq
    i = pl.program_id(2)
    heads_per_slab = LANE // v_head_dim
    lane = lax.broadcasted_iota(jnp.int32, (tq, LANE), 1)
    qpos = lax.broadcasted_iota(jnp.int32, (tq, tk), 0)
    kpos = lax.broadcasted_iota(jnp.int32, (tq, tk), 1)
    out = None
    for t in range(heads_per_slab):
        q = q_ref[:, t * HEAD_SLOT:(t + 1) * HEAD_SLOT]

        def step(j, carry, masked, q=q, t=t):
            m, l, acc = carry
            start = pl.multiple_of(j * tk, tk)
            ks = k_ref[pl.ds(start, tk), t * HEAD_SLOT:(t + 1) * HEAD_SLOT]
            vs = v_ref[pl.ds(start, tk), :]
            s = _dot_nt(q, ks) * scale
            if masked:
                s = jnp.where(kpos <= qpos, s, -jnp.inf)
            m_new = jnp.maximum(m, jnp.max(s, axis=-1, keepdims=True))
            p = jnp.exp(s - m_new)
            alpha = jnp.exp(m - m_new)
            l_new = alpha * l + jnp.sum(p, axis=-1, keepdims=True)
            acc_new = alpha * acc + _dot(p.astype(BF16), vs)
            return m_new, l_new, acc_new

        init = (jnp.full((tq, 1), -jnp.inf, F32), jnp.zeros((tq, 1), F32), jnp.zeros((tq, LANE), F32))
        carry = lax.fori_loop(0, i, functools.partial(step, masked=False), init)
        m, l, acc = step(i, carry, True)
        o_t = acc / l
        in_head = (lane >= t * v_head_dim) & (lane < (t + 1) * v_head_dim)
        out = o_t if out is None else jnp.where(in_head, o_t, out)
    o_ref[...] = out


def _back_common(x, out_a, o_raw, g_out_b, w_out_ref, g_ffn, w_up_ref, w_conv_ref, b_conv_ref,
                 w_down_ref, g_final, conv_inputs, d_ff):
    d_a = out_a.shape[1]
    ob = _rms(o_raw, g_out_b).astype(BF16)
    mix = _dot(out_a, w_out_ref[:d_a, :]) + _dot(ob, w_out_ref[d_a:, :])
    x1 = x + mix
    h2 = _rms(x1, g_ffn).astype(BF16)
    f = None
    for j in range(d_ff // FFN_CHUNK):
        conv = []
        for base in (0, d_ff):
            cols = slice(base + j * FFN_CHUNK, base + (j + 1) * FFN_CHUNK)
            up = _dot(h2, w_up_ref[:, cols])
            r2, r1 = conv_inputs(up, cols)
            conv.append(b_conv_ref[:, cols] + r2 * w_conv_ref[0:1, cols] + r1 * w_conv_ref[1:2, cols]
                        + up * w_conv_ref[2:3, cols])
        act = (_silu(conv[0]) * conv[1]).astype(BF16)
        part = _dot(act, w_down_ref[j * FFN_CHUNK:(j + 1) * FFN_CHUNK, :])
        f = part if f is None else f + part
    return _rms(x1 + f, g_final)


def _prompt_back_kernel(x_ref, out_a_ref, o_ref, g_out_b_ref, w_out_ref, g_ffn_ref, w_up_ref,
                        w_conv_ref, b_conv_ref, w_down_ref, g_final_ref,
                        y_ref, conv_ref, tail_ref, *, tiles_per_seq, d_ff, conv_w):
    tm = x_ref.shape[0]
    keep = conv_w - 1

    @pl.when(pl.program_id(0) % tiles_per_seq == 0)
    def _():
        tail_ref[...] = jnp.zeros_like(tail_ref)

    row = lax.broadcasted_iota(jnp.int32, (tm, FFN_CHUNK), 0)

    def conv_inputs(up, cols):
        tail = tail_ref[:, cols]
        t1 = tail[SUBLANE - 1:SUBLANE, :]
        t2 = tail[SUBLANE - 2:SUBLANE - 1, :]
        r1 = jnp.where(row == 0, t1, pltpu.roll(up, 1, 0))
        r2 = jnp.where(row == 0, t2, jnp.where(row == 1, t1, pltpu.roll(up, 2, 0)))
        tail_ref[:, cols] = up[tm - SUBLANE:, :]
        conv_ref[:, cols] = up[tm - keep:, :]
        return r2, r1

    y_ref[...] = _back_common(
        x_ref[...], out_a_ref[...], o_ref[...], g_out_b_ref[...], w_out_ref, g_ffn_ref[...], w_up_ref,
        w_conv_ref, b_conv_ref, w_down_ref, g_final_ref[...], conv_inputs, d_ff)


def _sample_front_kernel(x_ref, g_mix_ref, w_in_ref, g_sgu_ref, g_q_ref, g_kv_ref, w_q12_ref,
                         w_ukt_ref, w_sp0_ref, b_sp0_ref, g_out_a_ref, tqc_ref, tqs_ref, tk_ref,
                         out_a_ref, v_ref, ckv_ref, kr_ref, qlat_ref, qrope_ref,
                         *, d_a, q_rank, kv_rank, n_heads, nope_dim, rope_dim):
    u, v, q_full, c_kv, kr = _front_common(
        x_ref[...], g_mix_ref[...], w_in_ref[...], g_sgu_ref[...], g_q_ref[...], g_kv_ref[...],
        w_q12_ref[...], tqc_ref[...], tqs_ref[...], tk_ref[...], d_a, q_rank, kv_rank, n_heads)
    v_ref[...] = v
    ckv_ref[...] = c_kv
    kr_ref[...] = kr[:, :rope_dim]
    s = v * w_sp0_ref[...] + b_sp0_ref[...]
    out_a_ref[...] = _rms(u * s, g_out_a_ref[...]).astype(BF16)
    q_bf = q_full.astype(BF16)
    for hd in range(n_heads):
        q_h = q_bf[:, hd * HEAD_SLOT:(hd + 1) * HEAD_SLOT]
        qlat_ref[hd] = _dot(q_h, w_ukt_ref[hd])
        qrope_ref[hd] = q_full[:, hd * HEAD_SLOT + nope_dim:hd * HEAD_SLOT + nope_dim + rope_dim]


def _sample_attn_kernel(pt_ref, qlat_ref, qrope_ref, cnew_ref, krnew_ref, *refs, scale, n_pages_step):
    del pt_ref
    c_refs = refs[:n_pages_step]
    kr_refs = refs[n_pages_step:2 * n_pages_step]
    o_ref, m_ref, l_ref, acc_ref = refs[2 * n_pages_step:]
    j = pl.program_id(1)
    q_lat = qlat_ref[...]
    q_rope = qrope_ref[...]

    @pl.when(j == 0)
    def _():
        c_new = cnew_ref[...]
        s_new = (jnp.sum(q_lat * c_new, axis=-1, keepdims=True)
                 + jnp.sum(q_rope * krnew_ref[...], axis=-1, keepdims=True)) * scale
        m_ref[...] = s_new
        l_ref[...] = jnp.ones_like(l_ref)
        acc_ref[...] = jnp.broadcast_to(c_new, acc_ref.shape)

    pages = [c_refs[i][...] for i in range(n_pages_step)]
    s = jnp.concatenate(
        [(_dot_nt(q_lat, pages[i]) + _dot_nt(q_rope, kr_refs[i][...])) * scale for i in range(n_pages_step)],
        axis=1)
    m_old = m_ref[...]
    m_new = jnp.maximum(m_old, jnp.max(s, axis=-1, keepdims=True))
    p = jnp.exp(s - m_new)
    alpha = jnp.exp(m_old - m_new)
    l_ref[...] = alpha * l_ref[...] + jnp.sum(p, axis=-1, keepdims=True)
    page = pages[0].shape[0]
    pv = None
    for i in range(n_pages_step):
        part = _dot(p[:, i * page:(i + 1) * page], pages[i])
        pv = part if pv is None else pv + part
    acc_ref[...] = alpha * acc_ref[...] + pv
    m_ref[...] = m_new

    @pl.when(j == pl.num_programs(1) - 1)
    def _():
        o_ref[...] = acc_ref[...] / l_ref[...]


def _sample_back_kernel(x_ref, out_a_ref, olat_ref, w_uvp_ref, prev_ref, g_out_b_ref, w_out_ref, g_ffn_ref,
                        w_up_ref, w_conv_ref, b_conv_ref, w_down_ref, g_final_ref,
                        y_ref, up_ref, *, n_heads, d_ff):
    o_raw = None
    for hd in range(n_heads):
        part = _dot(olat_ref[hd].astype(BF16), w_uvp_ref[hd])
        o_raw = part if o_raw is None else o_raw + part

    def conv_inputs(up, cols):
        up_ref[:, cols] = up
        return prev_ref[0, :, cols], prev_ref[1, :, cols]

    y_ref[...] = _back_common(
        x_ref[...], out_a_ref[...], o_raw, g_out_b_ref[...], w_out_ref, g_ffn_ref[...], w_up_ref,
        w_conv_ref, b_conv_ref, w_down_ref, g_final_ref[...], conv_inputs, d_ff)


def _rope_tables(pos, rope_dim, nope_dim):
    half = rope_dim // 2
    inv = ROPE_THETA ** (-jnp.arange(half, dtype=F32) / half)
    ang = pos.astype(F32)[:, None] * inv
    cos, sin = jnp.cos(ang), jnp.sin(ang)
    n = pos.shape[0]
    z = lambda w: jnp.zeros((n, w), F32)
    pad = LANE - nope_dim - rope_dim
    tqc = jnp.concatenate([jnp.ones((n, nope_dim), F32), cos, cos, z(pad)], axis=1)
    tqs = jnp.concatenate([z(nope_dim), -sin, sin, z(pad)], axis=1)
    tk = jnp.concatenate([-sin, sin, z(LANE // 2 - rope_dim), cos, cos, z(LANE // 2 - rope_dim)], axis=1)
    return tqc, tqs, tk


def _pack_weights(w_in, w_uq, w_uk, w_uv, d_a, q_rank, kv_rank, rope_dim, nope_dim):
    half = rope_dim // 2
    d_model = w_in.shape[0]
    n_heads = w_uq.shape[1]
    o4 = 2 * d_a + q_rank + kv_rank
    x1, x2 = w_in[:, o4:o4 + half], w_in[:, o4 + half:o4 + rope_dim]
    zc = lambda rows, w: jnp.zeros((rows, w), w_in.dtype)
    gap = LANE // 2 - rope_dim
    w_in_p = jnp.concatenate([w_in[:, :o4], x2, x1, zc(d_model, gap), x1, x2, zc(d_model, gap)], axis=1)
    pad = HEAD_SLOT - nope_dim - rope_dim
    main, swap = [], []
    for hd in range(n_heads):
        nope = w_uq[:, hd, :nope_dim]
        r1, r2 = w_uq[:, hd, nope_dim:nope_dim + half], w_uq[:, hd, nope_dim + half:]
        main.append(jnp.concatenate([nope, r1, r2, zc(q_rank, pad)], axis=1))
        swap.append(jnp.concatenate([zc(q_rank, nope_dim), r2, r1, zc(q_rank, pad)], axis=1))
    w_q12 = jnp.concatenate(main + swap, axis=1)
    w_ukp = jnp.concatenate(
        [jnp.concatenate([w_uk[:, hd, :], zc(kv_rank, HEAD_SLOT - nope_dim)], axis=1) for hd in range(n_heads)],
        axis=1)
    w_ukt = jnp.stack(
        [jnp.concatenate([w_uk[:, hd, :].T, jnp.zeros((HEAD_SLOT - nope_dim, kv_rank), w_uk.dtype)], axis=0)
         for hd in range(n_heads)])
    v_dim = w_uv.shape[2]
    w_uv_flat = w_uv.reshape(kv_rank, n_heads * v_dim)
    w_uvp = jnp.stack(
        [jnp.pad(w_uv[:, hd, :], ((0, 0), (hd * v_dim, (n_heads - 1 - hd) * v_dim))) for hd in range(n_heads)])
    cast = lambda a: a.astype(BF16)
    return cast(w_in_p), cast(w_q12), cast(w_ukp), cast(w_ukt), cast(w_uv_flat), cast(w_uvp)


def kernel(x_prompt, x_sample, cache_kv_latent, cache_k_rope, state_ffn_conv, page_table,
           g_mix, w_in, g_sgu, w_spatial, b_spatial, g_q, w_uq, g_kv, w_uk, w_uv,
           g_out_a, g_out_b, w_out, g_ffn, w_up, w_conv, b_conv, w_down, g_final):
    B, S, D = x_prompt.shape
    DB, T, _ = x_sample.shape
    depth = w_in.shape[0]
    assert depth == 1 and T == 1
    n_heads_a, chunk = w_spatial.shape[1], w_spatial.shape[2]
    d_a = g_sgu.shape[1]
    q_rank, kv_rank = g_q.shape[1], g_kv.shape[1]
    n_heads, nope_dim, v_dim = w_uk.shape[2], w_uk.shape[3], w_uv.shape[3]
    rope_dim = w_uq.shape[3] - nope_dim
    d_b = n_heads * v_dim
    d_ff = w_down.shape[1]
    conv_w = w_conv.shape[1]
    n_pool, page_size = cache_kv_latent.shape[1], cache_kv_latent.shape[2]
    n_pages = page_table.shape[1]
    past_len = n_pages * page_size
    scale = float((nope_dim + rope_dim) ** -0.5)
    assert d_ff % FFN_CHUNK == 0 and S % PROMPT_TILE == 0 and PROMPT_TILE % chunk == 0
    assert n_pages % PAGES_PER_STEP == 0 and conv_w == 3 and LANE % v_dim == 0

    row2 = lambda a: a.reshape(1, -1)
    w_in_p, w_q12, w_ukp, w_ukt, w_uv_flat, w_uvp = _pack_weights(
        w_in[0], w_uq[0], w_uk[0], w_uv[0], d_a, q_rank, kv_rank, rope_dim, nope_dim)
    w_out_bf, w_up_bf, w_down_bf = w_out[0].astype(BF16), w_up[0].astype(BF16), w_down[0].astype(BF16)
    gm, gs, gq, gkv = row2(g_mix[0]), row2(g_sgu[0]), row2(g_q[0]), row2(g_kv[0])
    goa, gob, gf, gfin = row2(g_out_a[0]), row2(g_out_b[0]), row2(g_ffn[0]), row2(g_final)
    wc, bc = w_conv[0], row2(b_conv[0])
    hd_a = d_a // n_heads_a

    n_rows = B * S
    tm = PROMPT_TILE
    tiles_per_seq = S // tm
    xp = x_prompt.reshape(n_rows, D)
    tqc, tqs, tk = _rope_tables(jnp.arange(S, dtype=jnp.int32), rope_dim, nope_dim)
    b_sp = jnp.repeat(b_spatial[0][:, :chunk].T, hd_a, axis=1)
    n_in = w_in_p.shape[1]
    nq = n_heads * HEAD_SLOT
    rows = lambda w: pl.BlockSpec((tm, w), lambda i: (i, 0))
    table = pl.BlockSpec((tm, LANE), lambda i: (i % tiles_per_seq, 0))
    front = pl.pallas_call(
        functools.partial(_prompt_front_kernel, d_a=d_a, q_rank=q_rank, kv_rank=kv_rank, n_heads=n_heads,
                          n_heads_a=n_heads_a, chunk=chunk, rope_dim=rope_dim),
        grid=(n_rows // tm,),
        in_specs=[rows(D), _const_spec((1, D)), _const_spec((D, n_in), True), _const_spec((1, d_a)),
                  _const_spec((1, q_rank)), _const_spec((1, kv_rank)), _const_spec((q_rank, 2 * nq), True),
                  _const_spec((kv_rank, nq), True), _const_spec((kv_rank, d_b), True),
                  _const_spec((n_heads_a, chunk, chunk), True), _const_spec((chunk, d_a)), _const_spec((1, d_a)),
                  table, table, table],
        out_specs=[rows(d_a), rows(nq), rows(nq), rows(d_b), rows(kv_rank), rows(rope_dim)],
        out_shape=[jax.ShapeDtypeStruct((n_rows, d_a), BF16), jax.ShapeDtypeStruct((n_rows, nq), BF16),
                   jax.ShapeDtypeStruct((n_rows, nq), BF16), jax.ShapeDtypeStruct((n_rows, d_b), BF16),
                   jax.ShapeDtypeStruct((n_rows, kv_rank), F32), jax.ShapeDtypeStruct((n_rows, rope_dim), F32)],
        compiler_params=_params(("arbitrary",)),
        name="prompt_front",
    )
    out_a_p, q_p, k_p, v_p, ckv_p, kr_p = front(
        xp, gm, w_in_p, gs, gq, gkv, w_q12, w_ukp, w_uv_flat, w_spatial[0][:, :chunk, :chunk], b_sp, goa,
        tqc, tqs, tk)

    tq = ATTN_TILE
    nqb = S // tq
    heads_per_slab = LANE // v_dim
    n_slabs = n_heads // heads_per_slab
    slab_w = heads_per_slab * HEAD_SLOT
    o_p = pl.pallas_call(
        functools.partial(_prompt_attn_kernel, scale=scale, v_head_dim=v_dim),
        grid=(B, n_slabs, nqb),
        in_specs=[pl.BlockSpec((tq, slab_w), lambda b, p, i: (b * nqb + i, p)),
                  pl.BlockSpec((S, slab_w), lambda b, p, i: (b, p)),
                  pl.BlockSpec((S, LANE), lambda b, p, i: (b, p))],
        out_specs=pl.BlockSpec((tq, LANE), lambda b, p, i: (b * nqb + i, p)),
        out_shape=jax.ShapeDtypeStruct((n_rows, d_b), F32),
        compiler_params=_params(("arbitrary", "arbitrary", "arbitrary")),
        name="prompt_attn",
    )(q_p, k_p, v_p)

    back_weights = [_const_spec((1, d_b)), _const_spec((d_a + d_b, D), True), _const_spec((1, D)),
                    _const_spec((D, 2 * d_ff), True), _const_spec((conv_w, 2 * d_ff)), _const_spec((1, 2 * d_ff)),
                    _const_spec((d_ff, D), True), _const_spec((1, D))]
    y_p, conv_p = pl.pallas_call(
        functools.partial(_prompt_back_kernel, tiles_per_seq=tiles_per_seq, d_ff=d_ff, conv_w=conv_w),
        grid=(n_rows // tm,),
        in_specs=[rows(D), rows(d_a), rows(d_b)] + back_weights,
        out_specs=[rows(D), pl.BlockSpec((None, conv_w - 1, 2 * d_ff), lambda i: (i // tiles_per_seq, 0, 0))],
        out_shape=[jax.ShapeDtypeStruct((n_rows, D), F32),
                   jax.ShapeDtypeStruct((B, conv_w - 1, 2 * d_ff), F32)],
        scratch_shapes=[pltpu.VMEM((SUBLANE, 2 * d_ff), F32)],
        compiler_params=_params(("arbitrary",)),
        name="prompt_back",
    )(xp, out_a_p, o_p, gob, w_out_bf, gf, w_up_bf, wc, bc, w_down_bf, gfin)

    xs = x_sample.reshape(DB, D)
    sqc, sqs, sk = _rope_tables(past_len + jnp.arange(T, dtype=jnp.int32), rope_dim, nope_dim)
    w_sp0 = jnp.repeat(w_spatial[0][:, 0, 0], hd_a).reshape(1, d_a)
    b_sp0 = jnp.repeat(b_spatial[0][:, 0], hd_a).reshape(1, d_a)
    full = lambda *shape: _const_spec(shape, True)
    whole = lambda *shape: _const_spec(shape)
    out_a_s, v_s, ckv_s, kr_s, qlat_h, qrope_h = pl.pallas_call(
        functools.partial(_sample_front_kernel, d_a=d_a, q_rank=q_rank, kv_rank=kv_rank, n_heads=n_heads,
                          nope_dim=nope_dim, rope_dim=rope_dim),
        grid=(1,),
        in_specs=[full(DB, D), full(1, D), full(D, n_in), full(1, d_a), full(1, q_rank), full(1, kv_rank),
                  full(q_rank, 2 * nq), full(n_heads, HEAD_SLOT, kv_rank), full(1, d_a), full(1, d_a),
                  full(1, d_a), full(1, LANE), full(1, LANE), full(1, LANE)],
        out_specs=[whole(DB, d_a), whole(DB, d_a), whole(DB, kv_rank), whole(DB, rope_dim),
                   whole(n_heads, DB, kv_rank), whole(n_heads, DB, rope_dim)],
        out_shape=[jax.ShapeDtypeStruct((DB, d_a), BF16), jax.ShapeDtypeStruct((DB, d_a), F32),
                   jax.ShapeDtypeStruct((DB, kv_rank), F32), jax.ShapeDtypeStruct((DB, rope_dim), F32),
                   jax.ShapeDtypeStruct((n_heads, DB, kv_rank), F32),
                   jax.ShapeDtypeStruct((n_heads, DB, rope_dim), F32)],
        compiler_params=_params(("arbitrary",)),
        name="sample_front",
    )(xs, gm, w_in_p, gs, gq, gkv, w_q12, w_ukt, w_sp0, b_sp0, goa, sqc, sqs, sk)

    P = PAGES_PER_STEP
    steps = n_pages // P
    per_seq = lambda *tail: pl.BlockSpec((None,) + tail, lambda b, j, pt: (b,) + (0,) * len(tail))
    page_spec = lambda width, i: pl.BlockSpec(
        (None, None, page_size, width), lambda b, j, pt, i=i: (0, pt[b, j * P + i], 0, 0))
    o_lat = pl.pallas_call(
        functools.partial(_sample_attn_kernel, scale=scale, n_pages_step=P),
        grid_spec=pltpu.PrefetchScalarGridSpec(
            num_scalar_prefetch=1,
            grid=(DB, steps),
            in_specs=[per_seq(n_heads, kv_rank), per_seq(n_heads, rope_dim), per_seq(1, kv_rank),
                      per_seq(1, rope_dim)]
                     + [page_spec(kv_rank, i) for i in range(P)] + [page_spec(rope_dim, i) for i in range(P)],
            out_specs=per_seq(n_heads, kv_rank),
            scratch_shapes=[pltpu.VMEM((n_heads, 1), F32), pltpu.VMEM((n_heads, 1), F32),
                            pltpu.VMEM((n_heads, kv_rank), F32)],
        ),
        out_shape=jax.ShapeDtypeStruct((DB, n_heads, kv_rank), F32),
        compiler_params=_params(("arbitrary", "arbitrary")),
        name="sample_attn",
    )(page_table, qlat_h.transpose(1, 0, 2), qrope_h.transpose(1, 0, 2), ckv_s.reshape(DB, 1, kv_rank),
      kr_s.reshape(DB, 1, rope_dim), *([cache_kv_latent] * P), *([cache_k_rope] * P))

    prev = state_ffn_conv[0].transpose(1, 0, 2)
    y_s, up_s = pl.pallas_call(
        functools.partial(_sample_back_kernel, n_heads=n_heads, d_ff=d_ff),
        grid=(1,),
        in_specs=[full(DB, D), full(DB, d_a), full(n_heads, DB, kv_rank), full(n_heads, kv_rank, d_b),
                  full(conv_w - 1, DB, 2 * d_ff), full(1, d_b), full(d_a + d_b, D), full(1, D),
                  full(D, 2 * d_ff), full(conv_w, 2 * d_ff), full(1, 2 * d_ff), full(d_ff, D), full(1, D)],
        out_specs=[whole(DB, D), whole(DB, 2 * d_ff)],
        out_shape=[jax.ShapeDtypeStruct((DB, D), F32), jax.ShapeDtypeStruct((DB, 2 * d_ff), F32)],
        compiler_params=_params(("arbitrary",)),
        name="sample_back",
    )(xs, out_a_s, o_lat.transpose(1, 0, 2), w_uvp, prev, gob, w_out_bf, gf, w_up_bf, wc, bc, w_down_bf, gfin)

    conv_s = jnp.concatenate([state_ffn_conv[0][:, 1:, :], up_s[:, None, :]], axis=1)
    return (y_p.reshape(B, S, D), y_s.reshape(DB, T, D),
            ckv_p.reshape(1, B, S, kv_rank), kr_p.reshape(1, B, S, rope_dim),
            conv_p.reshape(1, B, conv_w - 1, 2 * d_ff),
            ckv_s.reshape(1, DB, T, kv_rank), kr_s.reshape(1, DB, T, rope_dim),
            v_s.reshape(1, DB, T, d_a), conv_s.reshape(1, DB, conv_w - 1, 2 * d_ff))
```

```python
import functools

import jax
import jax.numpy as jnp
import numpy as np
from jax import lax
from jax.experimental import pallas as pl
from jax.experimental.pallas import tpu as pltpu

F32 = jnp.float32
BF16 = jnp.bfloat16

EPS = 1e-6
ROPE_THETA = 10000.0
LANE = 128
SUBLANE = 8
HEAD_SLOT = LANE
PROMPT_TILE = 512
ATTN_Q_TILE = 512
ATTN_K_TILE = 256
FFN_CHUNK = 256
PAGES_PER_CHUNK = 16
VMEM_LIMIT = 56 * 1024 * 1024
LOG2E = float(np.log2(np.e))


def _rms(x, g):
    r = lax.rsqrt(jnp.mean(x * x, axis=-1, keepdims=True) + EPS)
    return (x * r) * g


def _gelu(x):
    return 0.5 * x * (1.0 + lax.erf(x * np.float32(np.sqrt(0.5))))


def _silu(x):
    return x * (1.0 / (1.0 + jnp.exp(-x)))


def _dot(a, b):
    return jnp.dot(a, b, preferred_element_type=F32)


def _dot_nt(a, b):
    return lax.dot_general(a, b, (((1,), (1,)), ((), ())), preferred_element_type=F32)


def _const_spec(shape, single_buffer=False):
    nd = len(shape)
    mode = pl.Buffered(1) if single_buffer else None
    return pl.BlockSpec(shape, lambda *_: (0,) * nd, pipeline_mode=mode)


def _params(semantics):
    return pltpu.CompilerParams(dimension_semantics=semantics, vmem_limit_bytes=VMEM_LIMIT)


def _front_common(x, g_mix, w_in, g_sgu, g_q, g_kv, w_q12, tqc, tqs, tk, d_a, q_rank, kv_rank, n_heads):
    h = _rms(x, g_mix).astype(BF16)
    proj = _dot(h, w_in)
    o1, o2 = d_a, 2 * d_a
    o3 = o2 + q_rank
    o4 = o3 + kv_rank
    u = _gelu(proj[:, :o1])
    v = _rms(_gelu(proj[:, o1:o2]), g_sgu)
    c_q = _rms(proj[:, o2:o3], g_q)
    c_kv = _rms(proj[:, o3:o4], g_kv)
    y = proj[:, o4:o4 + LANE] * tk
    kr = y + pltpu.roll(y, LANE // 2, 1)
    q12 = _dot(c_q.astype(BF16), w_q12)
    nq = n_heads * HEAD_SLOT
    q_parts = []
    for hd in range(n_heads):
        a = q12[:, hd * HEAD_SLOT:(hd + 1) * HEAD_SLOT]
        b = q12[:, nq + hd * HEAD_SLOT:nq + (hd + 1) * HEAD_SLOT]
        q_parts.append(a * tqc + b * tqs)
    return u, v, jnp.concatenate(q_parts, axis=1), c_kv, kr


def _prompt_front_kernel(x_ref, g_mix_ref, w_in_ref, g_sgu_ref, g_q_ref, g_kv_ref, w_q12_ref,
                         w_ukp_ref, w_uv_ref, w_sp_ref, b_sp_ref, g_out_a_ref,
                         tqc_ref, tqs_ref, tk_ref,
                         out_a_ref, q_ref, k_ref, v_ref, ckv_ref, kr_ref,
                         *, d_a, q_rank, kv_rank, n_heads, n_heads_a, chunk, rope_dim):
    tm = x_ref.shape[0]
    u, v, q_full, c_kv, kr = _front_common(
        x_ref[...], g_mix_ref[...], w_in_ref[...], g_sgu_ref[...], g_q_ref[...], g_kv_ref[...],
        w_q12_ref[...], tqc_ref[...], tqs_ref[...], tk_ref[...], d_a, q_rank, kv_rank, n_heads)
    q_ref[...] = q_full.astype(BF16)
    ckv_ref[...] = c_kv
    kr_ref[...] = kr[:, :rope_dim]

    c_bf = c_kv.astype(BF16)
    lane = lax.broadcasted_iota(jnp.int32, (tm, LANE), 1)
    kr_slot = jnp.where(lane >= LANE // 2, kr, 0.0)
    k_nope = _dot(c_bf, w_ukp_ref[...])
    k_parts = [k_nope[:, hd * HEAD_SLOT:(hd + 1) * HEAD_SLOT] + kr_slot for hd in range(n_heads)]
    k_ref[...] = jnp.concatenate(k_parts, axis=1).astype(BF16)
    v_ref[...] = _dot(c_bf, w_uv_ref[...]).astype(BF16)

    hd_a = d_a // n_heads_a
    heads_per_slab = LANE // hd_a
    row = lax.broadcasted_iota(jnp.int32, (chunk, chunk), 0)
    col = lax.broadcasted_iota(jnp.int32, (chunk, chunk), 1)
    w_causal = [jnp.where(row >= col, w_sp_ref[hd], 0.0).astype(BF16) for hd in range(n_heads_a)]
    v_bf = v.astype(BF16)
    lane_c = lax.broadcasted_iota(jnp.int32, (chunk, LANE), 1)
    gate_rows = []
    for c in range(tm // chunk):
        slabs = []
        for j in range(d_a // LANE):
            x_slab = v_bf[c * chunk:(c + 1) * chunk, j * LANE:(j + 1) * LANE]
            mixed = None
            for t in range(heads_per_slab):
                s_t = _dot(w_causal[j * heads_per_slab + t], x_slab)
                in_head = (lane_c >= t * hd_a) & (lane_c < (t + 1) * hd_a)
                mixed = s_t if mixed is None else jnp.where(in_head, s_t, mixed)
            slabs.append(mixed)
        gate_rows.append(jnp.concatenate(slabs, axis=1) + b_sp_ref[...])
    s = jnp.concatenate(gate_rows, axis=0)
    out_a_ref[...] = _rms(u * s, g_out_a_ref[...]).astype(BF16)


def _prompt_attn_kernel(q_ref, k_ref, v_ref, o_ref, *, scale, v_head_dim, tk):
    tq = q_ref.shape[0]
    i = pl.program_id(2)
    heads_per_slab = LANE // v_head_dim
    c2 = scale * LOG2E
    qs = [q_ref[:, t * HEAD_SLOT:(t + 1) * HEAD_SLOT] for t in range(heads_per_slab)]
    qpos = i * tq + lax.broadcasted_iota(jnp.int32, (tq, tk), 0)
    kloc = lax.broadcasted_iota(jnp.int32, (tq, tk), 1)

    def step(j, carry, masked):
        start = pl.multiple_of(j * tk, tk)
        vs = v_ref[pl.ds(start, tk), :]
        new = []
        for t in range(heads_per_slab):
            m, l, acc = carry[t]
            ks = k_ref[pl.ds(start, tk), t * HEAD_SLOT:(t + 1) * HEAD_SLOT]
            s = _dot_nt(qs[t], ks)
            if masked:
                s = jnp.where(start + kloc <= qpos, s, -jnp.inf)
            m_new = jnp.maximum(m, jnp.max(s, axis=-1, keepdims=True))
            p = jnp.exp2((s - m_new) * c2)
            alpha = jnp.exp2((m - m_new) * c2)
            l_new = alpha * l + jnp.sum(p, axis=-1, keepdims=True)
            acc_new = alpha * acc + _dot(p.astype(BF16), vs)
            new.append((m_new, l_new, acc_new))
        return tuple(new)

    init = tuple((jnp.full((tq, 1), -jnp.inf, F32), jnp.zeros((tq, 1), F32), jnp.zeros((tq, LANE), F32))
                 for _ in range(heads_per_slab))
    n_full = i * (tq // tk)
    carry = lax.fori_loop(0, n_full, functools.partial(step, masked=False), init)
    for d in range(tq // tk):
        carry = step(n_full + d, carry, True)
    lane = lax.broadcasted_iota(jnp.int32, (tq, LANE), 1)
    out = None
    for t in range(heads_per_slab):
        m, l, acc = carry[t]
        o_t = acc / l
        in_head = (lane >= t * v_head_dim) & (lane < (t + 1) * v_head_dim)
        out = o_t if out is None else jnp.where(in_head, o_t, out)
    o_ref[...] = out


def _back_common(x, out_a, o_raw, g_out_b, w_out_ref, g_ffn, w_up_ref, w_conv_ref, b_conv_ref,
                 w_down_ref, g_final, conv_inputs, d_ff):
    d_a = out_a.shape[1]
    ob = _rms(o_raw, g_out_b).astype(BF16)
    mix = _dot(out_a, w_out_ref[:d_a, :]) + _dot(ob, w_out_ref[d_a:, :])
    x1 = x + mix
    h2 = _rms(x1, g_ffn).astype(BF16)
    f = None
    for j in range(d_ff // FFN_CHUNK):
        conv = []
        for base in (0, d_ff):
            cols = slice(base + j * FFN_CHUNK, base + (j + 1) * FFN_CHUNK)
            up = _dot(h2, w_up_ref[:, cols])
            r2, r1 = conv_inputs(up, cols)
            conv.append(b_conv_ref[:, cols] + r2 * w_conv_ref[0:1, cols] + r1 * w_conv_ref[1:2, cols]
                        + up * w_conv_ref[2:3, cols])
        act = (_silu(conv[0]) * conv[1]).astype(BF16)
        part = _dot(act, w_down_ref[j * FFN_CHUNK:(j + 1) * FFN_CHUNK, :])
        f = part if f is None else f + part
    return _rms(x1 + f, g_final)


def _prompt_back_kernel(x_ref, out_a_ref, o_ref, g_out_b_ref, w_out_ref, g_ffn_ref, w_up_ref,
                        w_conv_ref, b_conv_ref, w_down_ref, g_final_ref,
                        y_ref, conv_ref, tail_ref, *, tiles_per_seq, d_ff, conv_w):
    tm = x_ref.shape[0]
    keep = conv_w - 1

    @pl.when(pl.program_id(0) % tiles_per_seq == 0)
    def _():
        tail_ref[...] = jnp.zeros_like(tail_ref)

    row = lax.broadcasted_iota(jnp.int32, (tm, FFN_CHUNK), 0)

    def conv_inputs(up, cols):
        tail = tail_ref[:, cols]
        t1 = tail[SUBLANE - 1:SUBLANE, :]
        t2 = tail[SUBLANE - 2:SUBLANE - 1, :]
        r1 = jnp.where(row == 0, t1, pltpu.roll(up, 1, 0))
        r2 = jnp.where(row == 0, t2, jnp.where(row == 1, t1, pltpu.roll(up, 2, 0)))
        tail_ref[:, cols] = up[tm - SUBLANE:, :]
        conv_ref[:, cols] = up[tm - keep:, :]
        return r2, r1

    y_ref[...] = _back_common(
        x_ref[...], out_a_ref[...], o_ref[...], g_out_b_ref[...], w_out_ref, g_ffn_ref[...], w_up_ref,
        w_conv_ref, b_conv_ref, w_down_ref, g_final_ref[...], conv_inputs, d_ff)


def _sample_front_kernel(x_ref, g_mix_ref, w_in_ref, g_sgu_ref, g_q_ref, g_kv_ref, w_q12_ref,
                         w_ukt_ref, w_sp0_ref, b_sp0_ref, g_out_a_ref, tqc_ref, tqs_ref, tk_ref,
                         out_a_ref, v_ref, ckv_ref, kr_ref, qlat_ref, qrope_ref,
                         *, d_a, q_rank, kv_rank, n_heads, nope_dim, rope_dim):
    u, v, q_full, c_kv, kr = _front_common(
        x_ref[...], g_mix_ref[...], w_in_ref[...], g_sgu_ref[...], g_q_ref[...], g_kv_ref[...],
        w_q12_ref[...], tqc_ref[...], tqs_ref[...], tk_ref[...], d_a, q_rank, kv_rank, n_heads)
    v_ref[...] = v
    ckv_ref[...] = c_kv
    kr_ref[...] = kr[:, :rope_dim]
    s = v * w_sp0_ref[...] + b_sp0_ref[...]
    out_a_ref[...] = _rms(u * s, g_out_a_ref[...]).astype(BF16)
    q_bf = q_full.astype(BF16)
    for hd in range(n_heads):
        q_h = q_bf[:, hd * HEAD_SLOT:(hd + 1) * HEAD_SLOT]
        qlat_ref[hd] = _dot(q_h, w_ukt_ref[hd])
        qrope_ref[hd] = q_full[:, hd * HEAD_SLOT + nope_dim:hd * HEAD_SLOT + nope_dim + rope_dim]


def _sample_attn_kernel(pt_ref, qlat_ref, qrope_ref, cnew_ref, krnew_ref, c_hbm, krt_hbm, o_ref,
                        cbuf, krbuf, sem_c, sem_k, *, scale, pages_per_chunk, page_size):
    b = pl.program_id(0)
    n_seq = pl.num_programs(0)
    chunks = pt_ref.shape[1] // pages_per_chunk
    c2 = scale * LOG2E

    def chunk_copies(seq, ch, slot):
        copies = []
        for i in range(pages_per_chunk):
            page = pt_ref[seq, ch * pages_per_chunk + i]
            rows = pl.ds(i * page_size, page_size)
            copies.append(pltpu.make_async_copy(c_hbm.at[0, page], cbuf.at[slot, rows, :], sem_c.at[slot]))
            copies.append(pltpu.make_async_copy(krt_hbm.at[0, page], krbuf.at[slot, :, rows], sem_k.at[slot]))
        return copies

    @pl.when(b == 0)
    def _():
        for cp in chunk_copies(0, 0, 0):
            cp.start()

    q_lat = qlat_ref[...]
    q_rope = qrope_ref[...]
    q_lat_bf, q_rope_bf = q_lat.astype(BF16), q_rope.astype(BF16)
    c_new = cnew_ref[...]
    m = (jnp.sum(q_lat * c_new, axis=-1, keepdims=True)
         + jnp.sum(q_rope * krnew_ref[...], axis=-1, keepdims=True))
    l = jnp.ones_like(m)
    acc = jnp.broadcast_to(c_new, q_lat.shape)
    b_next = jnp.minimum(b + 1, n_seq - 1)

    def chunk_pair(k, state):
        m, l, acc = state
        for slot in range(2):
            ch = 2 * k + slot
            wraps = ch + 1 == chunks
            nb = jnp.where(wraps, b_next, b)
            nch = jnp.where(wraps, 0, ch + 1)
            for cp in chunk_copies(nb, nch, 1 - slot):
                cp.start()
            for cp in chunk_copies(b, ch, slot):
                cp.wait()
            c_bf = cbuf[slot].astype(BF16)
            s = _dot_nt(q_lat_bf, c_bf) + _dot(q_rope_bf, krbuf[slot].astype(BF16))
            m_new = jnp.maximum(m, jnp.max(s, axis=-1, keepdims=True))
            p = jnp.exp2((s - m_new) * c2)
            alpha = jnp.exp2((m - m_new) * c2)
            l = alpha * l + jnp.sum(p, axis=-1, keepdims=True)
            acc = alpha * acc + _dot(p.astype(BF16), c_bf)
            m = m_new
        return m, l, acc

    m, l, acc = lax.fori_loop(0, chunks // 2, chunk_pair, (m, l, acc))
    o_ref[...] = acc / l

    @pl.when(b == n_seq - 1)
    def _():
        for cp in chunk_copies(b, 0, 0):
            cp.wait()


def _sample_back_kernel(x_ref, out_a_ref, olat_ref, w_uvp_ref, prev_ref, g_out_b_ref, w_out_ref, g_ffn_ref,
                        w_up_ref, w_conv_ref, b_conv_ref, w_down_ref, g_final_ref,
                        y_ref, up_ref, *, n_heads, d_ff):
    o_raw = None
    for hd in range(n_heads):
        part = _dot(olat_ref[hd].astype(BF16), w_uvp_ref[hd])
        o_raw = part if o_raw is None else o_raw + part

    def conv_inputs(up, cols):
        up_ref[:, cols] = up
        return prev_ref[0, :, cols], prev_ref[1, :, cols]

    y_ref[...] = _back_common(
        x_ref[...], out_a_ref[...], o_raw, g_out_b_ref[...], w_out_ref, g_ffn_ref[...], w_up_ref,
        w_conv_ref, b_conv_ref, w_down_ref, g_final_ref[...], conv_inputs, d_ff)


def _rope_tables(pos, rope_dim, nope_dim):
    half = rope_dim // 2
    inv = ROPE_THETA ** (-jnp.arange(half, dtype=F32) / half)
    ang = pos.astype(F32)[:, None] * inv
    cos, sin = jnp.cos(ang), jnp.sin(ang)
    n = pos.shape[0]
    z = lambda w: jnp.zeros((n, w), F32)
    pad = LANE - nope_dim - rope_dim
    tqc = jnp.concatenate([jnp.ones((n, nope_dim), F32), cos, cos, z(pad)], axis=1)
    tqs = jnp.concatenate([z(nope_dim), -sin, sin, z(pad)], axis=1)
    tk = jnp.concatenate([-sin, sin, z(LANE // 2 - rope_dim), cos, cos, z(LANE // 2 - rope_dim)], axis=1)
    return tqc, tqs, tk


def _pack_weights(w_in, w_uq, w_uk, w_uv, d_a, q_rank, kv_rank, rope_dim, nope_dim):
    half = rope_dim // 2
    d_model = w_in.shape[0]
    n_heads = w_uq.shape[1]
    o4 = 2 * d_a + q_rank + kv_rank
    x1, x2 = w_in[:, o4:o4 + half], w_in[:, o4 + half:o4 + rope_dim]
    zc = lambda rows, w: jnp.zeros((rows, w), w_in.dtype)
    gap = LANE // 2 - rope_dim
    w_in_p = jnp.concatenate([w_in[:, :o4], x2, x1, zc(d_model, gap), x1, x2, zc(d_model, gap)], axis=1)
    pad = HEAD_SLOT - nope_dim - rope_dim
    main, swap = [], []
    for hd in range(n_heads):
        nope = w_uq[:, hd, :nope_dim]
        r1, r2 = w_uq[:, hd, nope_dim:nope_dim + half], w_uq[:, hd, nope_dim + half:]
        main.append(jnp.concatenate([nope, r1, r2, zc(q_rank, pad)], axis=1))
        swap.append(jnp.concatenate([zc(q_rank, nope_dim), r2, r1, zc(q_rank, pad)], axis=1))
    w_q12 = jnp.concatenate(main + swap, axis=1)
    w_ukp = jnp.concatenate(
        [jnp.concatenate([w_uk[:, hd, :], zc(kv_rank, HEAD_SLOT - nope_dim)], axis=1) for hd in range(n_heads)],
        axis=1)
    w_ukt = jnp.stack(
        [jnp.concatenate([w_uk[:, hd, :].T, jnp.zeros((HEAD_SLOT - nope_dim, kv_rank), w_uk.dtype)], axis=0)
         for hd in range(n_heads)])
    v_dim = w_uv.shape[2]
    w_uv_flat = w_uv.reshape(kv_rank, n_heads * v_dim)
    w_uvp = jnp.stack(
        [jnp.pad(w_uv[:, hd, :], ((0, 0), (hd * v_dim, (n_heads - 1 - hd) * v_dim))) for hd in range(n_heads)])
    cast = lambda a: a.astype(BF16)
    return cast(w_in_p), cast(w_q12), cast(w_ukp), cast(w_ukt), cast(w_uv_flat), cast(w_uvp)


def kernel(x_prompt, x_sample, cache_kv_latent, cache_k_rope, state_ffn_conv, page_table,
           g_mix, w_in, g_sgu, w_spatial, b_spatial, g_q, w_uq, g_kv, w_uk, w_uv,
           g_out_a, g_out_b, w_out, g_ffn, w_up, w_conv, b_conv, w_down, g_final):
    B, S, D = x_prompt.shape
    DB, T, _ = x_sample.shape
    depth = w_in.shape[0]
    assert depth == 1 and T == 1
    n_heads_a, chunk = w_spatial.shape[1], w_spatial.shape[2]
    d_a = g_sgu.shape[1]
    q_rank, kv_rank = g_q.shape[1], g_kv.shape[1]
    n_heads, nope_dim, v_dim = w_uk.shape[2], w_uk.shape[3], w_uv.shape[3]
    rope_dim = w_uq.shape[3] - nope_dim
    d_b = n_heads * v_dim
    d_ff = w_down.shape[1]
    conv_w = w_conv.shape[1]
    n_pool, page_size = cache_kv_latent.shape[1], cache_kv_latent.shape[2]
    n_pages = page_table.shape[1]
    past_len = n_pages * page_size
    scale = float((nope_dim + rope_dim) ** -0.5)
    assert d_ff % FFN_CHUNK == 0 and S % PROMPT_TILE == 0 and PROMPT_TILE % chunk == 0
    assert n_pages % (2 * PAGES_PER_CHUNK) == 0 and conv_w == 3 and LANE % v_dim == 0
    assert S % ATTN_Q_TILE == 0 and ATTN_Q_TILE % ATTN_K_TILE == 0

    row2 = lambda a: a.reshape(1, -1)
    w_in_p, w_q12, w_ukp, w_ukt, w_uv_flat, w_uvp = _pack_weights(
        w_in[0], w_uq[0], w_uk[0], w_uv[0], d_a, q_rank, kv_rank, rope_dim, nope_dim)
    w_out_bf, w_up_bf, w_down_bf = w_out[0].astype(BF16), w_up[0].astype(BF16), w_down[0].astype(BF16)
    gm, gs, gq, gkv = row2(g_mix[0]), row2(g_sgu[0]), row2(g_q[0]), row2(g_kv[0])
    goa, gob, gf, gfin = row2(g_out_a[0]), row2(g_out_b[0]), row2(g_ffn[0]), row2(g_final)
    wc, bc = w_conv[0], row2(b_conv[0])
    hd_a = d_a // n_heads_a

    n_rows = B * S
    tm = PROMPT_TILE
    tiles_per_seq = S // tm
    xp = x_prompt.reshape(n_rows, D)
    tqc, tqs, tk = _rope_tables(jnp.arange(S, dtype=jnp.int32), rope_dim, nope_dim)
    b_sp = jnp.repeat(b_spatial[0][:, :chunk].T, hd_a, axis=1)
    n_in = w_in_p.shape[1]
    nq = n_heads * HEAD_SLOT
    rows = lambda w: pl.BlockSpec((tm, w), lambda i: (i, 0))
    table = pl.BlockSpec((tm, LANE), lambda i: (i % tiles_per_seq, 0))
    front = pl.pallas_call(
        functools.partial(_prompt_front_kernel, d_a=d_a, q_rank=q_rank, kv_rank=kv_rank, n_heads=n_heads,
                          n_heads_a=n_heads_a, chunk=chunk, rope_dim=rope_dim),
        grid=(n_rows // tm,),
        in_specs=[rows(D), _const_spec((1, D)), _const_spec((D, n_in), True), _const_spec((1, d_a)),
                  _const_spec((1, q_rank)), _const_spec((1, kv_rank)), _const_spec((q_rank, 2 * nq), True),
                  _const_spec((kv_rank, nq), True), _const_spec((kv_rank, d_b), True),
                  _const_spec((n_heads_a, chunk, chunk), True), _const_spec((chunk, d_a)), _const_spec((1, d_a)),
                  table, table, table],
        out_specs=[rows(d_a), rows(nq), rows(nq), rows(d_b), rows(kv_rank), rows(rope_dim)],
        out_shape=[jax.ShapeDtypeStruct((n_rows, d_a), BF16), jax.ShapeDtypeStruct((n_rows, nq), BF16),
                   jax.ShapeDtypeStruct((n_rows, nq), BF16), jax.ShapeDtypeStruct((n_rows, d_b), BF16),
                   jax.ShapeDtypeStruct((n_rows, kv_rank), F32), jax.ShapeDtypeStruct((n_rows, rope_dim), F32)],
        compiler_params=_params(("arbitrary",)),
        name="prompt_front",
    )
    out_a_p, q_p, k_p, v_p, ckv_p, kr_p = front(
        xp, gm, w_in_p, gs, gq, gkv, w_q12, w_ukp, w_uv_flat, w_spatial[0][:, :chunk, :chunk], b_sp, goa,
        tqc, tqs, tk)

    tq = ATTN_Q_TILE
    nqb = S // tq
    heads_per_slab = LANE // v_dim
    n_slabs = n_heads // heads_per_slab
    slab_w = heads_per_slab * HEAD_SLOT
    o_p = pl.pallas_call(
        functools.partial(_prompt_attn_kernel, scale=scale, v_head_dim=v_dim, tk=ATTN_K_TILE),
        grid=(B, n_slabs, nqb),
        in_specs=[pl.BlockSpec((tq, slab_w), lambda b, p, i: (b * nqb + i, p)),
                  pl.BlockSpec((S, slab_w), lambda b, p, i: (b, p)),
                  pl.BlockSpec((S, LANE), lambda b, p, i: (b, p))],
        out_specs=pl.BlockSpec((tq, LANE), lambda b, p, i: (b * nqb + i, p)),
        out_shape=jax.ShapeDtypeStruct((n_rows, d_b), F32),
        compiler_params=_params(("arbitrary", "arbitrary", "arbitrary")),
        name="prompt_attn",
    )(q_p, k_p, v_p)

    back_weights = [_const_spec((1, d_b)), _const_spec((d_a + d_b, D), True), _const_spec((1, D)),
                    _const_spec((D, 2 * d_ff), True), _const_spec((conv_w, 2 * d_ff)), _const_spec((1, 2 * d_ff)),
                    _const_spec((d_ff, D), True), _const_spec((1, D))]
    y_p, conv_p = pl.pallas_call(
        functools.partial(_prompt_back_kernel, tiles_per_seq=tiles_per_seq, d_ff=d_ff, conv_w=conv_w),
        grid=(n_rows // tm,),
        in_specs=[rows(D), rows(d_a), rows(d_b)] + back_weights,
        out_specs=[rows(D), pl.BlockSpec((None, conv_w - 1, 2 * d_ff), lambda i: (i // tiles_per_seq, 0, 0))],
        out_shape=[jax.ShapeDtypeStruct((n_rows, D), F32),
                   jax.ShapeDtypeStruct((B, conv_w - 1, 2 * d_ff), F32)],
        scratch_shapes=[pltpu.VMEM((SUBLANE, 2 * d_ff), F32)],
        compiler_params=_params(("arbitrary",)),
        name="prompt_back",
    )(xp, out_a_p, o_p, gob, w_out_bf, gf, w_up_bf, wc, bc, w_down_bf, gfin)

    xs = x_sample.reshape(DB, D)
    sqc, sqs, sk = _rope_tables(past_len + jnp.arange(T, dtype=jnp.int32), rope_dim, nope_dim)
    w_sp0 = jnp.repeat(w_spatial[0][:, 0, 0], hd_a).reshape(1, d_a)
    b_sp0 = jnp.repeat(b_spatial[0][:, 0], hd_a).reshape(1, d_a)
    full = lambda *shape: _const_spec(shape, True)
    whole = lambda *shape: _const_spec(shape)
    out_a_s, v_s, ckv_s, kr_s, qlat_h, qrope_h = pl.pallas_call(
        functools.partial(_sample_front_kernel, d_a=d_a, q_rank=q_rank, kv_rank=kv_rank, n_heads=n_heads,
                          nope_dim=nope_dim, rope_dim=rope_dim),
        grid=(1,),
        in_specs=[full(DB, D), full(1, D), full(D, n_in), full(1, d_a), full(1, q_rank), full(1, kv_rank),
                  full(q_rank, 2 * nq), full(n_heads, HEAD_SLOT, kv_rank), full(1, d_a), full(1, d_a),
                  full(1, d_a), full(1, LANE), full(1, LANE), full(1, LANE)],
        out_specs=[whole(DB, d_a), whole(DB, d_a), whole(DB, kv_rank), whole(DB, rope_dim),
                   whole(n_heads, DB, kv_rank), whole(n_heads, DB, rope_dim)],
        out_shape=[jax.ShapeDtypeStruct((DB, d_a), BF16), jax.ShapeDtypeStruct((DB, d_a), F32),
                   jax.ShapeDtypeStruct((DB, kv_rank), F32), jax.ShapeDtypeStruct((DB, rope_dim), F32),
                   jax.ShapeDtypeStruct((n_heads, DB, kv_rank), F32),
                   jax.ShapeDtypeStruct((n_heads, DB, rope_dim), F32)],
        compiler_params=_params(("arbitrary",)),
        name="sample_front",
    )(xs, gm, w_in_p, gs, gq, gkv, w_q12, w_ukt, w_sp0, b_sp0, goa, sqc, sqs, sk)

    P = PAGES_PER_CHUNK
    chunk_keys = P * page_size
    any_space = pl.BlockSpec(memory_space=pl.ANY)
    per_seq = lambda *tail: pl.BlockSpec((None,) + tail, lambda b, pt: (b,) + (0,) * len(tail))
    cache_krt = jnp.swapaxes(cache_k_rope, 2, 3)
    o_lat = pl.pallas_call(
        functools.partial(_sample_attn_kernel, scale=scale, pages_per_chunk=P, page_size=page_size),
        grid_spec=pltpu.PrefetchScalarGridSpec(
            num_scalar_prefetch=1,
            grid=(DB,),
            in_specs=[per_seq(n_heads, kv_rank), per_seq(n_heads, rope_dim), per_seq(1, kv_rank),
                      per_seq(1, rope_dim), any_space, any_space],
            out_specs=per_seq(n_heads, kv_rank),
            scratch_shapes=[pltpu.VMEM((2, chunk_keys, kv_rank), F32), pltpu.VMEM((2, rope_dim, chunk_keys), F32),
                            pltpu.SemaphoreType.DMA((2,)), pltpu.SemaphoreType.DMA((2,))],
        ),
        out_shape=jax.ShapeDtypeStruct((DB, n_heads, kv_rank), F32),
        compiler_params=_params(("arbitrary",)),
        name="sample_attn",
    )(page_table, qlat_h.transpose(1, 0, 2), qrope_h.transpose(1, 0, 2), ckv_s.reshape(DB, 1, kv_rank),
      kr_s.reshape(DB, 1, rope_dim), cache_kv_latent, cache_krt)

    prev = state_ffn_conv[0].transpose(1, 0, 2)
    y_s, up_s = pl.pallas_call(
        functools.partial(_sample_back_kernel, n_heads=n_heads, d_ff=d_ff),
        grid=(1,),
        in_specs=[full(DB, D), full(DB, d_a), full(n_heads, DB, kv_rank), full(n_heads, kv_rank, d_b),
                  full(conv_w - 1, DB, 2 * d_ff), full(1, d_b), full(d_a + d_b, D), full(1, D),
                  full(D, 2 * d_ff), full(conv_w, 2 * d_ff), full(1, 2 * d_ff), full(d_ff, D), full(1, D)],
        out_specs=[whole(DB, D), whole(DB, 2 * d_ff)],
        out_shape=[jax.ShapeDtypeStruct((DB, D), F32), jax.ShapeDtypeStruct((DB, 2 * d_ff), F32)],
        compiler_params=_params(("arbitrary",)),
        name="sample_back",
    )(xs, out_a_s, o_lat.transpose(1, 0, 2), w_uvp, prev, gob, w_out_bf, gf, w_up_bf, wc, bc, w_down_bf, gfin)

    conv_s = jnp.concatenate([state_ffn_conv[0][:, 1:, :], up_s[:, None, :]], axis=1)
    return (y_p.reshape(B, S, D), y_s.reshape(DB, T, D),
            ckv_p.reshape(1, B, S, kv_rank), kr_p.reshape(1, B, S, rope_dim),
            conv_p.reshape(1, B, conv_w - 1, 2 * d_ff),
            ckv_s.reshape(1, DB, T, kv_rank), kr_s.reshape(1, DB, T, rope_dim),
            v_s.reshape(1, DB, T, d_a), conv_s.reshape(1, DB, conv_w - 1, 2 * d_ff))
```

```python
import functools

import jax
import jax.numpy as jnp
import numpy as np
from jax import lax
from jax.experimental import pallas as pl
from jax.experimental.pallas import tpu as pltpu

F32 = jnp.float32
BF16 = jnp.bfloat16

EPS = 1e-6
ROPE_THETA = 10000.0
LANE = 128
SUBLANE = 8
HEAD_SLOT = LANE
PROMPT_TILE = 512
ATTN_Q_TILE = 512
ATTN_K_TILE = 256
ATTN_QK_LEAD = 2
FFN_CHUNK = 256
CONV_STAGE_BUFFERS = 4
PAGES_PER_CHUNK = 16
SAMPLE_DMA_LEAD = 2
SAMPLE_SLOTS = SAMPLE_DMA_LEAD + 2
VMEM_LIMIT = 56 * 1024 * 1024
LOG2E = float(np.log2(np.e))


def _rms(x, g):
    r = lax.rsqrt(jnp.mean(x * x, axis=-1, keepdims=True) + EPS)
    return (x * r) * g


def _gelu(x):
    return 0.5 * x * (1.0 + lax.erf(x * np.float32(np.sqrt(0.5))))


def _silu(x):
    return x * (1.0 / (1.0 + jnp.exp(-x)))


def _dot(a, b):
    return jnp.dot(a, b, preferred_element_type=F32)


def _dot_nt(a, b):
    return lax.dot_general(a, b, (((1,), (1,)), ((), ())), preferred_element_type=F32)


def _const_spec(shape, single_buffer=False):
    nd = len(shape)
    mode = pl.Buffered(1) if single_buffer else None
    return pl.BlockSpec(shape, lambda *_: (0,) * nd, pipeline_mode=mode)


def _params(semantics):
    return pltpu.CompilerParams(dimension_semantics=semantics, vmem_limit_bytes=VMEM_LIMIT)


def _front_common(x, g_mix, w_in, g_sgu, g_q, g_kv, w_q12, tqc, tqs, tk, d_a, q_rank, kv_rank, n_heads):
    h = _rms(x, g_mix).astype(BF16)
    proj = _dot(h, w_in)
    o1, o2 = d_a, 2 * d_a
    o3 = o2 + q_rank
    o4 = o3 + kv_rank
    u = _gelu(proj[:, :o1])
    v = _rms(_gelu(proj[:, o1:o2]), g_sgu)
    c_q = _rms(proj[:, o2:o3], g_q)
    c_kv = _rms(proj[:, o3:o4], g_kv)
    y = proj[:, o4:o4 + LANE] * tk
    kr = y + pltpu.roll(y, LANE // 2, 1)
    q12 = _dot(c_q.astype(BF16), w_q12)
    nq = n_heads * HEAD_SLOT
    q_parts = []
    for hd in range(n_heads):
        a = q12[:, hd * HEAD_SLOT:(hd + 1) * HEAD_SLOT]
        b = q12[:, nq + hd * HEAD_SLOT:nq + (hd + 1) * HEAD_SLOT]
        q_parts.append(a * tqc + b * tqs)
    return u, v, jnp.concatenate(q_parts, axis=1), c_kv, kr


def _prompt_front_kernel(x_ref, g_mix_ref, w_in_ref, g_sgu_ref, g_q_ref, g_kv_ref, w_q12_ref,
                         w_ukp_ref, w_uvt_ref, w_sp_ref, b_sp_ref, g_out_a_ref,
                         tqc_ref, tqs_ref, tk_ref,
                         out_a_ref, q_ref, k_ref, vt_ref, ckv_ref, kr_ref,
                         *, d_a, q_rank, kv_rank, n_heads, n_heads_a, chunk, rope_dim):
    tm = x_ref.shape[0]
    u, v, q_full, c_kv, kr = _front_common(
        x_ref[...], g_mix_ref[...], w_in_ref[...], g_sgu_ref[...], g_q_ref[...], g_kv_ref[...],
        w_q12_ref[...], tqc_ref[...], tqs_ref[...], tk_ref[...], d_a, q_rank, kv_rank, n_heads)
    q_ref[...] = q_full.astype(BF16)
    ckv_ref[...] = c_kv
    kr_ref[...] = kr[:, :rope_dim]

    c_bf = c_kv.astype(BF16)
    lane = lax.broadcasted_iota(jnp.int32, (tm, LANE), 1)
    kr_slot = jnp.where(lane >= LANE // 2, kr, 0.0)
    k_nope = _dot(c_bf, w_ukp_ref[...])
    k_parts = [k_nope[:, hd * HEAD_SLOT:(hd + 1) * HEAD_SLOT] + kr_slot for hd in range(n_heads)]
    k_ref[...] = jnp.concatenate(k_parts, axis=1).astype(BF16)
    v_t = _dot_nt(w_uvt_ref[...], c_bf).astype(BF16)
    tk = vt_ref.shape[2]
    for c in range(tm // tk):
        vt_ref[c] = v_t[:, c * tk:(c + 1) * tk]

    hd_a = d_a // n_heads_a
    heads_per_slab = LANE // hd_a
    row = lax.broadcasted_iota(jnp.int32, (chunk, chunk), 0)
    col = lax.broadcasted_iota(jnp.int32, (chunk, chunk), 1)
    w_causal = [jnp.where(row >= col, w_sp_ref[hd], 0.0).astype(BF16) for hd in range(n_heads_a)]
    v_bf = v.astype(BF16)
    lane_c = lax.broadcasted_iota(jnp.int32, (chunk, LANE), 1)
    gate_rows = []
    for c in range(tm // chunk):
        slabs = []
        for j in range(d_a // LANE):
            x_slab = v_bf[c * chunk:(c + 1) * chunk, j * LANE:(j + 1) * LANE]
            mixed = None
            for t in range(heads_per_slab):
                s_t = _dot(w_causal[j * heads_per_slab + t], x_slab)
                in_head = (lane_c >= t * hd_a) & (lane_c < (t + 1) * hd_a)
                mixed = s_t if mixed is None else jnp.where(in_head, s_t, mixed)
            slabs.append(mixed)
        gate_rows.append(jnp.concatenate(slabs, axis=1) + b_sp_ref[...])
    s = jnp.concatenate(gate_rows, axis=0)
    out_a_ref[...] = _rms(u * s, g_out_a_ref[...]).astype(BF16)


def _prompt_attn_kernel(q_ref, k_ref, vt_ref, o_ref, *, scale, v_head_dim):
    tq = q_ref.shape[0]
    tk = vt_ref.shape[2]
    i = pl.program_id(1)
    n_heads = q_ref.shape[1] // HEAD_SLOT
    c2 = scale * LOG2E
    q_t = [q_ref[:, t * HEAD_SLOT:(t + 1) * HEAD_SLOT].astype(F32).T.astype(BF16)
           for t in range(n_heads)]
    qpos = i * tq + lax.broadcasted_iota(jnp.int32, (tk, tq), 1)
    kloc = lax.broadcasted_iota(jnp.int32, (tk, tq), 0)

    def step(j, carry, masked):
        start = pl.multiple_of(j * tk, tk)
        v_t = vt_ref[j]

        def scores(t):
            ks = k_ref[pl.ds(start, tk), t * HEAD_SLOT:(t + 1) * HEAD_SLOT]
            s_t = _dot(ks, q_t[t])
            return jnp.where(start + kloc <= qpos, s_t, -jnp.inf) if masked else s_t

        s_ahead = [scores(t) for t in range(min(ATTN_QK_LEAD, n_heads))]
        new = []
        for t in range(n_heads):
            m, l, acc = carry[t]
            if t + ATTN_QK_LEAD < n_heads:
                s_ahead.append(scores(t + ATTN_QK_LEAD))
            s_t = s_ahead[t]
            m_new = jnp.maximum(m, jnp.max(s_t, axis=0, keepdims=True))
            p_t = jnp.exp2((s_t - m_new) * c2)
            alpha = jnp.exp2((m - m_new) * c2)
            l_new = alpha * l + jnp.sum(p_t, axis=0, keepdims=True)
            acc_new = alpha * acc + _dot(v_t[t * v_head_dim:(t + 1) * v_head_dim, :], p_t.astype(BF16))
            new.append((m_new, l_new, acc_new))
        return tuple(new)

    init = tuple((jnp.full((1, tq), -jnp.inf, F32), jnp.zeros((1, tq), F32), jnp.zeros((v_head_dim, tq), F32))
                 for _ in range(n_heads))
    n_full = i * (tq // tk)
    carry = lax.fori_loop(0, n_full, functools.partial(step, masked=False), init)
    for d in range(tq // tk):
        carry = step(n_full + d, carry, True)
    o_t = jnp.concatenate([acc / l for _, l, acc in carry], axis=0)
    o_ref[...] = o_t.T


def _back_common(x, out_a, o_raw, g_out_b, w_out_ref, g_ffn, w_up_ref, w_conv_ref, b_conv_ref,
                 w_down_ref, g_final, conv_inputs, d_ff):
    d_a = out_a.shape[1]
    ob = _rms(o_raw, g_out_b).astype(BF16)
    mix = _dot(out_a, w_out_ref[:d_a, :]) + _dot(ob, w_out_ref[d_a:, :])
    x1 = x + mix
    h2 = _rms(x1, g_ffn).astype(BF16)
    n_chunks = d_ff // FFN_CHUNK

    def up_proj(j):
        cols = [slice(base + j * FFN_CHUNK, base + (j + 1) * FFN_CHUNK) for base in (0, d_ff)]
        return [(_dot(h2, w_up_ref[:, c]), c) for c in cols]

    f = None
    ups = up_proj(0)
    for j in range(n_chunks):
        ups_next = up_proj(j + 1) if j + 1 < n_chunks else None
        conv = []
        for half, (up, cols) in enumerate(ups):
            r2, r1 = conv_inputs(up, cols, 2 * j + half)
            conv.append(b_conv_ref[:, cols] + r2 * w_conv_ref[0:1, cols] + r1 * w_conv_ref[1:2, cols]
                        + up * w_conv_ref[2:3, cols])
        act = (_silu(conv[0]) * conv[1]).astype(BF16)
        part = _dot(act, w_down_ref[j * FFN_CHUNK:(j + 1) * FFN_CHUNK, :])
        f = part if f is None else f + part
        ups = ups_next
    return _rms(x1 + f, g_final)


def _prompt_back_kernel(x_ref, out_a_ref, o_ref, g_out_b_ref, w_out_ref, g_ffn_ref, w_up_ref,
                        w_conv_ref, b_conv_ref, w_down_ref, g_final_ref,
                        y_ref, conv_ref, tail_ref, stage_ref, *, tiles_per_seq, d_ff, conv_w):
    tm = x_ref.shape[0]
    keep = conv_w - 1

    @pl.when(pl.program_id(0) % tiles_per_seq == 0)
    def _():
        tail_ref[...] = jnp.zeros_like(tail_ref)

    def conv_inputs(up, cols, k):
        stage = stage_ref.at[k % stage_ref.shape[0]]
        stage[0:SUBLANE, :] = tail_ref[:, cols]
        stage[SUBLANE:, :] = up
        tail_ref[:, cols] = up[tm - SUBLANE:, :]
        conv_ref[:, cols] = up[tm - keep:, :]
        return stage[pl.ds(SUBLANE - 2, tm), :], stage[pl.ds(SUBLANE - 1, tm), :]

    y_ref[...] = _back_common(
        x_ref[...], out_a_ref[...], o_ref[...], g_out_b_ref[...], w_out_ref, g_ffn_ref[...], w_up_ref,
        w_conv_ref, b_conv_ref, w_down_ref, g_final_ref[...], conv_inputs, d_ff)


def _sample_front_kernel(x_ref, g_mix_ref, w_in_ref, g_sgu_ref, g_q_ref, g_kv_ref, w_q12_ref,
                         w_ukt_ref, w_sp0_ref, b_sp0_ref, g_out_a_ref, tqc_ref, tqs_ref, tk_ref,
                         out_a_ref, v_ref, ckv_ref, kr_ref, qlat_ref, qrope_ref,
                         *, d_a, q_rank, kv_rank, n_heads, nope_dim, rope_dim):
    u, v, q_full, c_kv, kr = _front_common(
        x_ref[...], g_mix_ref[...], w_in_ref[...], g_sgu_ref[...], g_q_ref[...], g_kv_ref[...],
        w_q12_ref[...], tqc_ref[...], tqs_ref[...], tk_ref[...], d_a, q_rank, kv_rank, n_heads)
    v_ref[...] = v
    ckv_ref[...] = c_kv
    kr_ref[...] = kr[:, :rope_dim]
    s = v * w_sp0_ref[...] + b_sp0_ref[...]
    out_a_ref[...] = _rms(u * s, g_out_a_ref[...]).astype(BF16)
    q_bf = q_full.astype(BF16)
    for hd in range(n_heads):
        q_h = q_bf[:, hd * HEAD_SLOT:(hd + 1) * HEAD_SLOT]
        qlat_ref[hd] = _dot(q_h, w_ukt_ref[hd])
        qrope_ref[hd] = q_full[:, hd * HEAD_SLOT + nope_dim:hd * HEAD_SLOT + nope_dim + rope_dim]


def _sample_attn_kernel(pt_ref, qlat_ref, qrope_ref, cnew_ref, krnew_ref, c_hbm, krt_hbm, o_ref,
                        cbuf, krbuf, s_ref, sem_c, sem_k, *, scale, pages_per_chunk, page_size):
    b = pl.program_id(0)
    n_seq = pl.num_programs(0)
    chunks = pt_ref.shape[1] // pages_per_chunk
    n_slots = cbuf.shape[0]
    c2 = scale * LOG2E
    b_next = jnp.minimum(b + 1, n_seq - 1)

    def owner(ch):
        return (b, ch) if ch < chunks else (b_next, ch - chunks)

    def chunk_copies(seq, ch, slot):
        copies = []
        for i in range(pages_per_chunk):
            page = pt_ref[seq, ch * pages_per_chunk + i]
            rows = pl.ds(i * page_size, page_size)
            copies.append(pltpu.make_async_copy(c_hbm.at[0, page], cbuf.at[slot, rows, :], sem_c.at[slot]))
            copies.append(pltpu.make_async_copy(krt_hbm.at[0, page], krbuf.at[slot, :, rows], sem_k.at[slot]))
        return copies

    def scores(seq, slot):
        return (_dot_nt(qlat_ref[seq].astype(BF16), cbuf[slot].astype(BF16))
                + _dot(qrope_ref[seq].astype(BF16), krbuf[slot].astype(BF16)))

    @pl.when(b == 0)
    def _():
        for ch in range(SAMPLE_DMA_LEAD):
            for cp in chunk_copies(0, ch, ch):
                cp.start()
        for cp in chunk_copies(0, 0, 0):
            cp.wait()
        s_ref[...] = scores(0, 0)

    q_lat = qlat_ref[b]
    c_new = cnew_ref[...]
    m = (jnp.sum(q_lat * c_new, axis=-1, keepdims=True)
         + jnp.sum(qrope_ref[b] * krnew_ref[...], axis=-1, keepdims=True))
    l = jnp.ones_like(m)
    acc = jnp.broadcast_to(c_new, q_lat.shape)
    s = s_ref[...]
    for ch in range(chunks):
        ahead = ch + SAMPLE_DMA_LEAD
        for cp in chunk_copies(*owner(ahead), ahead % n_slots):
            cp.start()
        seq_q, ch_q = owner(ch + 1)
        for cp in chunk_copies(seq_q, ch_q, (ch + 1) % n_slots):
            cp.wait()
        s_next = scores(seq_q, (ch + 1) % n_slots)
        m_new = jnp.maximum(m, jnp.max(s, axis=-1, keepdims=True))
        p = jnp.exp2((s - m_new) * c2)
        alpha = jnp.exp2((m - m_new) * c2)
        l = alpha * l + jnp.sum(p, axis=-1, keepdims=True)
        acc = alpha * acc + _dot(p.astype(BF16), cbuf[ch % n_slots].astype(BF16))
        m, s = m_new, s_next
    s_ref[...] = s
    o_ref[...] = acc / l

    @pl.when(b == n_seq - 1)
    def _():
        for ch in range(1, SAMPLE_DMA_LEAD):
            for cp in chunk_copies(b, ch, (chunks + ch) % n_slots):
                cp.wait()


def _sample_back_kernel(x_ref, out_a_ref, olat_ref, w_uvp_ref, prev_ref, g_out_b_ref, w_out_ref, g_ffn_ref,
                        w_up_ref, w_conv_ref, b_conv_ref, w_down_ref, g_final_ref,
                        y_ref, up_ref, *, n_heads, d_ff):
    o_raw = None
    for hd in range(n_heads):
        part = _dot(olat_ref[hd].astype(BF16), w_uvp_ref[hd])
        o_raw = part if o_raw is None else o_raw + part

    def conv_inputs(up, cols, k):
        del k
        up_ref[:, cols] = up
        return prev_ref[0, :, cols], prev_ref[1, :, cols]

    y_ref[...] = _back_common(
        x_ref[...], out_a_ref[...], o_raw, g_out_b_ref[...], w_out_ref, g_ffn_ref[...], w_up_ref,
        w_conv_ref, b_conv_ref, w_down_ref, g_final_ref[...], conv_inputs, d_ff)


def _rope_tables(pos, rope_dim, nope_dim):
    half = rope_dim // 2
    inv = ROPE_THETA ** (-jnp.arange(half, dtype=F32) / half)
    ang = pos.astype(F32)[:, None] * inv
    cos, sin = jnp.cos(ang), jnp.sin(ang)
    n = pos.shape[0]
    z = lambda w: jnp.zeros((n, w), F32)
    pad = LANE - nope_dim - rope_dim
    tqc = jnp.concatenate([jnp.ones((n, nope_dim), F32), cos, cos, z(pad)], axis=1)
    tqs = jnp.concatenate([z(nope_dim), -sin, sin, z(pad)], axis=1)
    tk = jnp.concatenate([-sin, sin, z(LANE // 2 - rope_dim), cos, cos, z(LANE // 2 - rope_dim)], axis=1)
    return tqc, tqs, tk


def _pack_weights(w_in, w_uq, w_uk, w_uv, d_a, q_rank, kv_rank, rope_dim, nope_dim):
    half = rope_dim // 2
    d_model = w_in.shape[0]
    n_heads = w_uq.shape[1]
    o4 = 2 * d_a + q_rank + kv_rank
    x1, x2 = w_in[:, o4:o4 + half], w_in[:, o4 + half:o4 + rope_dim]
    zc = lambda rows, w: jnp.zeros((rows, w), w_in.dtype)
    gap = LANE // 2 - rope_dim
    w_in_p = jnp.concatenate([w_in[:, :o4], x2, x1, zc(d_model, gap), x1, x2, zc(d_model, gap)], axis=1)
    pad = HEAD_SLOT - nope_dim - rope_dim
    main, swap = [], []
    for hd in range(n_heads):
        nope = w_uq[:, hd, :nope_dim]
        r1, r2 = w_uq[:, hd, nope_dim:nope_dim + half], w_uq[:, hd, nope_dim + half:]
        main.append(jnp.concatenate([nope, r1, r2, zc(q_rank, pad)], axis=1))
        swap.append(jnp.concatenate([zc(q_rank, nope_dim), r2, r1, zc(q_rank, pad)], axis=1))
    w_q12 = jnp.concatenate(main + swap, axis=1)
    w_ukp = jnp.concatenate(
        [jnp.concatenate([w_uk[:, hd, :], zc(kv_rank, HEAD_SLOT - nope_dim)], axis=1) for hd in range(n_heads)],
        axis=1)
    w_ukt = jnp.stack(
        [jnp.concatenate([w_uk[:, hd, :].T, jnp.zeros((HEAD_SLOT - nope_dim, kv_rank), w_uk.dtype)], axis=0)
         for hd in range(n_heads)])
    v_dim = w_uv.shape[2]
    w_uv_flat = w_uv.reshape(kv_rank, n_heads * v_dim)
    w_uvp = jnp.stack(
        [jnp.pad(w_uv[:, hd, :], ((0, 0), (hd * v_dim, (n_heads - 1 - hd) * v_dim))) for hd in range(n_heads)])
    cast = lambda a: a.astype(BF16)
    return cast(w_in_p), cast(w_q12), cast(w_ukp), cast(w_ukt), cast(w_uv_flat), cast(w_uvp)


def kernel(x_prompt, x_sample, cache_kv_latent, cache_k_rope, state_ffn_conv, page_table,
           g_mix, w_in, g_sgu, w_spatial, b_spatial, g_q, w_uq, g_kv, w_uk, w_uv,
           g_out_a, g_out_b, w_out, g_ffn, w_up, w_conv, b_conv, w_down, g_final):
    B, S, D = x_prompt.shape
    DB, T, _ = x_sample.shape
    depth = w_in.shape[0]
    assert depth == 1 and T == 1
    n_heads_a, chunk = w_spatial.shape[1], w_spatial.shape[2]
    d_a = g_sgu.shape[1]
    q_rank, kv_rank = g_q.shape[1], g_kv.shape[1]
    n_heads, nope_dim, v_dim = w_uk.shape[2], w_uk.shape[3], w_uv.shape[3]
    rope_dim = w_uq.shape[3] - nope_dim
    d_b = n_heads * v_dim
    d_ff = w_down.shape[1]
    conv_w = w_conv.shape[1]
    n_pool, page_size = cache_kv_latent.shape[1], cache_kv_latent.shape[2]
    n_pages = page_table.shape[1]
    past_len = n_pages * page_size
    scale = float((nope_dim + rope_dim) ** -0.5)
    assert d_ff % FFN_CHUNK == 0 and S % PROMPT_TILE == 0 and PROMPT_TILE % chunk == 0
    assert n_pages % (SAMPLE_SLOTS * PAGES_PER_CHUNK) == 0 and conv_w == 3
    assert S % ATTN_Q_TILE == 0 and ATTN_Q_TILE % ATTN_K_TILE == 0 and PROMPT_TILE % ATTN_K_TILE == 0

    row2 = lambda a: a.reshape(1, -1)
    w_in_p, w_q12, w_ukp, w_ukt, w_uv_flat, w_uvp = _pack_weights(
        w_in[0], w_uq[0], w_uk[0], w_uv[0], d_a, q_rank, kv_rank, rope_dim, nope_dim)
    w_out_bf, w_up_bf, w_down_bf = w_out[0].astype(BF16), w_up[0].astype(BF16), w_down[0].astype(BF16)
    gm, gs, gq, gkv = row2(g_mix[0]), row2(g_sgu[0]), row2(g_q[0]), row2(g_kv[0])
    goa, gob, gf, gfin = row2(g_out_a[0]), row2(g_out_b[0]), row2(g_ffn[0]), row2(g_final)
    wc, bc = w_conv[0], row2(b_conv[0])
    hd_a = d_a // n_heads_a

    n_rows = B * S
    tm = PROMPT_TILE
    tkb = ATTN_K_TILE
    tiles_per_seq = S // tm
    xp = x_prompt.reshape(n_rows, D)
    tqc, tqs, tk = _rope_tables(jnp.arange(S, dtype=jnp.int32), rope_dim, nope_dim)
    b_sp = jnp.repeat(b_spatial[0][:, :chunk].T, hd_a, axis=1)
    n_in = w_in_p.shape[1]
    nq = n_heads * HEAD_SLOT
    rows = lambda w: pl.BlockSpec((tm, w), lambda i: (i, 0))
    table = pl.BlockSpec((tm, LANE), lambda i: (i % tiles_per_seq, 0))
    front = pl.pallas_call(
        functools.partial(_prompt_front_kernel, d_a=d_a, q_rank=q_rank, kv_rank=kv_rank, n_heads=n_heads,
                          n_heads_a=n_heads_a, chunk=chunk, rope_dim=rope_dim),
        grid=(n_rows // tm,),
        in_specs=[rows(D), _const_spec((1, D)), _const_spec((D, n_in), True), _const_spec((1, d_a)),
                  _const_spec((1, q_rank)), _const_spec((1, kv_rank)), _const_spec((q_rank, 2 * nq), True),
                  _const_spec((kv_rank, nq), True), _const_spec((d_b, kv_rank), True),
                  _const_spec((n_heads_a, chunk, chunk), True), _const_spec((chunk, d_a)), _const_spec((1, d_a)),
                  table, table, table],
        out_specs=[rows(d_a), rows(nq), rows(nq), pl.BlockSpec((tm // tkb, d_b, tkb), lambda i: (i, 0, 0)),
                   rows(kv_rank), rows(rope_dim)],
        out_shape=[jax.ShapeDtypeStruct((n_rows, d_a), BF16), jax.ShapeDtypeStruct((n_rows, nq), BF16),
                   jax.ShapeDtypeStruct((n_rows, nq), BF16), jax.ShapeDtypeStruct((n_rows // tkb, d_b, tkb), BF16),
                   jax.ShapeDtypeStruct((n_rows, kv_rank), F32), jax.ShapeDtypeStruct((n_rows, rope_dim), F32)],
        compiler_params=_params(("arbitrary",)),
        name="prompt_front",
    )
    out_a_p, q_p, k_p, vt_p, ckv_p, kr_p = front(
        xp, gm, w_in_p, gs, gq, gkv, w_q12, w_ukp, w_uv_flat.T, w_spatial[0][:, :chunk, :chunk], b_sp, goa,
        tqc, tqs, tk)

    tq = ATTN_Q_TILE
    nqb = S // tq
    o_p = pl.pallas_call(
        functools.partial(_prompt_attn_kernel, scale=scale, v_head_dim=v_dim),
        grid=(B, nqb),
        in_specs=[pl.BlockSpec((tq, nq), lambda b, i: (b * nqb + i, 0)),
                  pl.BlockSpec((S, nq), lambda b, i: (b, 0)),
                  pl.BlockSpec((S // tkb, d_b, tkb), lambda b, i: (b, 0, 0))],
        out_specs=pl.BlockSpec((tq, d_b), lambda b, i: (b * nqb + i, 0)),
        out_shape=jax.ShapeDtypeStruct((n_rows, d_b), F32),
        compiler_params=_params(("arbitrary", "arbitrary")),
        name="prompt_attn",
    )(q_p, k_p, vt_p)

    back_weights = [_const_spec((1, d_b)), _const_spec((d_a + d_b, D), True), _const_spec((1, D)),
                    _const_spec((D, 2 * d_ff), True), _const_spec((conv_w, 2 * d_ff)), _const_spec((1, 2 * d_ff)),
                    _const_spec((d_ff, D), True), _const_spec((1, D))]
    y_p, conv_p = pl.pallas_call(
        functools.partial(_prompt_back_kernel, tiles_per_seq=tiles_per_seq, d_ff=d_ff, conv_w=conv_w),
        grid=(n_rows // tm,),
        in_specs=[rows(D), rows(d_a), rows(d_b)] + back_weights,
        out_specs=[rows(D), pl.BlockSpec((None, conv_w - 1, 2 * d_ff), lambda i: (i // tiles_per_seq, 0, 0))],
        out_shape=[jax.ShapeDtypeStruct((n_rows, D), F32),
                   jax.ShapeDtypeStruct((B, conv_w - 1, 2 * d_ff), F32)],
        scratch_shapes=[pltpu.VMEM((SUBLANE, 2 * d_ff), F32),
                        pltpu.VMEM((CONV_STAGE_BUFFERS, tm + SUBLANE, FFN_CHUNK), F32)],
        compiler_params=_params(("arbitrary",)),
        name="prompt_back",
    )(xp, out_a_p, o_p, gob, w_out_bf, gf, w_up_bf, wc, bc, w_down_bf, gfin)

    xs = x_sample.reshape(DB, D)
    sqc, sqs, sk = _rope_tables(past_len + jnp.arange(T, dtype=jnp.int32), rope_dim, nope_dim)
    w_sp0 = jnp.repeat(w_spatial[0][:, 0, 0], hd_a).reshape(1, d_a)
    b_sp0 = jnp.repeat(b_spatial[0][:, 0], hd_a).reshape(1, d_a)
    full = lambda *shape: _const_spec(shape, True)
    whole = lambda *shape: _const_spec(shape)
    out_a_s, v_s, ckv_s, kr_s, qlat_h, qrope_h = pl.pallas_call(
        functools.partial(_sample_front_kernel, d_a=d_a, q_rank=q_rank, kv_rank=kv_rank, n_heads=n_heads,
                          nope_dim=nope_dim, rope_dim=rope_dim),
        grid=(1,),
        in_specs=[full(DB, D), full(1, D), full(D, n_in), full(1, d_a), full(1, q_rank), full(1, kv_rank),
                  full(q_rank, 2 * nq), full(n_heads, HEAD_SLOT, kv_rank), full(1, d_a), full(1, d_a),
                  full(1, d_a), full(1, LANE), full(1, LANE), full(1, LANE)],
        out_specs=[whole(DB, d_a), whole(DB, d_a), whole(DB, kv_rank), whole(DB, rope_dim),
                   whole(n_heads, DB, kv_rank), whole(n_heads, DB, rope_dim)],
        out_shape=[jax.ShapeDtypeStruct((DB, d_a), BF16), jax.ShapeDtypeStruct((DB, d_a), F32),
                   jax.ShapeDtypeStruct((DB, kv_rank), F32), jax.ShapeDtypeStruct((DB, rope_dim), F32),
                   jax.ShapeDtypeStruct((n_heads, DB, kv_rank), F32),
                   jax.ShapeDtypeStruct((n_heads, DB, rope_dim), F32)],
        compiler_params=_params(("arbitrary",)),
        name="sample_front",
    )(xs, gm, w_in_p, gs, gq, gkv, w_q12, w_ukt, w_sp0, b_sp0, goa, sqc, sqs, sk)

    P = PAGES_PER_CHUNK
    chunk_keys = P * page_size
    any_space = pl.BlockSpec(memory_space=pl.ANY)
    per_seq = lambda *tail: pl.BlockSpec((None,) + tail, lambda b, pt: (b,) + (0,) * len(tail))
    cache_krt = jnp.swapaxes(cache_k_rope, 2, 3)
    o_lat = pl.pallas_call(
        functools.partial(_sample_attn_kernel, scale=scale, pages_per_chunk=P, page_size=page_size),
        grid_spec=pltpu.PrefetchScalarGridSpec(
            num_scalar_prefetch=1,
            grid=(DB,),
            in_specs=[whole(DB, n_heads, kv_rank), whole(DB, n_heads, rope_dim), per_seq(1, kv_rank),
                      per_seq(1, rope_dim), any_space, any_space],
            out_specs=per_seq(n_heads, kv_rank),
            scratch_shapes=[pltpu.VMEM((SAMPLE_SLOTS, chunk_keys, kv_rank), F32),
                            pltpu.VMEM((SAMPLE_SLOTS, rope_dim, chunk_keys), F32),
                            pltpu.VMEM((n_heads, chunk_keys), F32),
                            pltpu.SemaphoreType.DMA((SAMPLE_SLOTS,)), pltpu.SemaphoreType.DMA((SAMPLE_SLOTS,))],
        ),
        out_shape=jax.ShapeDtypeStruct((DB, n_heads, kv_rank), F32),
        compiler_params=_params(("arbitrary",)),
        name="sample_attn",
    )(page_table, qlat_h.transpose(1, 0, 2), qrope_h.transpose(1, 0, 2), ckv_s.reshape(DB, 1, kv_rank),
      kr_s.reshape(DB, 1, rope_dim), cache_kv_latent, cache_krt)

    prev = state_ffn_conv[0].transpose(1, 0, 2)
    y_s, up_s = pl.pallas_call(
        functools.partial(_sample_back_kernel, n_heads=n_heads, d_ff=d_ff),
        grid=(1,),
        in_specs=[full(DB, D), full(DB, d_a), full(n_heads, DB, kv_rank), full(n_heads, kv_rank, d_b),
                  full(conv_w - 1, DB, 2 * d_ff), full(1, d_b), full(d_a + d_b, D), full(1, D),
                  full(D, 2 * d_ff), full(conv_w, 2 * d_ff), full(1, 2 * d_ff), full(d_ff, D), full(1, D)],
        out_specs=[whole(DB, D), whole(DB, 2 * d_ff)],
        out_shape=[jax.ShapeDtypeStruct((DB, D), F32), jax.ShapeDtypeStruct((DB, 2 * d_ff), F32)],
        compiler_params=_params(("arbitrary",)),
        name="sample_back",
    )(xs, out_a_s, o_lat.transpose(1, 0, 2), w_uvp, prev, gob, w_out_bf, gf, w_up_bf, wc, bc, w_down_bf, gfin)

    conv_s = jnp.concatenate([state_ffn_conv[0][:, 1:, :], up_s[:, None, :]], axis=1)
    return (y_p.reshape(B, S, D), y_s.reshape(DB, T, D),
            ckv_p.reshape(1, B, S, kv_rank), kr_p.reshape(1, B, S, rope_dim),
            conv_p.reshape(1, B, conv_w - 1, 2 * d_ff),
            ckv_s.reshape(1, DB, T, kv_rank), kr_s.reshape(1, DB, T, rope_dim),
            v_s.reshape(1, DB, T, d_a), conv_s.reshape(1, DB, conv_w - 1, 2 * d_ff))
```

```python
import functools

import jax
import jax.numpy as jnp
import numpy as np
from jax import lax
from jax.experimental import pallas as pl
from jax.experimental.pallas import tpu as pltpu

F32 = jnp.float32
BF16 = jnp.bfloat16

EPS = 1e-6
ROPE_THETA = 10000.0
LANE = 128
SUBLANE = 8
HEAD_SLOT = LANE
PROMPT_TILE = 512
ATTN_Q_TILE = 512
ATTN_K_TILE = 256
ATTN_QK_LEAD = 2
FFN_CHUNK = 256
CONV_STAGE_BUFFERS = 4
FFN_UP_LEAD = 2
PAGES_PER_CHUNK = 16
SAMPLE_DMA_LEAD = 2
SAMPLE_SLOTS = SAMPLE_DMA_LEAD + 2
VMEM_LIMIT = 56 * 1024 * 1024
LOG2E = float(np.log2(np.e))


def _rms(x, g):
    r = lax.rsqrt(jnp.mean(x * x, axis=-1, keepdims=True) + EPS)
    return (x * r) * g


def _gelu(x):
    return 0.5 * x * (1.0 + lax.erf(x * np.float32(np.sqrt(0.5))))


def _silu(x):
    return x * (1.0 / (1.0 + jnp.exp(-x)))


def _dot(a, b):
    return jnp.dot(a, b, preferred_element_type=F32)


def _dot_nt(a, b):
    return lax.dot_general(a, b, (((1,), (1,)), ((), ())), preferred_element_type=F32)


def _const_spec(shape, single_buffer=False):
    nd = len(shape)
    mode = pl.Buffered(1) if single_buffer else None
    return pl.BlockSpec(shape, lambda *_: (0,) * nd, pipeline_mode=mode)


def _params(semantics):
    return pltpu.CompilerParams(dimension_semantics=semantics, vmem_limit_bytes=VMEM_LIMIT)


def _front_common(x, g_mix, w_in, g_sgu, g_q, g_kv, w_q12, tqc, tqs, tk, d_a, q_rank, kv_rank, n_heads):
    h = _rms(x, g_mix).astype(BF16)
    proj = _dot(h, w_in)
    o1, o2 = d_a, 2 * d_a
    o3 = o2 + q_rank
    o4 = o3 + kv_rank
    u = _gelu(proj[:, :o1])
    v = _rms(_gelu(proj[:, o1:o2]), g_sgu)
    c_q = _rms(proj[:, o2:o3], g_q)
    c_kv = _rms(proj[:, o3:o4], g_kv)
    y = proj[:, o4:o4 + LANE] * tk
    kr = y + pltpu.roll(y, LANE // 2, 1)
    q12 = _dot(c_q.astype(BF16), w_q12)
    nq = n_heads * HEAD_SLOT
    q_parts = []
    for hd in range(n_heads):
        a = q12[:, hd * HEAD_SLOT:(hd + 1) * HEAD_SLOT]
        b = q12[:, nq + hd * HEAD_SLOT:nq + (hd + 1) * HEAD_SLOT]
        q_parts.append(a * tqc + b * tqs)
    return u, v, jnp.concatenate(q_parts, axis=1), c_kv, kr


def _prompt_front_kernel(x_ref, g_mix_ref, w_in_ref, g_sgu_ref, g_q_ref, g_kv_ref, w_q12_ref,
                         w_ukp_ref, w_uvt_ref, w_sp_ref, b_sp_ref, g_out_a_ref,
                         tqc_ref, tqs_ref, tk_ref,
                         out_a_ref, q_ref, k_ref, vt_ref, ckv_ref, kr_ref,
                         *, d_a, q_rank, kv_rank, n_heads, n_heads_a, chunk, rope_dim):
    tm = x_ref.shape[0]
    u, v, q_full, c_kv, kr = _front_common(
        x_ref[...], g_mix_ref[...], w_in_ref[...], g_sgu_ref[...], g_q_ref[...], g_kv_ref[...],
        w_q12_ref[...], tqc_ref[...], tqs_ref[...], tk_ref[...], d_a, q_rank, kv_rank, n_heads)
    q_ref[...] = q_full.astype(BF16)
    ckv_ref[...] = c_kv
    kr_ref[...] = kr[:, :rope_dim]

    c_bf = c_kv.astype(BF16)
    lane = lax.broadcasted_iota(jnp.int32, (tm, LANE), 1)
    kr_slot = jnp.where(lane >= LANE // 2, kr, 0.0)
    k_nope = _dot(c_bf, w_ukp_ref[...])
    k_parts = [k_nope[:, hd * HEAD_SLOT:(hd + 1) * HEAD_SLOT] + kr_slot for hd in range(n_heads)]
    k_ref[...] = jnp.concatenate(k_parts, axis=1).astype(BF16)
    v_t = _dot_nt(w_uvt_ref[...], c_bf).astype(BF16)
    tk = vt_ref.shape[2]
    for c in range(tm // tk):
        vt_ref[c] = v_t[:, c * tk:(c + 1) * tk]

    hd_a = d_a // n_heads_a
    heads_per_slab = LANE // hd_a
    row = lax.broadcasted_iota(jnp.int32, (chunk, chunk), 0)
    col = lax.broadcasted_iota(jnp.int32, (chunk, chunk), 1)
    w_causal = [jnp.where(row >= col, w_sp_ref[hd], 0.0).astype(BF16) for hd in range(n_heads_a)]
    v_bf = v.astype(BF16)
    lane_c = lax.broadcasted_iota(jnp.int32, (chunk, LANE), 1)
    gate_rows = []
    for c in range(tm // chunk):
        slabs = []
        for j in range(d_a // LANE):
            x_slab = v_bf[c * chunk:(c + 1) * chunk, j * LANE:(j + 1) * LANE]
            mixed = None
            for t in range(heads_per_slab):
                s_t = _dot(w_causal[j * heads_per_slab + t], x_slab)
                in_head = (lane_c >= t * hd_a) & (lane_c < (t + 1) * hd_a)
                mixed = s_t if mixed is None else jnp.where(in_head, s_t, mixed)
            slabs.append(mixed)
        gate_rows.append(jnp.concatenate(slabs, axis=1) + b_sp_ref[...])
    s = jnp.concatenate(gate_rows, axis=0)
    out_a_ref[...] = _rms(u * s, g_out_a_ref[...]).astype(BF16)


def _prompt_attn_kernel(q_ref, k_ref, vt_ref, o_ref, *, scale, v_head_dim):
    tq = q_ref.shape[0]
    tk = vt_ref.shape[2]
    i = pl.program_id(1)
    n_heads = q_ref.shape[1] // HEAD_SLOT
    c2 = scale * LOG2E
    q_t = [q_ref[:, t * HEAD_SLOT:(t + 1) * HEAD_SLOT].astype(F32).T.astype(BF16)
           for t in range(n_heads)]
    qpos = i * tq + lax.broadcasted_iota(jnp.int32, (tk, tq), 1)
    kloc = lax.broadcasted_iota(jnp.int32, (tk, tq), 0)

    def step(j, carry, masked):
        start = pl.multiple_of(j * tk, tk)
        v_t = vt_ref[j]

        def scores(t):
            ks = k_ref[pl.ds(start, tk), t * HEAD_SLOT:(t + 1) * HEAD_SLOT]
            s_t = _dot(ks, q_t[t])
            return jnp.where(start + kloc <= qpos, s_t, -jnp.inf) if masked else s_t

        s_ahead = [scores(t) for t in range(min(ATTN_QK_LEAD, n_heads))]
        new = []
        for t in range(n_heads):
            m, l, acc = carry[t]
            if t + ATTN_QK_LEAD < n_heads:
                s_ahead.append(scores(t + ATTN_QK_LEAD))
            s_t = s_ahead[t]
            m_new = jnp.maximum(m, jnp.max(s_t, axis=0, keepdims=True))
            p_t = jnp.exp2((s_t - m_new) * c2)
            alpha = jnp.exp2((m - m_new) * c2)
            l_new = alpha * l + jnp.sum(p_t, axis=0, keepdims=True)
            acc_new = alpha * acc + _dot(v_t[t * v_head_dim:(t + 1) * v_head_dim, :], p_t.astype(BF16))
            new.append((m_new, l_new, acc_new))
        return tuple(new)

    init = tuple((jnp.full((1, tq), -jnp.inf, F32), jnp.zeros((1, tq), F32), jnp.zeros((v_head_dim, tq), F32))
                 for _ in range(n_heads))
    n_full = i * (tq // tk)
    carry = lax.fori_loop(0, n_full, functools.partial(step, masked=False), init)
    for d in range(tq // tk):
        carry = step(n_full + d, carry, True)
    o_t = jnp.concatenate([acc / l for _, l, acc in carry], axis=0)
    o_ref[...] = o_t.T


def _back_common(x, out_a, o_raw, g_out_b, w_out_ref, g_ffn, w_up_ref, w_conv_ref, b_conv_ref,
                 w_down_ref, g_final, conv_inputs, d_ff):
    d_a = out_a.shape[1]
    ob = _rms(o_raw, g_out_b).astype(BF16)
    mix = _dot(out_a, w_out_ref[:d_a, :]) + _dot(ob, w_out_ref[d_a:, :])
    x1 = x + mix
    h2 = _rms(x1, g_ffn).astype(BF16)
    n_chunks = d_ff // FFN_CHUNK

    def up_proj(j):
        cols = [slice(base + j * FFN_CHUNK, base + (j + 1) * FFN_CHUNK) for base in (0, d_ff)]
        return [(_dot(h2, w_up_ref[:, c]), c) for c in cols]

    f = None
    pending = [up_proj(j) for j in range(min(FFN_UP_LEAD, n_chunks))]
    for j in range(n_chunks):
        if j + FFN_UP_LEAD < n_chunks:
            pending.append(up_proj(j + FFN_UP_LEAD))
        conv = []
        for half, (up, cols) in enumerate(pending[j]):
            r2, r1 = conv_inputs(up, cols, 2 * j + half)
            conv.append(b_conv_ref[:, cols] + r2 * w_conv_ref[0:1, cols] + r1 * w_conv_ref[1:2, cols]
                        + up * w_conv_ref[2:3, cols])
        act = (_silu(conv[0]) * conv[1]).astype(BF16)
        part = _dot(act, w_down_ref[j * FFN_CHUNK:(j + 1) * FFN_CHUNK, :])
        f = part if f is None else f + part
        pending[j] = None
    return _rms(x1 + f, g_final)


def _prompt_back_kernel(x_ref, out_a_ref, o_ref, g_out_b_ref, w_out_ref, g_ffn_ref, w_up_ref,
                        w_conv_ref, b_conv_ref, w_down_ref, g_final_ref,
                        y_ref, conv_ref, tail_ref, stage_ref, *, tiles_per_seq, d_ff, conv_w):
    tm = x_ref.shape[0]
    keep = conv_w - 1

    @pl.when(pl.program_id(0) % tiles_per_seq == 0)
    def _():
        tail_ref[...] = jnp.zeros_like(tail_ref)

    def conv_inputs(up, cols, k):
        stage = stage_ref.at[k % stage_ref.shape[0]]
        stage[0:SUBLANE, :] = tail_ref[:, cols]
        stage[SUBLANE:, :] = up
        tail_ref[:, cols] = up[tm - SUBLANE:, :]
        conv_ref[:, cols] = up[tm - keep:, :]
        return stage[pl.ds(SUBLANE - 2, tm), :], stage[pl.ds(SUBLANE - 1, tm), :]

    y_ref[...] = _back_common(
        x_ref[...], out_a_ref[...], o_ref[...], g_out_b_ref[...], w_out_ref, g_ffn_ref[...], w_up_ref,
        w_conv_ref, b_conv_ref, w_down_ref, g_final_ref[...], conv_inputs, d_ff)


def _sample_front_kernel(x_ref, g_mix_ref, w_in_ref, g_sgu_ref, g_q_ref, g_kv_ref, w_q12_ref,
                         w_ukt_ref, w_sp0_ref, b_sp0_ref, g_out_a_ref, tqc_ref, tqs_ref, tk_ref,
                         out_a_ref, v_ref, ckv_ref, kr_ref, qlat_ref, qrope_ref,
                         *, d_a, q_rank, kv_rank, n_heads, nope_dim, rope_dim):
    u, v, q_full, c_kv, kr = _front_common(
        x_ref[...], g_mix_ref[...], w_in_ref[...], g_sgu_ref[...], g_q_ref[...], g_kv_ref[...],
        w_q12_ref[...], tqc_ref[...], tqs_ref[...], tk_ref[...], d_a, q_rank, kv_rank, n_heads)
    v_ref[...] = v
    ckv_ref[...] = c_kv
    kr_ref[...] = kr[:, :rope_dim]
    s = v * w_sp0_ref[...] + b_sp0_ref[...]
    out_a_ref[...] = _rms(u * s, g_out_a_ref[...]).astype(BF16)
    q_bf = q_full.astype(BF16)
    for hd in range(n_heads):
        q_h = q_bf[:, hd * HEAD_SLOT:(hd + 1) * HEAD_SLOT]
        qlat_ref[hd] = _dot(q_h, w_ukt_ref[hd])
        qrope_ref[hd] = q_full[:, hd * HEAD_SLOT + nope_dim:hd * HEAD_SLOT + nope_dim + rope_dim]


def _sample_attn_kernel(pt_ref, qlat_ref, qrope_ref, qbd_ref, qbd_next_ref, cnew_ref, krnew_ref, c_hbm, krt_hbm,
                        o_ref, cbuf, krbuf, s_ref, sem_c, sem_k, *, scale, pages_per_chunk, page_size):
    b = pl.program_id(0)
    n_seq = pl.num_programs(0)
    chunks = pt_ref.shape[1] // pages_per_chunk
    n_slots = cbuf.shape[0]
    n_heads = qlat_ref.shape[1]
    rope_dim = krt_hbm.shape[2]
    c2 = scale * LOG2E
    b_next = jnp.minimum(b + 1, n_seq - 1)

    def owner(ch):
        return (b, ch, qbd_ref) if ch < chunks else (b_next, ch - chunks, qbd_next_ref)

    def chunk_copies(seq, ch, slot):
        copies = []
        for i in range(pages_per_chunk):
            page = pt_ref[seq, ch * pages_per_chunk + i]
            copies.append(pltpu.make_async_copy(
                c_hbm.at[0, page], cbuf.at[slot, pl.ds(i * page_size, page_size), :], sem_c.at[slot]))
            copies.append(pltpu.make_async_copy(
                krt_hbm.at[0, page], krbuf.at[slot, pl.ds(i * rope_dim, rope_dim), :], sem_k.at[slot]))
        return copies

    def start(copies):
        for n, cp in enumerate(copies):
            cp.start(priority=(n // 2) % 2)

    def scores(seq, qbd, slot):
        s_lat = _dot_nt(qlat_ref[seq].astype(BF16), cbuf[slot].astype(BF16))
        blocks = _dot(qbd[...], krbuf[slot].astype(BF16))
        s_rope = jnp.concatenate(
            [blocks[i * n_heads:(i + 1) * n_heads, :] for i in range(pages_per_chunk)], axis=1)
        return s_lat + s_rope

    @pl.when(b == 0)
    def _():
        for ch in range(SAMPLE_DMA_LEAD):
            start(chunk_copies(0, ch, ch))
        for cp in chunk_copies(0, 0, 0):
            cp.wait()
        s_ref[...] = scores(0, qbd_ref, 0)

    q_lat = qlat_ref[b]
    c_new = cnew_ref[...]
    m = (jnp.sum(q_lat * c_new, axis=-1, keepdims=True)
         + jnp.sum(qrope_ref[b] * krnew_ref[...], axis=-1, keepdims=True))
    l = jnp.ones_like(m)
    acc = jnp.broadcast_to(c_new, q_lat.shape)
    s = s_ref[...]
    for ch in range(chunks):
        ahead = ch + SAMPLE_DMA_LEAD
        seq_a, ch_a, _ = owner(ahead)
        start(chunk_copies(seq_a, ch_a, ahead % n_slots))
        seq_q, ch_q, qbd_q = owner(ch + 1)
        for cp in chunk_copies(seq_q, ch_q, (ch + 1) % n_slots):
            cp.wait()
        s_next = scores(seq_q, qbd_q, (ch + 1) % n_slots)
        m_new = jnp.maximum(m, jnp.max(s, axis=-1, keepdims=True))
        p = jnp.exp2((s - m_new) * c2)
        alpha = jnp.exp2((m - m_new) * c2)
        l = alpha * l + jnp.sum(p, axis=-1, keepdims=True)
        acc = alpha * acc + _dot(p.astype(BF16), cbuf[ch % n_slots].astype(BF16))
        m, s = m_new, s_next
    s_ref[...] = s
    o_ref[...] = acc / l

    @pl.when(b == n_seq - 1)
    def _():
        for ch in range(1, SAMPLE_DMA_LEAD):
            for cp in chunk_copies(b, ch, (chunks + ch) % n_slots):
                cp.wait()


def _sample_back_kernel(x_ref, out_a_ref, olat_ref, w_uvp_ref, prev_ref, g_out_b_ref, w_out_ref, g_ffn_ref,
                        w_up_ref, w_conv_ref, b_conv_ref, w_down_ref, g_final_ref,
                        y_ref, up_ref, *, n_heads, d_ff):
    o_raw = None
    for hd in range(n_heads):
        part = _dot(olat_ref[hd].astype(BF16), w_uvp_ref[hd])
        o_raw = part if o_raw is None else o_raw + part

    def conv_inputs(up, cols, k):
        del k
        up_ref[:, cols] = up
        return prev_ref[0, :, cols], prev_ref[1, :, cols]

    y_ref[...] = _back_common(
        x_ref[...], out_a_ref[...], o_raw, g_out_b_ref[...], w_out_ref, g_ffn_ref[...], w_up_ref,
        w_conv_ref, b_conv_ref, w_down_ref, g_final_ref[...], conv_inputs, d_ff)


def _rope_tables(pos, rope_dim, nope_dim):
    half = rope_dim // 2
    inv = ROPE_THETA ** (-jnp.arange(half, dtype=F32) / half)
    ang = pos.astype(F32)[:, None] * inv
    cos, sin = jnp.cos(ang), jnp.sin(ang)
    n = pos.shape[0]
    z = lambda w: jnp.zeros((n, w), F32)
    pad = LANE - nope_dim - rope_dim
    tqc = jnp.concatenate([jnp.ones((n, nope_dim), F32), cos, cos, z(pad)], axis=1)
    tqs = jnp.concatenate([z(nope_dim), -sin, sin, z(pad)], axis=1)
    tk = jnp.concatenate([-sin, sin, z(LANE // 2 - rope_dim), cos, cos, z(LANE // 2 - rope_dim)], axis=1)
    return tqc, tqs, tk


def _pack_weights(w_in, w_uq, w_uk, w_uv, d_a, q_rank, kv_rank, rope_dim, nope_dim):
    half = rope_dim // 2
    d_model = w_in.shape[0]
    n_heads = w_uq.shape[1]
    o4 = 2 * d_a + q_rank + kv_rank
    x1, x2 = w_in[:, o4:o4 + half], w_in[:, o4 + half:o4 + rope_dim]
    zc = lambda rows, w: jnp.zeros((rows, w), w_in.dtype)
    gap = LANE // 2 - rope_dim
    w_in_p = jnp.concatenate([w_in[:, :o4], x2, x1, zc(d_model, gap), x1, x2, zc(d_model, gap)], axis=1)
    pad = HEAD_SLOT - nope_dim - rope_dim
    main, swap = [], []
    for hd in range(n_heads):
        nope = w_uq[:, hd, :nope_dim]
        r1, r2 = w_uq[:, hd, nope_dim:nope_dim + half], w_uq[:, hd, nope_dim + half:]
        main.append(jnp.concatenate([nope, r1, r2, zc(q_rank, pad)], axis=1))
        swap.append(jnp.concatenate([zc(q_rank, nope_dim), r2, r1, zc(q_rank, pad)], axis=1))
    w_q12 = jnp.concatenate(main + swap, axis=1)
    w_ukp = jnp.concatenate(
        [jnp.concatenate([w_uk[:, hd, :], zc(kv_rank, HEAD_SLOT - nope_dim)], axis=1) for hd in range(n_heads)],
        axis=1)
    w_ukt = jnp.stack(
        [jnp.concatenate([w_uk[:, hd, :].T, jnp.zeros((HEAD_SLOT - nope_dim, kv_rank), w_uk.dtype)], axis=0)
         for hd in range(n_heads)])
    v_dim = w_uv.shape[2]
    w_uv_flat = w_uv.reshape(kv_rank, n_heads * v_dim)
    w_uvp = jnp.stack(
        [jnp.pad(w_uv[:, hd, :], ((0, 0), (hd * v_dim, (n_heads - 1 - hd) * v_dim))) for hd in range(n_heads)])
    cast = lambda a: a.astype(BF16)
    return cast(w_in_p), cast(w_q12), cast(w_ukp), cast(w_ukt), cast(w_uv_flat), cast(w_uvp)


def kernel(x_prompt, x_sample, cache_kv_latent, cache_k_rope, state_ffn_conv, page_table,
           g_mix, w_in, g_sgu, w_spatial, b_spatial, g_q, w_uq, g_kv, w_uk, w_uv,
           g_out_a, g_out_b, w_out, g_ffn, w_up, w_conv, b_conv, w_down, g_final):
    B, S, D = x_prompt.shape
    DB, T, _ = x_sample.shape
    depth = w_in.shape[0]
    assert depth == 1 and T == 1
    n_heads_a, chunk = w_spatial.shape[1], w_spatial.shape[2]
    d_a = g_sgu.shape[1]
    q_rank, kv_rank = g_q.shape[1], g_kv.shape[1]
    n_heads, nope_dim, v_dim = w_uk.shape[2], w_uk.shape[3], w_uv.shape[3]
    rope_dim = w_uq.shape[3] - nope_dim
    d_b = n_heads * v_dim
    d_ff = w_down.shape[1]
    conv_w = w_conv.shape[1]
    n_pool, page_size = cache_kv_latent.shape[1], cache_kv_latent.shape[2]
    n_pages = page_table.shape[1]
    past_len = n_pages * page_size
    scale = float((nope_dim + rope_dim) ** -0.5)
    assert d_ff % FFN_CHUNK == 0 and S % PROMPT_TILE == 0 and PROMPT_TILE % chunk == 0
    assert n_pages % (SAMPLE_SLOTS * PAGES_PER_CHUNK) == 0 and conv_w == 3
    assert S % ATTN_Q_TILE == 0 and ATTN_Q_TILE % ATTN_K_TILE == 0 and PROMPT_TILE % ATTN_K_TILE == 0

    row2 = lambda a: a.reshape(1, -1)
    w_in_p, w_q12, w_ukp, w_ukt, w_uv_flat, w_uvp = _pack_weights(
        w_in[0], w_uq[0], w_uk[0], w_uv[0], d_a, q_rank, kv_rank, rope_dim, nope_dim)
    w_out_bf, w_up_bf, w_down_bf = w_out[0].astype(BF16), w_up[0].astype(BF16), w_down[0].astype(BF16)
    gm, gs, gq, gkv = row2(g_mix[0]), row2(g_sgu[0]), row2(g_q[0]), row2(g_kv[0])
    goa, gob, gf, gfin = row2(g_out_a[0]), row2(g_out_b[0]), row2(g_ffn[0]), row2(g_final)
    wc, bc = w_conv[0], row2(b_conv[0])
    hd_a = d_a // n_heads_a

    n_rows = B * S
    tm = PROMPT_TILE
    tkb = ATTN_K_TILE
    tiles_per_seq = S // tm
    xp = x_prompt.reshape(n_rows, D)
    tqc, tqs, tk = _rope_tables(jnp.arange(S, dtype=jnp.int32), rope_dim, nope_dim)
    b_sp = jnp.repeat(b_spatial[0][:, :chunk].T, hd_a, axis=1)
    n_in = w_in_p.shape[1]
    nq = n_heads * HEAD_SLOT
    rows = lambda w: pl.BlockSpec((tm, w), lambda i: (i, 0))
    table = pl.BlockSpec((tm, LANE), lambda i: (i % tiles_per_seq, 0))
    front = pl.pallas_call(
        functools.partial(_prompt_front_kernel, d_a=d_a, q_rank=q_rank, kv_rank=kv_rank, n_heads=n_heads,
                          n_heads_a=n_heads_a, chunk=chunk, rope_dim=rope_dim),
        grid=(n_rows // tm,),
        in_specs=[rows(D), _const_spec((1, D)), _const_spec((D, n_in), True), _const_spec((1, d_a)),
                  _const_spec((1, q_rank)), _const_spec((1, kv_rank)), _const_spec((q_rank, 2 * nq), True),
                  _const_spec((kv_rank, nq), True), _const_spec((d_b, kv_rank), True),
                  _const_spec((n_heads_a, chunk, chunk), True), _const_spec((chunk, d_a)), _const_spec((1, d_a)),
                  table, table, table],
        out_specs=[rows(d_a), rows(nq), rows(nq), pl.BlockSpec((tm // tkb, d_b, tkb), lambda i: (i, 0, 0)),
                   rows(kv_rank), rows(rope_dim)],
        out_shape=[jax.ShapeDtypeStruct((n_rows, d_a), BF16), jax.ShapeDtypeStruct((n_rows, nq), BF16),
                   jax.ShapeDtypeStruct((n_rows, nq), BF16), jax.ShapeDtypeStruct((n_rows // tkb, d_b, tkb), BF16),
                   jax.ShapeDtypeStruct((n_rows, kv_rank), F32), jax.ShapeDtypeStruct((n_rows, rope_dim), F32)],
        compiler_params=_params(("arbitrary",)),
        name="prompt_front",
    )
    out_a_p, q_p, k_p, vt_p, ckv_p, kr_p = front(
        xp, gm, w_in_p, gs, gq, gkv, w_q12, w_ukp, w_uv_flat.T, w_spatial[0][:, :chunk, :chunk], b_sp, goa,
        tqc, tqs, tk)

    tq = ATTN_Q_TILE
    nqb = S // tq
    o_p = pl.pallas_call(
        functools.partial(_prompt_attn_kernel, scale=scale, v_head_dim=v_dim),
        grid=(B, nqb),
        in_specs=[pl.BlockSpec((tq, nq), lambda b, i: (b * nqb + i, 0)),
                  pl.BlockSpec((S, nq), lambda b, i: (b, 0)),
                  pl.BlockSpec((S // tkb, d_b, tkb), lambda b, i: (b, 0, 0))],
        out_specs=pl.BlockSpec((tq, d_b), lambda b, i: (b * nqb + i, 0)),
        out_shape=jax.ShapeDtypeStruct((n_rows, d_b), F32),
        compiler_params=_params(("arbitrary", "arbitrary")),
        name="prompt_attn",
    )(q_p, k_p, vt_p)

    back_weights = [_const_spec((1, d_b)), _const_spec((d_a + d_b, D), True), _const_spec((1, D)),
                    _const_spec((D, 2 * d_ff), True), _const_spec((conv_w, 2 * d_ff)), _const_spec((1, 2 * d_ff)),
                    _const_spec((d_ff, D), True), _const_spec((1, D))]
    y_p, conv_p = pl.pallas_call(
        functools.partial(_prompt_back_kernel, tiles_per_seq=tiles_per_seq, d_ff=d_ff, conv_w=conv_w),
        grid=(n_rows // tm,),
        in_specs=[rows(D), rows(d_a), rows(d_b)] + back_weights,
        out_specs=[rows(D), pl.BlockSpec((None, conv_w - 1, 2 * d_ff), lambda i: (i // tiles_per_seq, 0, 0))],
        out_shape=[jax.ShapeDtypeStruct((n_rows, D), F32),
                   jax.ShapeDtypeStruct((B, conv_w - 1, 2 * d_ff), F32)],
        scratch_shapes=[pltpu.VMEM((SUBLANE, 2 * d_ff), F32),
                        pltpu.VMEM((CONV_STAGE_BUFFERS, tm + SUBLANE, FFN_CHUNK), F32)],
        compiler_params=_params(("arbitrary",)),
        name="prompt_back",
    )(xp, out_a_p, o_p, gob, w_out_bf, gf, w_up_bf, wc, bc, w_down_bf, gfin)

    xs = x_sample.reshape(DB, D)
    sqc, sqs, sk = _rope_tables(past_len + jnp.arange(T, dtype=jnp.int32), rope_dim, nope_dim)
    w_sp0 = jnp.repeat(w_spatial[0][:, 0, 0], hd_a).reshape(1, d_a)
    b_sp0 = jnp.repeat(b_spatial[0][:, 0], hd_a).reshape(1, d_a)
    full = lambda *shape: _const_spec(shape, True)
    whole = lambda *shape: _const_spec(shape)
    out_a_s, v_s, ckv_s, kr_s, qlat_h, qrope_h = pl.pallas_call(
        functools.partial(_sample_front_kernel, d_a=d_a, q_rank=q_rank, kv_rank=kv_rank, n_heads=n_heads,
                          nope_dim=nope_dim, rope_dim=rope_dim),
        grid=(1,),
        in_specs=[full(DB, D), full(1, D), full(D, n_in), full(1, d_a), full(1, q_rank), full(1, kv_rank),
                  full(q_rank, 2 * nq), full(n_heads, HEAD_SLOT, kv_rank), full(1, d_a), full(1, d_a),
                  full(1, d_a), full(1, LANE), full(1, LANE), full(1, LANE)],
        out_specs=[whole(DB, d_a), whole(DB, d_a), whole(DB, kv_rank), whole(DB, rope_dim),
                   whole(n_heads, DB, kv_rank), whole(n_heads, DB, rope_dim)],
        out_shape=[jax.ShapeDtypeStruct((DB, d_a), BF16), jax.ShapeDtypeStruct((DB, d_a), F32),
                   jax.ShapeDtypeStruct((DB, kv_rank), F32), jax.ShapeDtypeStruct((DB, rope_dim), F32),
                   jax.ShapeDtypeStruct((n_heads, DB, kv_rank), F32),
                   jax.ShapeDtypeStruct((n_heads, DB, rope_dim), F32)],
        compiler_params=_params(("arbitrary",)),
        name="sample_front",
    )(xs, gm, w_in_p, gs, gq, gkv, w_q12, w_ukt, w_sp0, b_sp0, goa, sqc, sqs, sk)

    P = PAGES_PER_CHUNK
    chunk_keys = P * page_size
    any_space = pl.BlockSpec(memory_space=pl.ANY)
    per_seq = lambda *tail: pl.BlockSpec((None,) + tail, lambda b, pt: (b,) + (0,) * len(tail))
    cache_krt = jnp.swapaxes(cache_k_rope, 2, 3)
    qrope_s = qrope_h.transpose(1, 0, 2)
    q_bd = jnp.einsum("ij,bhd->bihjd", jnp.eye(P, dtype=F32), qrope_s).reshape(
        DB, P * n_heads, P * rope_dim).astype(BF16)
    bd_block = (None, P * n_heads, P * rope_dim)
    o_lat = pl.pallas_call(
        functools.partial(_sample_attn_kernel, scale=scale, pages_per_chunk=P, page_size=page_size),
        grid_spec=pltpu.PrefetchScalarGridSpec(
            num_scalar_prefetch=1,
            grid=(DB,),
            in_specs=[whole(DB, n_heads, kv_rank), whole(DB, n_heads, rope_dim),
                      pl.BlockSpec(bd_block, lambda b, pt: (b, 0, 0)),
                      pl.BlockSpec(bd_block, lambda b, pt: (jnp.minimum(b + 1, DB - 1), 0, 0)),
                      per_seq(1, kv_rank), per_seq(1, rope_dim), any_space, any_space],
            out_specs=per_seq(n_heads, kv_rank),
            scratch_shapes=[pltpu.VMEM((SAMPLE_SLOTS, chunk_keys, kv_rank), F32),
                            pltpu.VMEM((SAMPLE_SLOTS, P * rope_dim, page_size), F32),
                            pltpu.VMEM((n_heads, chunk_keys), F32),
                            pltpu.SemaphoreType.DMA((SAMPLE_SLOTS,)), pltpu.SemaphoreType.DMA((SAMPLE_SLOTS,))],
        ),
        out_shape=jax.ShapeDtypeStruct((DB, n_heads, kv_rank), F32),
        compiler_params=_params(("arbitrary",)),
        name="sample_attn",
    )(page_table, qlat_h.transpose(1, 0, 2), qrope_s, q_bd, q_bd, ckv_s.reshape(DB, 1, kv_rank),
      kr_s.reshape(DB, 1, rope_dim), cache_kv_latent, cache_krt)

    prev = state_ffn_conv[0].transpose(1, 0, 2)
    y_s, up_s = pl.pallas_call(
        functools.partial(_sample_back_kernel, n_heads=n_heads, d_ff=d_ff),
        grid=(1,),
        in_specs=[full(DB, D), full(DB, d_a), full(n_heads, DB, kv_rank), full(n_heads, kv_rank, d_b),
                  full(conv_w - 1, DB, 2 * d_ff), full(1, d_b), full(d_a + d_b, D), full(1, D),
                  full(D, 2 * d_ff), full(conv_w, 2 * d_ff), full(1, 2 * d_ff), full(d_ff, D), full(1, D)],
        out_specs=[whole(DB, D), whole(DB, 2 * d_ff)],
        out_shape=[jax.ShapeDtypeStruct((DB, D), F32), jax.ShapeDtypeStruct((DB, 2 * d_ff), F32)],
        compiler_params=_params(("arbitrary",)),
        name="sample_back",
    )(xs, out_a_s, o_lat.transpose(1, 0, 2), w_uvp, prev, gob, w_out_bf, gf, w_up_bf, wc, bc, w_down_bf, gfin)

    conv_s = jnp.concatenate([state_ffn_conv[0][:, 1:, :], up_s[:, None, :]], axis=1)
    return (y_p.reshape(B, S, D), y_s.reshape(DB, T, D),
            ckv_p.reshape(1, B, S, kv_rank), kr_p.reshape(1, B, S, rope_dim),
            conv_p.reshape(1, B, conv_w - 1, 2 * d_ff),
            ckv_s.reshape(1, DB, T, kv_rank), kr_s.reshape(1, DB, T, rope_dim),
            v_s.reshape(1, DB, T, d_a), conv_s.reshape(1, DB, conv_w - 1, 2 * d_ff))
```

```python
import functools

import jax
import jax.numpy as jnp
import numpy as np
from jax import lax
from jax.experimental import pallas as pl
from jax.experimental.pallas import tpu as pltpu

F32 = jnp.float32
BF16 = jnp.bfloat16

EPS = 1e-6
ROPE_THETA = 10000.0
LANE = 128
SUBLANE = 8
HEAD_SLOT = LANE
PROMPT_TILE = 512
ATTN_Q_TILE = 512
ATTN_K_TILE = 256
ATTN_QK_LEAD = 2
FFN_CHUNK = 256
CONV_STAGE_BUFFERS = 4
FFN_UP_LEAD = 2
PAGES_PER_CHUNK = 16
SAMPLE_DMA_LEAD = 2
SAMPLE_SLOTS = SAMPLE_DMA_LEAD + 2
VMEM_LIMIT = 56 * 1024 * 1024
LOG2E = float(np.log2(np.e))


def _rms(x, g):
    r = lax.rsqrt(jnp.mean(x * x, axis=-1, keepdims=True) + EPS)
    return (x * r) * g


def _gelu(x):
    return 0.5 * x * (1.0 + lax.erf(x * np.float32(np.sqrt(0.5))))


def _silu(x):
    return x * (1.0 / (1.0 + jnp.exp(-x)))


def _dot(a, b):
    return jnp.dot(a, b, preferred_element_type=F32)


def _dot_nt(a, b):
    return lax.dot_general(a, b, (((1,), (1,)), ((), ())), preferred_element_type=F32)


def _const_spec(shape, single_buffer=False):
    nd = len(shape)
    mode = pl.Buffered(1) if single_buffer else None
    return pl.BlockSpec(shape, lambda *_: (0,) * nd, pipeline_mode=mode)


def _params(semantics):
    return pltpu.CompilerParams(dimension_semantics=semantics, vmem_limit_bytes=VMEM_LIMIT)


def _front_common(x, g_mix, w_in, g_sgu, g_q, g_kv, w_q12, tqc, tqs, tk, d_a, q_rank, kv_rank, n_heads):
    h = _rms(x, g_mix).astype(BF16)
    proj = _dot(h, w_in)
    o1, o2 = d_a, 2 * d_a
    o3 = o2 + q_rank
    o4 = o3 + kv_rank
    u = _gelu(proj[:, :o1])
    v = _rms(_gelu(proj[:, o1:o2]), g_sgu)
    c_q = _rms(proj[:, o2:o3], g_q)
    c_kv = _rms(proj[:, o3:o4], g_kv)
    y = proj[:, o4:o4 + LANE] * tk
    kr = y + pltpu.roll(y, LANE // 2, 1)
    q12 = _dot(c_q.astype(BF16), w_q12)
    nq = n_heads * HEAD_SLOT
    q_parts = []
    for hd in range(n_heads):
        a = q12[:, hd * HEAD_SLOT:(hd + 1) * HEAD_SLOT]
        b = q12[:, nq + hd * HEAD_SLOT:nq + (hd + 1) * HEAD_SLOT]
        q_parts.append(a * tqc + b * tqs)
    return u, v, jnp.concatenate(q_parts, axis=1), c_kv, kr


def _prompt_front_kernel(x_ref, g_mix_ref, w_in_ref, g_sgu_ref, g_q_ref, g_kv_ref, w_q12_ref,
                         w_ukp_ref, w_uvt_ref, w_sp_ref, b_sp_ref, g_out_a_ref,
                         tqc_ref, tqs_ref, tk_ref,
                         out_a_ref, q_ref, k_ref, vt_ref, ckv_ref, kr_ref,
                         *, d_a, q_rank, kv_rank, n_heads, n_heads_a, chunk, rope_dim):
    tm = x_ref.shape[0]
    u, v, q_full, c_kv, kr = _front_common(
        x_ref[...], g_mix_ref[...], w_in_ref[...], g_sgu_ref[...], g_q_ref[...], g_kv_ref[...],
        w_q12_ref[...], tqc_ref[...], tqs_ref[...], tk_ref[...], d_a, q_rank, kv_rank, n_heads)
    q_ref[...] = q_full.astype(BF16)
    ckv_ref[...] = c_kv
    kr_ref[...] = kr[:, :rope_dim]

    c_bf = c_kv.astype(BF16)
    lane = lax.broadcasted_iota(jnp.int32, (tm, LANE), 1)
    kr_slot = jnp.where(lane >= LANE // 2, kr, 0.0)
    k_nope = _dot(c_bf, w_ukp_ref[...])
    k_parts = [k_nope[:, hd * HEAD_SLOT:(hd + 1) * HEAD_SLOT] + kr_slot for hd in range(n_heads)]
    k_ref[...] = jnp.concatenate(k_parts, axis=1).astype(BF16)
    v_t = _dot_nt(w_uvt_ref[...], c_bf).astype(BF16)
    tk = vt_ref.shape[2]
    for c in range(tm // tk):
        vt_ref[c] = v_t[:, c * tk:(c + 1) * tk]

    hd_a = d_a // n_heads_a
    heads_per_slab = LANE // hd_a
    row = lax.broadcasted_iota(jnp.int32, (chunk, chunk), 0)
    col = lax.broadcasted_iota(jnp.int32, (chunk, chunk), 1)
    w_causal = [jnp.where(row >= col, w_sp_ref[hd], 0.0).astype(BF16) for hd in range(n_heads_a)]
    v_bf = v.astype(BF16)
    lane_c = lax.broadcasted_iota(jnp.int32, (chunk, LANE), 1)
    gate_rows = []
    for c in range(tm // chunk):
        slabs = []
        for j in range(d_a // LANE):
            x_slab = v_bf[c * chunk:(c + 1) * chunk, j * LANE:(j + 1) * LANE]
            mixed = None
            for t in range(heads_per_slab):
                s_t = _dot(w_causal[j * heads_per_slab + t], x_slab)
                in_head = (lane_c >= t * hd_a) & (lane_c < (t + 1) * hd_a)
                mixed = s_t if mixed is None else jnp.where(in_head, s_t, mixed)
            slabs.append(mixed)
        gate_rows.append(jnp.concatenate(slabs, axis=1) + b_sp_ref[...])
    s = jnp.concatenate(gate_rows, axis=0)
    out_a_ref[...] = _rms(u * s, g_out_a_ref[...]).astype(BF16)


def _prompt_attn_kernel(q_ref, k_ref, vt_ref, o_ref, *, scale, v_head_dim):
    tq = q_ref.shape[0]
    tk = vt_ref.shape[2]
    i = pl.program_id(1)
    n_heads = q_ref.shape[1] // HEAD_SLOT
    c2 = scale * LOG2E
    q_t = [q_ref[:, t * HEAD_SLOT:(t + 1) * HEAD_SLOT].astype(F32).T.astype(BF16)
           for t in range(n_heads)]
    qpos = i * tq + lax.broadcasted_iota(jnp.int32, (tk, tq), 1)
    kloc = lax.broadcasted_iota(jnp.int32, (tk, tq), 0)

    blocks_per_step = tq // tk

    def step(g, carry, masked):
        units = [(g * blocks_per_step + d, t) for d in range(blocks_per_step) for t in range(n_heads)]

        def scores(j, t):
            start = pl.multiple_of(j * tk, tk)
            ks = k_ref[pl.ds(start, tk), t * HEAD_SLOT:(t + 1) * HEAD_SLOT]
            s_t = _dot(ks, q_t[t])
            return jnp.where(start + kloc <= qpos, s_t, -jnp.inf) if masked else s_t

        s_ahead = [scores(*u) for u in units[:ATTN_QK_LEAD]]
        state = list(carry)
        for n, (j, t) in enumerate(units):
            m, l, acc = state[t]
            if n + ATTN_QK_LEAD < len(units):
                s_ahead.append(scores(*units[n + ATTN_QK_LEAD]))
            s_t, s_ahead[n] = s_ahead[n], None
            m_new = jnp.maximum(m, jnp.max(s_t, axis=0, keepdims=True))
            p_t = jnp.exp2((s_t - m_new) * c2)
            alpha = jnp.exp2((m - m_new) * c2)
            l_new = alpha * l + jnp.sum(p_t, axis=0, keepdims=True)
            v_t = vt_ref[j, t * v_head_dim:(t + 1) * v_head_dim, :]
            state[t] = (m_new, l_new, alpha * acc + _dot(v_t, p_t.astype(BF16)))
        return tuple(state)

    init = tuple((jnp.full((1, tq), -jnp.inf, F32), jnp.zeros((1, tq), F32), jnp.zeros((v_head_dim, tq), F32))
                 for _ in range(n_heads))
    carry = lax.fori_loop(0, i, functools.partial(step, masked=False), init)
    carry = step(i, carry, True)
    o_t = jnp.concatenate([acc / l for _, l, acc in carry], axis=0)
    o_ref[...] = o_t.T


def _back_common(x, out_a, o_raw, g_out_b, w_out_ref, g_ffn, w_up_ref, w_conv_ref, b_conv_ref,
                 w_down_ref, g_final, conv_inputs, d_ff):
    d_a = out_a.shape[1]
    ob = _rms(o_raw, g_out_b).astype(BF16)
    mix = _dot(out_a, w_out_ref[:d_a, :]) + _dot(ob, w_out_ref[d_a:, :])
    x1 = x + mix
    h2 = _rms(x1, g_ffn).astype(BF16)
    n_chunks = d_ff // FFN_CHUNK

    def up_proj(j):
        cols = [slice(base + j * FFN_CHUNK, base + (j + 1) * FFN_CHUNK) for base in (0, d_ff)]
        return [(_dot(h2, w_up_ref[:, c]), c) for c in cols]

    f = None
    pending = [up_proj(j) for j in range(min(FFN_UP_LEAD, n_chunks))]
    for j in range(n_chunks):
        if j + FFN_UP_LEAD < n_chunks:
            pending.append(up_proj(j + FFN_UP_LEAD))
        conv = []
        for half, (up, cols) in enumerate(pending[j]):
            r2, r1 = conv_inputs(up, cols, 2 * j + half)
            conv.append(b_conv_ref[:, cols] + r2 * w_conv_ref[0:1, cols] + r1 * w_conv_ref[1:2, cols]
                        + up * w_conv_ref[2:3, cols])
        act = (_silu(conv[0]) * conv[1]).astype(BF16)
        part = _dot(act, w_down_ref[j * FFN_CHUNK:(j + 1) * FFN_CHUNK, :])
        f = part if f is None else f + part
        pending[j] = None
    return _rms(x1 + f, g_final)


def _prompt_back_kernel(x_ref, out_a_ref, o_ref, g_out_b_ref, w_out_ref, g_ffn_ref, w_up_ref,
                        w_conv_ref, b_conv_ref, w_down_ref, g_final_ref,
                        y_ref, conv_ref, tail_ref, stage_ref, *, tiles_per_seq, d_ff, conv_w):
    tm = x_ref.shape[0]
    keep = conv_w - 1

    @pl.when(pl.program_id(0) % tiles_per_seq == 0)
    def _():
        tail_ref[...] = jnp.zeros_like(tail_ref)

    def conv_inputs(up, cols, k):
        stage = stage_ref.at[k % stage_ref.shape[0]]
        stage[0:SUBLANE, :] = tail_ref[:, cols]
        stage[SUBLANE:, :] = up
        tail_ref[:, cols] = up[tm - SUBLANE:, :]
        conv_ref[:, cols] = up[tm - keep:, :]
        return stage[pl.ds(SUBLANE - 2, tm), :], stage[pl.ds(SUBLANE - 1, tm), :]

    y_ref[...] = _back_common(
        x_ref[...], out_a_ref[...], o_ref[...], g_out_b_ref[...], w_out_ref, g_ffn_ref[...], w_up_ref,
        w_conv_ref, b_conv_ref, w_down_ref, g_final_ref[...], conv_inputs, d_ff)


def _sample_front_kernel(x_ref, g_mix_ref, w_in_ref, g_sgu_ref, g_q_ref, g_kv_ref, w_q12_ref,
                         w_ukt_ref, w_sp0_ref, b_sp0_ref, g_out_a_ref, tqc_ref, tqs_ref, tk_ref,
                         out_a_ref, v_ref, ckv_ref, kr_ref, qlat_ref, qrope_ref,
                         *, d_a, q_rank, kv_rank, n_heads, nope_dim, rope_dim):
    u, v, q_full, c_kv, kr = _front_common(
        x_ref[...], g_mix_ref[...], w_in_ref[...], g_sgu_ref[...], g_q_ref[...], g_kv_ref[...],
        w_q12_ref[...], tqc_ref[...], tqs_ref[...], tk_ref[...], d_a, q_rank, kv_rank, n_heads)
    v_ref[...] = v
    ckv_ref[...] = c_kv
    kr_ref[...] = kr[:, :rope_dim]
    s = v * w_sp0_ref[...] + b_sp0_ref[...]
    out_a_ref[...] = _rms(u * s, g_out_a_ref[...]).astype(BF16)
    q_bf = q_full.astype(BF16)
    for hd in range(n_heads):
        q_h = q_bf[:, hd * HEAD_SLOT:(hd + 1) * HEAD_SLOT]
        qlat_ref[hd] = _dot(q_h, w_ukt_ref[hd])
        qrope_ref[hd] = q_full[:, hd * HEAD_SLOT + nope_dim:hd * HEAD_SLOT + nope_dim + rope_dim]


def _sample_attn_kernel(pt_ref, qlat_ref, qrope_ref, qbd_ref, qbd_next_ref, cnew_ref, krnew_ref, c_hbm, krt_hbm,
                        o_ref, cbuf, krbuf, s_ref, sem_c, sem_k, *, scale, pages_per_chunk, page_size):
    b = pl.program_id(0)
    n_seq = pl.num_programs(0)
    chunks = pt_ref.shape[1] // pages_per_chunk
    n_slots = cbuf.shape[0]
    n_heads = qlat_ref.shape[1]
    rope_dim = krt_hbm.shape[2]
    chunk_keys = pages_per_chunk * page_size
    c2 = scale * LOG2E
    b_next = jnp.minimum(b + 1, n_seq - 1)

    def owner(ch):
        return (b, ch, qbd_ref) if ch < chunks else (b_next, ch - chunks, qbd_next_ref)

    def start_chunk(seq, ch, slot):
        for i in range(pages_per_chunk):
            page = pt_ref[seq, ch * pages_per_chunk + i]
            queue = i % 2
            pltpu.make_async_copy(c_hbm.at[0, page], cbuf.at[slot, i], sem_c.at[slot]).start(priority=queue)
            pltpu.make_async_copy(krt_hbm.at[0, page], krbuf.at[slot, i], sem_k.at[slot]).start(priority=queue)

    def wait_chunk(slot):
        first_pages = pl.ds(0, pages_per_chunk)
        pltpu.make_async_copy(c_hbm.at[0, first_pages], cbuf.at[slot], sem_c.at[slot]).wait()
        pltpu.make_async_copy(krt_hbm.at[0, first_pages], krbuf.at[slot], sem_k.at[slot]).wait()

    def latent(slot):
        return cbuf[slot].reshape(chunk_keys, cbuf.shape[3]).astype(BF16)

    def scores(seq, qbd, slot):
        s_lat = _dot_nt(qlat_ref[seq].astype(BF16), latent(slot))
        kr = krbuf[slot].reshape(pages_per_chunk * rope_dim, page_size).astype(BF16)
        blocks = _dot(qbd[...], kr)
        s_rope = jnp.concatenate(
            [blocks[i * n_heads:(i + 1) * n_heads, :] for i in range(pages_per_chunk)], axis=1)
        return s_lat + s_rope

    @pl.when(b == 0)
    def _():
        for ch in range(SAMPLE_DMA_LEAD):
            start_chunk(0, ch, ch)
        wait_chunk(0)
        s_ref[...] = scores(0, qbd_ref, 0)

    q_lat = qlat_ref[b]
    c_new = cnew_ref[...]
    m = (jnp.sum(q_lat * c_new, axis=-1, keepdims=True)
         + jnp.sum(qrope_ref[b] * krnew_ref[...], axis=-1, keepdims=True))
    l = jnp.ones_like(m)
    acc = jnp.broadcast_to(c_new, q_lat.shape)
    s = s_ref[...]
    for ch in range(chunks):
        ahead = ch + SAMPLE_DMA_LEAD
        seq_a, ch_a, _ = owner(ahead)
        start_chunk(seq_a, ch_a, ahead % n_slots)
        seq_q, _, qbd_q = owner(ch + 1)
        wait_chunk((ch + 1) % n_slots)
        s_next = scores(seq_q, qbd_q, (ch + 1) % n_slots)
        m_new = jnp.maximum(m, jnp.max(s, axis=-1, keepdims=True))
        p = jnp.exp2((s - m_new) * c2)
        alpha = jnp.exp2((m - m_new) * c2)
        l = alpha * l + jnp.sum(p, axis=-1, keepdims=True)
        acc = alpha * acc + _dot(p.astype(BF16), latent(ch % n_slots))
        m, s = m_new, s_next
    s_ref[...] = s
    o_ref[...] = acc / l

    @pl.when(b == n_seq - 1)
    def _():
        for ch in range(1, SAMPLE_DMA_LEAD):
            wait_chunk((chunks + ch) % n_slots)


def _sample_back_kernel(x_ref, out_a_ref, olat_ref, w_uvp_ref, prev_ref, g_out_b_ref, w_out_ref, g_ffn_ref,
                        w_up_ref, w_conv_ref, b_conv_ref, w_down_ref, g_final_ref,
                        y_ref, up_ref, *, n_heads, d_ff):
    o_raw = None
    for hd in range(n_heads):
        part = _dot(olat_ref[hd].astype(BF16), w_uvp_ref[hd])
        o_raw = part if o_raw is None else o_raw + part

    def conv_inputs(up, cols, k):
        del k
        up_ref[:, cols] = up
        return prev_ref[0, :, cols], prev_ref[1, :, cols]

    y_ref[...] = _back_common(
        x_ref[...], out_a_ref[...], o_raw, g_out_b_ref[...], w_out_ref, g_ffn_ref[...], w_up_ref,
        w_conv_ref, b_conv_ref, w_down_ref, g_final_ref[...], conv_inputs, d_ff)


def _rope_tables(pos, rope_dim, nope_dim):
    half = rope_dim // 2
    inv = ROPE_THETA ** (-jnp.arange(half, dtype=F32) / half)
    ang = pos.astype(F32)[:, None] * inv
    cos, sin = jnp.cos(ang), jnp.sin(ang)
    n = pos.shape[0]
    z = lambda w: jnp.zeros((n, w), F32)
    pad = LANE - nope_dim - rope_dim
    tqc = jnp.concatenate([jnp.ones((n, nope_dim), F32), cos, cos, z(pad)], axis=1)
    tqs = jnp.concatenate([z(nope_dim), -sin, sin, z(pad)], axis=1)
    tk = jnp.concatenate([-sin, sin, z(LANE // 2 - rope_dim), cos, cos, z(LANE // 2 - rope_dim)], axis=1)
    return tqc, tqs, tk


def _pack_weights(w_in, w_uq, w_uk, w_uv, d_a, q_rank, kv_rank, rope_dim, nope_dim):
    half = rope_dim // 2
    d_model = w_in.shape[0]
    n_heads = w_uq.shape[1]
    o4 = 2 * d_a + q_rank + kv_rank
    x1, x2 = w_in[:, o4:o4 + half], w_in[:, o4 + half:o4 + rope_dim]
    zc = lambda rows, w: jnp.zeros((rows, w), w_in.dtype)
    gap = LANE // 2 - rope_dim
    w_in_p = jnp.concatenate([w_in[:, :o4], x2, x1, zc(d_model, gap), x1, x2, zc(d_model, gap)], axis=1)
    pad = HEAD_SLOT - nope_dim - rope_dim
    main, swap = [], []
    for hd in range(n_heads):
        nope = w_uq[:, hd, :nope_dim]
        r1, r2 = w_uq[:, hd, nope_dim:nope_dim + half], w_uq[:, hd, nope_dim + half:]
        main.append(jnp.concatenate([nope, r1, r2, zc(q_rank, pad)], axis=1))
        swap.append(jnp.concatenate([zc(q_rank, nope_dim), r2, r1, zc(q_rank, pad)], axis=1))
    w_q12 = jnp.concatenate(main + swap, axis=1)
    w_ukp = jnp.concatenate(
        [jnp.concatenate([w_uk[:, hd, :], zc(kv_rank, HEAD_SLOT - nope_dim)], axis=1) for hd in range(n_heads)],
        axis=1)
    w_ukt = jnp.stack(
        [jnp.concatenate([w_uk[:, hd, :].T, jnp.zeros((HEAD_SLOT - nope_dim, kv_rank), w_uk.dtype)], axis=0)
         for hd in range(n_heads)])
    v_dim = w_uv.shape[2]
    w_uv_flat = w_uv.reshape(kv_rank, n_heads * v_dim)
    w_uvp = jnp.stack(
        [jnp.pad(w_uv[:, hd, :], ((0, 0), (hd * v_dim, (n_heads - 1 - hd) * v_dim))) for hd in range(n_heads)])
    cast = lambda a: a.astype(BF16)
    return cast(w_in_p), cast(w_q12), cast(w_ukp), cast(w_ukt), cast(w_uv_flat), cast(w_uvp)


def kernel(x_prompt, x_sample, cache_kv_latent, cache_k_rope, state_ffn_conv, page_table,
           g_mix, w_in, g_sgu, w_spatial, b_spatial, g_q, w_uq, g_kv, w_uk, w_uv,
           g_out_a, g_out_b, w_out, g_ffn, w_up, w_conv, b_conv, w_down, g_final):
    B, S, D = x_prompt.shape
    DB, T, _ = x_sample.shape
    depth = w_in.shape[0]
    assert depth == 1 and T == 1
    n_heads_a, chunk = w_spatial.shape[1], w_spatial.shape[2]
    d_a = g_sgu.shape[1]
    q_rank, kv_rank = g_q.shape[1], g_kv.shape[1]
    n_heads, nope_dim, v_dim = w_uk.shape[2], w_uk.shape[3], w_uv.shape[3]
    rope_dim = w_uq.shape[3] - nope_dim
    d_b = n_heads * v_dim
    d_ff = w_down.shape[1]
    conv_w = w_conv.shape[1]
    n_pool, page_size = cache_kv_latent.shape[1], cache_kv_latent.shape[2]
    n_pages = page_table.shape[1]
    past_len = n_pages * page_size
    scale = float((nope_dim + rope_dim) ** -0.5)
    assert d_ff % FFN_CHUNK == 0 and S % PROMPT_TILE == 0 and PROMPT_TILE % chunk == 0
    assert n_pages % (SAMPLE_SLOTS * PAGES_PER_CHUNK) == 0 and conv_w == 3
    assert S % ATTN_Q_TILE == 0 and ATTN_Q_TILE % ATTN_K_TILE == 0 and PROMPT_TILE % ATTN_K_TILE == 0

    row2 = lambda a: a.reshape(1, -1)
    w_in_p, w_q12, w_ukp, w_ukt, w_uv_flat, w_uvp = _pack_weights(
        w_in[0], w_uq[0], w_uk[0], w_uv[0], d_a, q_rank, kv_rank, rope_dim, nope_dim)
    w_out_bf, w_up_bf, w_down_bf = w_out[0].astype(BF16), w_up[0].astype(BF16), w_down[0].astype(BF16)
    gm, gs, gq, gkv = row2(g_mix[0]), row2(g_sgu[0]), row2(g_q[0]), row2(g_kv[0])
    goa, gob, gf, gfin = row2(g_out_a[0]), row2(g_out_b[0]), row2(g_ffn[0]), row2(g_final)
    wc, bc = w_conv[0], row2(b_conv[0])
    hd_a = d_a // n_heads_a

    n_rows = B * S
    tm = PROMPT_TILE
    tkb = ATTN_K_TILE
    tiles_per_seq = S // tm
    xp = x_prompt.reshape(n_rows, D)
    tqc, tqs, tk = _rope_tables(jnp.arange(S, dtype=jnp.int32), rope_dim, nope_dim)
    b_sp = jnp.repeat(b_spatial[0][:, :chunk].T, hd_a, axis=1)
    n_in = w_in_p.shape[1]
    nq = n_heads * HEAD_SLOT
    rows = lambda w: pl.BlockSpec((tm, w), lambda i: (i, 0))
    table = pl.BlockSpec((tm, LANE), lambda i: (i % tiles_per_seq, 0))
    front = pl.pallas_call(
        functools.partial(_prompt_front_kernel, d_a=d_a, q_rank=q_rank, kv_rank=kv_rank, n_heads=n_heads,
                          n_heads_a=n_heads_a, chunk=chunk, rope_dim=rope_dim),
        grid=(n_rows // tm,),
        in_specs=[rows(D), _const_spec((1, D)), _const_spec((D, n_in), True), _const_spec((1, d_a)),
                  _const_spec((1, q_rank)), _const_spec((1, kv_rank)), _const_spec((q_rank, 2 * nq), True),
                  _const_spec((kv_rank, nq), True), _const_spec((d_b, kv_rank), True),
                  _const_spec((n_heads_a, chunk, chunk), True), _const_spec((chunk, d_a)), _const_spec((1, d_a)),
                  table, table, table],
        out_specs=[rows(d_a), rows(nq), rows(nq), pl.BlockSpec((tm // tkb, d_b, tkb), lambda i: (i, 0, 0)),
                   rows(kv_rank), rows(rope_dim)],
        out_shape=[jax.ShapeDtypeStruct((n_rows, d_a), BF16), jax.ShapeDtypeStruct((n_rows, nq), BF16),
                   jax.ShapeDtypeStruct((n_rows, nq), BF16), jax.ShapeDtypeStruct((n_rows // tkb, d_b, tkb), BF16),
                   jax.ShapeDtypeStruct((n_rows, kv_rank), F32), jax.ShapeDtypeStruct((n_rows, rope_dim), F32)],
        compiler_params=_params(("arbitrary",)),
        name="prompt_front",
    )
    out_a_p, q_p, k_p, vt_p, ckv_p, kr_p = front(
        xp, gm, w_in_p, gs, gq, gkv, w_q12, w_ukp, w_uv_flat.T, w_spatial[0][:, :chunk, :chunk], b_sp, goa,
        tqc, tqs, tk)

    tq = ATTN_Q_TILE
    nqb = S // tq
    o_p = pl.pallas_call(
        functools.partial(_prompt_attn_kernel, scale=scale, v_head_dim=v_dim),
        grid=(B, nqb),
        in_specs=[pl.BlockSpec((tq, nq), lambda b, i: (b * nqb + i, 0)),
                  pl.BlockSpec((S, nq), lambda b, i: (b, 0)),
                  pl.BlockSpec((S // tkb, d_b, tkb), lambda b, i: (b, 0, 0))],
        out_specs=pl.BlockSpec((tq, d_b), lambda b, i: (b * nqb + i, 0)),
        out_shape=jax.ShapeDtypeStruct((n_rows, d_b), F32),
        compiler_params=_params(("arbitrary", "arbitrary")),
        name="prompt_attn",
    )(q_p, k_p, vt_p)

    back_weights = [_const_spec((1, d_b)), _const_spec((d_a + d_b, D), True), _const_spec((1, D)),
                    _const_spec((D, 2 * d_ff), True), _const_spec((conv_w, 2 * d_ff)), _const_spec((1, 2 * d_ff)),
                    _const_spec((d_ff, D), True), _const_spec((1, D))]
    y_p, conv_p = pl.pallas_call(
        functools.partial(_prompt_back_kernel, tiles_per_seq=tiles_per_seq, d_ff=d_ff, conv_w=conv_w),
        grid=(n_rows // tm,),
        in_specs=[rows(D), rows(d_a), rows(d_b)] + back_weights,
        out_specs=[rows(D), pl.BlockSpec((None, conv_w - 1, 2 * d_ff), lambda i: (i // tiles_per_seq, 0, 0))],
        out_shape=[jax.ShapeDtypeStruct((n_rows, D), F32),
                   jax.ShapeDtypeStruct((B, conv_w - 1, 2 * d_ff), F32)],
        scratch_shapes=[pltpu.VMEM((SUBLANE, 2 * d_ff), F32),
                        pltpu.VMEM((CONV_STAGE_BUFFERS, tm + SUBLANE, FFN_CHUNK), F32)],
        compiler_params=_params(("arbitrary",)),
        name="prompt_back",
    )(xp, out_a_p, o_p, gob, w_out_bf, gf, w_up_bf, wc, bc, w_down_bf, gfin)

    xs = x_sample.reshape(DB, D)
    sqc, sqs, sk = _rope_tables(past_len + jnp.arange(T, dtype=jnp.int32), rope_dim, nope_dim)
    w_sp0 = jnp.repeat(w_spatial[0][:, 0, 0], hd_a).reshape(1, d_a)
    b_sp0 = jnp.repeat(b_spatial[0][:, 0], hd_a).reshape(1, d_a)
    full = lambda *shape: _const_spec(shape, True)
    whole = lambda *shape: _const_spec(shape)
    out_a_s, v_s, ckv_s, kr_s, qlat_h, qrope_h = pl.pallas_call(
        functools.partial(_sample_front_kernel, d_a=d_a, q_rank=q_rank, kv_rank=kv_rank, n_heads=n_heads,
                          nope_dim=nope_dim, rope_dim=rope_dim),
        grid=(1,),
        in_specs=[full(DB, D), full(1, D), full(D, n_in), full(1, d_a), full(1, q_rank), full(1, kv_rank),
                  full(q_rank, 2 * nq), full(n_heads, HEAD_SLOT, kv_rank), full(1, d_a), full(1, d_a),
                  full(1, d_a), full(1, LANE), full(1, LANE), full(1, LANE)],
        out_specs=[whole(DB, d_a), whole(DB, d_a), whole(DB, kv_rank), whole(DB, rope_dim),
                   whole(n_heads, DB, kv_rank), whole(n_heads, DB, rope_dim)],
        out_shape=[jax.ShapeDtypeStruct((DB, d_a), BF16), jax.ShapeDtypeStruct((DB, d_a), F32),
                   jax.ShapeDtypeStruct((DB, kv_rank), F32), jax.ShapeDtypeStruct((DB, rope_dim), F32),
                   jax.ShapeDtypeStruct((n_heads, DB, kv_rank), F32),
                   jax.ShapeDtypeStruct((n_heads, DB, rope_dim), F32)],
        compiler_params=_params(("arbitrary",)),
        name="sample_front",
    )(xs, gm, w_in_p, gs, gq, gkv, w_q12, w_ukt, w_sp0, b_sp0, goa, sqc, sqs, sk)

    P = PAGES_PER_CHUNK
    chunk_keys = P * page_size
    any_space = pl.BlockSpec(memory_space=pl.ANY)
    per_seq = lambda *tail: pl.BlockSpec((None,) + tail, lambda b, pt: (b,) + (0,) * len(tail))
    cache_krt = jnp.swapaxes(cache_k_rope, 2, 3)
    qrope_s = qrope_h.transpose(1, 0, 2)
    q_bd = jnp.einsum("ij,bhd->bihjd", jnp.eye(P, dtype=F32), qrope_s).reshape(
        DB, P * n_heads, P * rope_dim).astype(BF16)
    bd_block = (None, P * n_heads, P * rope_dim)
    o_lat = pl.pallas_call(
        functools.partial(_sample_attn_kernel, scale=scale, pages_per_chunk=P, page_size=page_size),
        grid_spec=pltpu.PrefetchScalarGridSpec(
            num_scalar_prefetch=1,
            grid=(DB,),
            in_specs=[whole(DB, n_heads, kv_rank), whole(DB, n_heads, rope_dim),
                      pl.BlockSpec(bd_block, lambda b, pt: (b, 0, 0)),
                      pl.BlockSpec(bd_block, lambda b, pt: (jnp.minimum(b + 1, DB - 1), 0, 0)),
                      per_seq(1, kv_rank), per_seq(1, rope_dim), any_space, any_space],
            out_specs=per_seq(n_heads, kv_rank),
            scratch_shapes=[pltpu.VMEM((SAMPLE_SLOTS, P, page_size, kv_rank), F32),
                            pltpu.VMEM((SAMPLE_SLOTS, P, rope_dim, page_size), F32),
                            pltpu.VMEM((n_heads, chunk_keys), F32),
                            pltpu.SemaphoreType.DMA((SAMPLE_SLOTS,)), pltpu.SemaphoreType.DMA((SAMPLE_SLOTS,))],
        ),
        out_shape=jax.ShapeDtypeStruct((DB, n_heads, kv_rank), F32),
        compiler_params=_params(("arbitrary",)),
        name="sample_attn",
    )(page_table, qlat_h.transpose(1, 0, 2), qrope_s, q_bd, q_bd, ckv_s.reshape(DB, 1, kv_rank),
      kr_s.reshape(DB, 1, rope_dim), cache_kv_latent, cache_krt)

    prev = state_ffn_conv[0].transpose(1, 0, 2)
    y_s, up_s = pl.pallas_call(
        functools.partial(_sample_back_kernel, n_heads=n_heads, d_ff=d_ff),
        grid=(1,),
        in_specs=[full(DB, D), full(DB, d_a), full(n_heads, DB, kv_rank), full(n_heads, kv_rank, d_b),
                  full(conv_w - 1, DB, 2 * d_ff), full(1, d_b), full(d_a + d_b, D), full(1, D),
                  full(D, 2 * d_ff), full(conv_w, 2 * d_ff), full(1, 2 * d_ff), full(d_ff, D), full(1, D)],
        out_specs=[whole(DB, D), whole(DB, 2 * d_ff)],
        out_shape=[jax.ShapeDtypeStruct((DB, D), F32), jax.ShapeDtypeStruct((DB, 2 * d_ff), F32)],
        compiler_params=_params(("arbitrary",)),
        name="sample_back",
    )(xs, out_a_s, o_lat.transpose(1, 0, 2), w_uvp, prev, gob, w_out_bf, gf, w_up_bf, wc, bc, w_down_bf, gfin)

    conv_s = jnp.concatenate([state_ffn_conv[0][:, 1:, :], up_s[:, None, :]], axis=1)
    return (y_p.reshape(B, S, D), y_s.reshape(DB, T, D),
            ckv_p.reshape(1, B, S, kv_rank), kr_p.reshape(1, B, S, rope_dim),
            conv_p.reshape(1, B, conv_w - 1, 2 * d_ff),
            ckv_s.reshape(1, DB, T, kv_rank), kr_s.reshape(1, DB, T, rope_dim),
            v_s.reshape(1, DB, T, d_a), conv_s.reshape(1, DB, conv_w - 1, 2 * d_ff))
```

```python
import functools

import jax
import jax.numpy as jnp
import numpy as np
from jax import lax
from jax.experimental import pallas as pl
from jax.experimental.pallas import tpu as pltpu

F32 = jnp.float32
BF16 = jnp.bfloat16

EPS = 1e-6
ROPE_THETA = 10000.0
LANE = 128
SUBLANE = 8
HEAD_SLOT = LANE
PROMPT_TILE = 512
ATTN_Q_TILE = 512
ATTN_K_TILE = 256
ATTN_QK_LEAD = 2
FFN_CHUNK = 256
CONV_STAGE_BUFFERS = 4
FFN_UP_LEAD = 2
PAGES_PER_CHUNK = 16
SAMPLE_DMA_LEAD = 2
SAMPLE_SLOTS = SAMPLE_DMA_LEAD + 2
VMEM_LIMIT = 56 * 1024 * 1024
LOG2E = float(np.log2(np.e))


def _rms(x, g):
    r = lax.rsqrt(jnp.mean(x * x, axis=-1, keepdims=True) + EPS)
    return (x * r) * g


def _gelu(x):
    return 0.5 * x * (1.0 + lax.erf(x * np.float32(np.sqrt(0.5))))


def _silu(x):
    return x * (1.0 / (1.0 + jnp.exp(-x)))


def _dot(a, b):
    return jnp.dot(a, b, preferred_element_type=F32)


def _dot_nt(a, b):
    return lax.dot_general(a, b, (((1,), (1,)), ((), ())), preferred_element_type=F32)


def _const_spec(shape, single_buffer=False):
    nd = len(shape)
    mode = pl.Buffered(1) if single_buffer else None
    return pl.BlockSpec(shape, lambda *_: (0,) * nd, pipeline_mode=mode)


def _params(semantics):
    return pltpu.CompilerParams(dimension_semantics=semantics, vmem_limit_bytes=VMEM_LIMIT)


def _front_common(x, g_mix, w_in, g_sgu, g_q, g_kv, w_q12, tqc, tqs, tk, d_a, q_rank, kv_rank, n_heads):
    h = _rms(x, g_mix).astype(BF16)
    proj = _dot(h, w_in)
    o1, o2 = d_a, 2 * d_a
    o3 = o2 + q_rank
    o4 = o3 + kv_rank
    u = _gelu(proj[:, :o1])
    v = _rms(_gelu(proj[:, o1:o2]), g_sgu)
    c_q = _rms(proj[:, o2:o3], g_q)
    c_kv = _rms(proj[:, o3:o4], g_kv)
    y = proj[:, o4:o4 + LANE] * tk
    kr = y + pltpu.roll(y, LANE // 2, 1)
    q12 = _dot(c_q.astype(BF16), w_q12)
    nq = n_heads * HEAD_SLOT
    q_parts = []
    for hd in range(n_heads):
        a = q12[:, hd * HEAD_SLOT:(hd + 1) * HEAD_SLOT]
        b = q12[:, nq + hd * HEAD_SLOT:nq + (hd + 1) * HEAD_SLOT]
        q_parts.append(a * tqc + b * tqs)
    return u, v, jnp.concatenate(q_parts, axis=1), c_kv, kr


def _prompt_front_kernel(x_ref, g_mix_ref, w_in_ref, g_sgu_ref, g_q_ref, g_kv_ref, w_q12_ref,
                         w_ukp_ref, w_uvt_ref, w_sp_ref, b_sp_ref, g_out_a_ref,
                         tqc_ref, tqs_ref, tk_ref,
                         out_a_ref, q_ref, k_ref, vt_ref, ckv_ref, kr_ref,
                         *, d_a, q_rank, kv_rank, n_heads, n_heads_a, chunk, rope_dim):
    tm = x_ref.shape[0]
    u, v, q_full, c_kv, kr = _front_common(
        x_ref[...], g_mix_ref[...], w_in_ref[...], g_sgu_ref[...], g_q_ref[...], g_kv_ref[...],
        w_q12_ref[...], tqc_ref[...], tqs_ref[...], tk_ref[...], d_a, q_rank, kv_rank, n_heads)
    q_ref[...] = q_full.astype(BF16)
    ckv_ref[...] = c_kv
    kr_ref[...] = kr[:, :rope_dim]

    c_bf = c_kv.astype(BF16)
    lane = lax.broadcasted_iota(jnp.int32, (tm, LANE), 1)
    kr_slot = jnp.where(lane >= LANE // 2, kr, 0.0)
    k_nope = _dot(c_bf, w_ukp_ref[...])
    k_parts = [k_nope[:, hd * HEAD_SLOT:(hd + 1) * HEAD_SLOT] + kr_slot for hd in range(n_heads)]
    k_ref[...] = jnp.concatenate(k_parts, axis=1).astype(BF16)
    v_t = _dot_nt(w_uvt_ref[...], c_bf).astype(BF16)
    tk = vt_ref.shape[2]
    for c in range(tm // tk):
        vt_ref[c] = v_t[:, c * tk:(c + 1) * tk]

    hd_a = d_a // n_heads_a
    heads_per_slab = LANE // hd_a
    row = lax.broadcasted_iota(jnp.int32, (chunk, chunk), 0)
    col = lax.broadcasted_iota(jnp.int32, (chunk, chunk), 1)
    w_causal = [jnp.where(row >= col, w_sp_ref[hd], 0.0).astype(BF16) for hd in range(n_heads_a)]
    v_bf = v.astype(BF16)
    lane_c = lax.broadcasted_iota(jnp.int32, (chunk, LANE), 1)
    gate_rows = []
    for c in range(tm // chunk):
        slabs = []
        for j in range(d_a // LANE):
            x_slab = v_bf[c * chunk:(c + 1) * chunk, j * LANE:(j + 1) * LANE]
            mixed = None
            for t in range(heads_per_slab):
                s_t = _dot(w_causal[j * heads_per_slab + t], x_slab)
                in_head = (lane_c >= t * hd_a) & (lane_c < (t + 1) * hd_a)
                mixed = s_t if mixed is None else jnp.where(in_head, s_t, mixed)
            slabs.append(mixed)
        gate_rows.append(jnp.concatenate(slabs, axis=1) + b_sp_ref[...])
    s = jnp.concatenate(gate_rows, axis=0)
    out_a_ref[...] = _rms(u * s, g_out_a_ref[...]).astype(BF16)


def _prompt_attn_kernel(q_ref, k_ref, vt_ref, o_ref, *, scale, v_head_dim):
    tq = q_ref.shape[0]
    tk = vt_ref.shape[2]
    i = pl.program_id(1)
    n_heads = q_ref.shape[1] // HEAD_SLOT
    c2 = scale * LOG2E
    q_t = [q_ref[:, t * HEAD_SLOT:(t + 1) * HEAD_SLOT].astype(F32).T.astype(BF16)
           for t in range(n_heads)]
    qpos = i * tq + lax.broadcasted_iota(jnp.int32, (tk, tq), 1)
    kloc = lax.broadcasted_iota(jnp.int32, (tk, tq), 0)

    blocks_per_step = tq // tk

    def step(g, carry, masked):
        units = [(g * blocks_per_step + d, t) for d in range(blocks_per_step) for t in range(n_heads)]

        def scores(j, t):
            start = pl.multiple_of(j * tk, tk)
            ks = k_ref[pl.ds(start, tk), t * HEAD_SLOT:(t + 1) * HEAD_SLOT]
            s_t = _dot(ks, q_t[t])
            return jnp.where(start + kloc <= qpos, s_t, -jnp.inf) if masked else s_t

        s_ahead = [scores(*u) for u in units[:ATTN_QK_LEAD]]
        state = list(carry)
        for n, (j, t) in enumerate(units):
            m, l, acc = state[t]
            if n + ATTN_QK_LEAD < len(units):
                s_ahead.append(scores(*units[n + ATTN_QK_LEAD]))
            s_t, s_ahead[n] = s_ahead[n], None
            m_new = jnp.maximum(m, jnp.max(s_t, axis=0, keepdims=True))
            p_t = jnp.exp2((s_t - m_new) * c2)
            alpha = jnp.exp2((m - m_new) * c2)
            l_new = alpha * l + jnp.sum(p_t, axis=0, keepdims=True)
            v_t = vt_ref[j, t * v_head_dim:(t + 1) * v_head_dim, :]
            state[t] = (m_new, l_new, alpha * acc + _dot(v_t, p_t.astype(BF16)))
        return tuple(state)

    init = tuple((jnp.full((1, tq), -jnp.inf, F32), jnp.zeros((1, tq), F32), jnp.zeros((v_head_dim, tq), F32))
                 for _ in range(n_heads))
    carry = lax.fori_loop(0, i, functools.partial(step, masked=False), init)
    carry = step(i, carry, True)
    o_t = jnp.concatenate([acc / l for _, l, acc in carry], axis=0)
    o_ref[...] = o_t.T


def _back_common(x, out_a, o_raw, g_out_b, w_out_ref, g_ffn, w_up_ref, w_conv_ref, b_conv_ref,
                 w_down_ref, g_final, conv_inputs, d_ff, after_chunk=None):
    d_a = out_a.shape[1]
    ob = _rms(o_raw, g_out_b).astype(BF16)
    mix = _dot(out_a, w_out_ref[:d_a, :]) + _dot(ob, w_out_ref[d_a:, :])
    x1 = x + mix
    h2 = _rms(x1, g_ffn).astype(BF16)
    n_chunks = d_ff // FFN_CHUNK

    def up_proj(j):
        cols = [slice(base + j * FFN_CHUNK, base + (j + 1) * FFN_CHUNK) for base in (0, d_ff)]
        return [(_dot(h2, w_up_ref[:, c]), c) for c in cols]

    f = None
    pending = [up_proj(j) for j in range(min(FFN_UP_LEAD, n_chunks))]
    for j in range(n_chunks):
        if j + FFN_UP_LEAD < n_chunks:
            pending.append(up_proj(j + FFN_UP_LEAD))
        conv = []
        for half, (up, cols) in enumerate(pending[j]):
            r2, r1 = conv_inputs(up, cols, 2 * j + half)
            conv.append(b_conv_ref[:, cols] + r2 * w_conv_ref[0:1, cols] + r1 * w_conv_ref[1:2, cols]
                        + up * w_conv_ref[2:3, cols])
        act = (_silu(conv[0]) * conv[1]).astype(BF16)
        part = _dot(act, w_down_ref[j * FFN_CHUNK:(j + 1) * FFN_CHUNK, :])
        f = part if f is None else f + part
        pending[j] = None
        if after_chunk is not None:
            after_chunk(j, n_chunks)
    return _rms(x1 + f, g_final)


def _sample_front_kernel(x_ref, g_mix_ref, w_in_ref, g_sgu_ref, g_q_ref, g_kv_ref, w_q12_ref,
                         w_ukt_ref, w_sp0_ref, b_sp0_ref, g_out_a_ref, tqc_ref, tqs_ref, tk_ref,
                         out_a_ref, v_ref, ckv_ref, kr_ref, qlat_ref, qrope_ref,
                         *, d_a, q_rank, kv_rank, n_heads, nope_dim, rope_dim):
    u, v, q_full, c_kv, kr = _front_common(
        x_ref[...], g_mix_ref[...], w_in_ref[...], g_sgu_ref[...], g_q_ref[...], g_kv_ref[...],
        w_q12_ref[...], tqc_ref[...], tqs_ref[...], tk_ref[...], d_a, q_rank, kv_rank, n_heads)
    v_ref[...] = v
    ckv_ref[...] = c_kv
    kr_ref[...] = kr[:, :rope_dim]
    s = v * w_sp0_ref[...] + b_sp0_ref[...]
    out_a_ref[...] = _rms(u * s, g_out_a_ref[...]).astype(BF16)
    q_bf = q_full.astype(BF16)
    for hd in range(n_heads):
        q_h = q_bf[:, hd * HEAD_SLOT:(hd + 1) * HEAD_SLOT]
        qlat_ref[hd] = _dot(q_h, w_ukt_ref[hd])
        qrope_ref[hd] = q_full[:, hd * HEAD_SLOT + nope_dim:hd * HEAD_SLOT + nope_dim + rope_dim]


def _paged_decoder(step, pt_ref, qlat_ref, qrope_ref, qbd_ref, qbd_next_ref, cnew_ref, krnew_ref, c_hbm, krt_hbm,
                   o_ref, cbuf, krbuf, s_ref, sem_c, sem_k, *, scale, pages_per_chunk, page_size):
    seqs = o_ref.shape[0]
    n_seq = qlat_ref.shape[0]
    chunks = pt_ref.shape[1] // pages_per_chunk
    n_units = seqs * chunks
    n_slots = cbuf.shape[0]
    n_heads = qlat_ref.shape[1]
    rope_dim = krt_hbm.shape[2]
    chunk_keys = pages_per_chunk * page_size
    c2 = scale * LOG2E
    first = step * seqs
    state = {}

    def owner(u):
        k = u // chunks
        if u < n_units:
            return first + k, u % chunks, qbd_ref.at[k]
        return jnp.minimum(first + k, n_seq - 1), u % chunks, qbd_next_ref

    def start_chunk(seq, ch, slot):
        for i in range(pages_per_chunk):
            page = pt_ref[seq, ch * pages_per_chunk + i]
            queue = i % 2
            pltpu.make_async_copy(c_hbm.at[0, page], cbuf.at[slot, i], sem_c.at[slot]).start(priority=queue)
            pltpu.make_async_copy(krt_hbm.at[0, page], krbuf.at[slot, i], sem_k.at[slot]).start(priority=queue)

    def wait_chunk(slot):
        first_pages = pl.ds(0, pages_per_chunk)
        pltpu.make_async_copy(c_hbm.at[0, first_pages], cbuf.at[slot], sem_c.at[slot]).wait()
        pltpu.make_async_copy(krt_hbm.at[0, first_pages], krbuf.at[slot], sem_k.at[slot]).wait()

    def latent(slot):
        return cbuf[slot].reshape(chunk_keys, cbuf.shape[3]).astype(BF16)

    def scores(seq, qbd, slot):
        s_lat = _dot_nt(qlat_ref[seq].astype(BF16), latent(slot))
        kr = krbuf[slot].reshape(pages_per_chunk * rope_dim, page_size).astype(BF16)
        blocks = _dot(qbd[...], kr)
        s_rope = jnp.concatenate(
            [blocks[i * n_heads:(i + 1) * n_heads, :] for i in range(pages_per_chunk)], axis=1)
        return s_lat + s_rope

    def prologue():
        for u in range(SAMPLE_DMA_LEAD):
            seq, ch, _ = owner(u)
            start_chunk(seq, ch, u % n_slots)
        wait_chunk(0)
        seq, _, qbd = owner(0)
        s_ref[...] = scores(seq, qbd, 0)

    def begin_sequence(k):
        q_lat = qlat_ref[first + k]
        c_new = cnew_ref[k]
        state["m"] = (jnp.sum(q_lat * c_new, axis=-1, keepdims=True)
                      + jnp.sum(qrope_ref[first + k] * krnew_ref[k], axis=-1, keepdims=True))
        state["l"] = jnp.ones_like(state["m"])
        state["acc"] = jnp.broadcast_to(c_new, q_lat.shape)

    def run_unit(u):
        if u == 0:
            state["s"] = s_ref[...]
        if u % chunks == 0:
            begin_sequence(u // chunks)
        ahead = u + SAMPLE_DMA_LEAD
        seq_a, ch_a, _ = owner(ahead)
        start_chunk(seq_a, ch_a, ahead % n_slots)
        seq_q, _, qbd_q = owner(u + 1)
        wait_chunk((u + 1) % n_slots)
        s_next = scores(seq_q, qbd_q, (u + 1) % n_slots)
        m, l, acc, s = state["m"], state["l"], state["acc"], state["s"]
        m_new = jnp.maximum(m, jnp.max(s, axis=-1, keepdims=True))
        p = jnp.exp2((s - m_new) * c2)
        alpha = jnp.exp2((m - m_new) * c2)
        state["l"] = alpha * l + jnp.sum(p, axis=-1, keepdims=True)
        state["acc"] = alpha * acc + _dot(p.astype(BF16), latent(u % n_slots))
        state["m"], state["s"] = m_new, s_next
        if u % chunks == chunks - 1:
            o_ref[u // chunks] = state["acc"] / state["l"]
        if u == n_units - 1:
            s_ref[...] = s_next

    def epilogue():
        for d in range(1, SAMPLE_DMA_LEAD):
            wait_chunk((n_units + d) % n_slots)

    return prologue, run_unit, epilogue, n_units


def _prompt_back_decode_kernel(pt_ref, x_ref, out_a_ref, o_ref, g_out_b_ref, w_out_ref, g_ffn_ref, w_up_ref,
                               w_conv_ref, b_conv_ref, w_down_ref, g_final_ref,
                               qlat_ref, qrope_ref, qbd_ref, qbd_next_ref, cnew_ref, krnew_ref, c_hbm, krt_hbm,
                               y_ref, conv_ref, olat_ref,
                               tail_ref, stage_ref, cbuf, krbuf, s_ref, sem_c, sem_k,
                               *, tiles_per_seq, d_ff, conv_w, scale, pages_per_chunk, page_size):
    tm = x_ref.shape[0]
    keep = conv_w - 1
    step = pl.program_id(0)
    prologue, run_unit, epilogue, n_units = _paged_decoder(
        step, pt_ref, qlat_ref, qrope_ref, qbd_ref, qbd_next_ref, cnew_ref, krnew_ref, c_hbm, krt_hbm,
        olat_ref, cbuf, krbuf, s_ref, sem_c, sem_k, scale=scale, pages_per_chunk=pages_per_chunk,
        page_size=page_size)
    pl.when(step == 0)(prologue)

    @pl.when(step % tiles_per_seq == 0)
    def _():
        tail_ref[...] = jnp.zeros_like(tail_ref)

    def conv_inputs(up, cols, k):
        stage = stage_ref.at[k % stage_ref.shape[0]]
        stage[0:SUBLANE, :] = tail_ref[:, cols]
        stage[SUBLANE:, :] = up
        tail_ref[:, cols] = up[tm - SUBLANE:, :]
        conv_ref[:, cols] = up[tm - keep:, :]
        return stage[pl.ds(SUBLANE - 2, tm), :], stage[pl.ds(SUBLANE - 1, tm), :]

    def after_chunk(j, n_chunks):
        for u in range(j * n_units // n_chunks, (j + 1) * n_units // n_chunks):
            run_unit(u)

    y_ref[...] = _back_common(
        x_ref[...], out_a_ref[...], o_ref[...], g_out_b_ref[...], w_out_ref, g_ffn_ref[...], w_up_ref,
        w_conv_ref, b_conv_ref, w_down_ref, g_final_ref[...], conv_inputs, d_ff, after_chunk=after_chunk)
    pl.when(step == pl.num_programs(0) - 1)(epilogue)


def _sample_back_kernel(x_ref, out_a_ref, olat_ref, w_uvp_ref, prev_ref, g_out_b_ref, w_out_ref, g_ffn_ref,
                        w_up_ref, w_conv_ref, b_conv_ref, w_down_ref, g_final_ref,
                        y_ref, up_ref, *, n_heads, d_ff):
    o_raw = None
    for hd in range(n_heads):
        part = _dot(olat_ref[hd].astype(BF16), w_uvp_ref[hd])
        o_raw = part if o_raw is None else o_raw + part

    def conv_inputs(up, cols, k):
        del k
        up_ref[:, cols] = up
        return prev_ref[0, :, cols], prev_ref[1, :, cols]

    y_ref[...] = _back_common(
        x_ref[...], out_a_ref[...], o_raw, g_out_b_ref[...], w_out_ref, g_ffn_ref[...], w_up_ref,
        w_conv_ref, b_conv_ref, w_down_ref, g_final_ref[...], conv_inputs, d_ff)


def _rope_tables(pos, rope_dim, nope_dim):
    half = rope_dim // 2
    inv = ROPE_THETA ** (-jnp.arange(half, dtype=F32) / half)
    ang = pos.astype(F32)[:, None] * inv
    cos, sin = jnp.cos(ang), jnp.sin(ang)
    n = pos.shape[0]
    z = lambda w: jnp.zeros((n, w), F32)
    pad = LANE - nope_dim - rope_dim
    tqc = jnp.concatenate([jnp.ones((n, nope_dim), F32), cos, cos, z(pad)], axis=1)
    tqs = jnp.concatenate([z(nope_dim), -sin, sin, z(pad)], axis=1)
    tk = jnp.concatenate([-sin, sin, z(LANE // 2 - rope_dim), cos, cos, z(LANE // 2 - rope_dim)], axis=1)
    return tqc, tqs, tk


def _pack_weights(w_in, w_uq, w_uk, w_uv, d_a, q_rank, kv_rank, rope_dim, nope_dim):
    half = rope_dim // 2
    d_model = w_in.shape[0]
    n_heads = w_uq.shape[1]
    o4 = 2 * d_a + q_rank + kv_rank
    x1, x2 = w_in[:, o4:o4 + half], w_in[:, o4 + half:o4 + rope_dim]
    zc = lambda rows, w: jnp.zeros((rows, w), w_in.dtype)
    gap = LANE // 2 - rope_dim
    w_in_p = jnp.concatenate([w_in[:, :o4], x2, x1, zc(d_model, gap), x1, x2, zc(d_model, gap)], axis=1)
    pad = HEAD_SLOT - nope_dim - rope_dim
    main, swap = [], []
    for hd in range(n_heads):
        nope = w_uq[:, hd, :nope_dim]
        r1, r2 = w_uq[:, hd, nope_dim:nope_dim + half], w_uq[:, hd, nope_dim + half:]
        main.append(jnp.concatenate([nope, r1, r2, zc(q_rank, pad)], axis=1))
        swap.append(jnp.concatenate([zc(q_rank, nope_dim), r2, r1, zc(q_rank, pad)], axis=1))
    w_q12 = jnp.concatenate(main + swap, axis=1)
    w_ukp = jnp.concatenate(
        [jnp.concatenate([w_uk[:, hd, :], zc(kv_rank, HEAD_SLOT - nope_dim)], axis=1) for hd in range(n_heads)],
        axis=1)
    w_ukt = jnp.stack(
        [jnp.concatenate([w_uk[:, hd, :].T, jnp.zeros((HEAD_SLOT - nope_dim, kv_rank), w_uk.dtype)], axis=0)
         for hd in range(n_heads)])
    v_dim = w_uv.shape[2]
    w_uv_flat = w_uv.reshape(kv_rank, n_heads * v_dim)
    w_uvp = jnp.stack(
        [jnp.pad(w_uv[:, hd, :], ((0, 0), (hd * v_dim, (n_heads - 1 - hd) * v_dim))) for hd in range(n_heads)])
    cast = lambda a: a.astype(BF16)
    return cast(w_in_p), cast(w_q12), cast(w_ukp), cast(w_ukt), cast(w_uv_flat), cast(w_uvp)


def kernel(x_prompt, x_sample, cache_kv_latent, cache_k_rope, state_ffn_conv, page_table,
           g_mix, w_in, g_sgu, w_spatial, b_spatial, g_q, w_uq, g_kv, w_uk, w_uv,
           g_out_a, g_out_b, w_out, g_ffn, w_up, w_conv, b_conv, w_down, g_final):
    B, S, D = x_prompt.shape
    DB, T, _ = x_sample.shape
    depth = w_in.shape[0]
    assert depth == 1 and T == 1
    n_heads_a, chunk = w_spatial.shape[1], w_spatial.shape[2]
    d_a = g_sgu.shape[1]
    q_rank, kv_rank = g_q.shape[1], g_kv.shape[1]
    n_heads, nope_dim, v_dim = w_uk.shape[2], w_uk.shape[3], w_uv.shape[3]
    rope_dim = w_uq.shape[3] - nope_dim
    d_b = n_heads * v_dim
    d_ff = w_down.shape[1]
    conv_w = w_conv.shape[1]
    n_pool, page_size = cache_kv_latent.shape[1], cache_kv_latent.shape[2]
    n_pages = page_table.shape[1]
    past_len = n_pages * page_size
    scale = float((nope_dim + rope_dim) ** -0.5)
    assert d_ff % FFN_CHUNK == 0 and S % PROMPT_TILE == 0 and PROMPT_TILE % chunk == 0
    assert n_pages % (SAMPLE_SLOTS * PAGES_PER_CHUNK) == 0 and conv_w == 3
    assert S % ATTN_Q_TILE == 0 and ATTN_Q_TILE % ATTN_K_TILE == 0 and PROMPT_TILE % ATTN_K_TILE == 0

    row2 = lambda a: a.reshape(1, -1)
    w_in_p, w_q12, w_ukp, w_ukt, w_uv_flat, w_uvp = _pack_weights(
        w_in[0], w_uq[0], w_uk[0], w_uv[0], d_a, q_rank, kv_rank, rope_dim, nope_dim)
    w_out_bf, w_up_bf, w_down_bf = w_out[0].astype(BF16), w_up[0].astype(BF16), w_down[0].astype(BF16)
    gm, gs, gq, gkv = row2(g_mix[0]), row2(g_sgu[0]), row2(g_q[0]), row2(g_kv[0])
    goa, gob, gf, gfin = row2(g_out_a[0]), row2(g_out_b[0]), row2(g_ffn[0]), row2(g_final)
    wc, bc = w_conv[0], row2(b_conv[0])
    hd_a = d_a // n_heads_a

    n_rows = B * S
    tm = PROMPT_TILE
    tkb = ATTN_K_TILE
    tiles_per_seq = S // tm
    xp = x_prompt.reshape(n_rows, D)
    tqc, tqs, tk = _rope_tables(jnp.arange(S, dtype=jnp.int32), rope_dim, nope_dim)
    b_sp = jnp.repeat(b_spatial[0][:, :chunk].T, hd_a, axis=1)
    n_in = w_in_p.shape[1]
    nq = n_heads * HEAD_SLOT
    rows = lambda w: pl.BlockSpec((tm, w), lambda i: (i, 0))
    table = pl.BlockSpec((tm, LANE), lambda i: (i % tiles_per_seq, 0))
    front = pl.pallas_call(
        functools.partial(_prompt_front_kernel, d_a=d_a, q_rank=q_rank, kv_rank=kv_rank, n_heads=n_heads,
                          n_heads_a=n_heads_a, chunk=chunk, rope_dim=rope_dim),
        grid=(n_rows // tm,),
        in_specs=[rows(D), _const_spec((1, D)), _const_spec((D, n_in), True), _const_spec((1, d_a)),
                  _const_spec((1, q_rank)), _const_spec((1, kv_rank)), _const_spec((q_rank, 2 * nq), True),
                  _const_spec((kv_rank, nq), True), _const_spec((d_b, kv_rank), True),
                  _const_spec((n_heads_a, chunk, chunk), True), _const_spec((chunk, d_a)), _const_spec((1, d_a)),
                  table, table, table],
        out_specs=[rows(d_a), rows(nq), rows(nq), pl.BlockSpec((tm // tkb, d_b, tkb), lambda i: (i, 0, 0)),
                   rows(kv_rank), rows(rope_dim)],
        out_shape=[jax.ShapeDtypeStruct((n_rows, d_a), BF16), jax.ShapeDtypeStruct((n_rows, nq), BF16),
                   jax.ShapeDtypeStruct((n_rows, nq), BF16), jax.ShapeDtypeStruct((n_rows // tkb, d_b, tkb), BF16),
                   jax.ShapeDtypeStruct((n_rows, kv_rank), F32), jax.ShapeDtypeStruct((n_rows, rope_dim), F32)],
        compiler_params=_params(("arbitrary",)),
        name="prompt_front",
    )
    out_a_p, q_p, k_p, vt_p, ckv_p, kr_p = front(
        xp, gm, w_in_p, gs, gq, gkv, w_q12, w_ukp, w_uv_flat.T, w_spatial[0][:, :chunk, :chunk], b_sp, goa,
        tqc, tqs, tk)

    tq = ATTN_Q_TILE
    nqb = S // tq
    o_p = pl.pallas_call(
        functools.partial(_prompt_attn_kernel, scale=scale, v_head_dim=v_dim),
        grid=(B, nqb),
        in_specs=[pl.BlockSpec((tq, nq), lambda b, i: (b * nqb + i, 0)),
                  pl.BlockSpec((S, nq), lambda b, i: (b, 0)),
                  pl.BlockSpec((S // tkb, d_b, tkb), lambda b, i: (b, 0, 0))],
        out_specs=pl.BlockSpec((tq, d_b), lambda b, i: (b * nqb + i, 0)),
        out_shape=jax.ShapeDtypeStruct((n_rows, d_b), F32),
        compiler_params=_params(("arbitrary", "arbitrary")),
        name="prompt_attn",
    )(q_p, k_p, vt_p)

    xs = x_sample.reshape(DB, D)
    sqc, sqs, sk = _rope_tables(past_len + jnp.arange(T, dtype=jnp.int32), rope_dim, nope_dim)
    w_sp0 = jnp.repeat(w_spatial[0][:, 0, 0], hd_a).reshape(1, d_a)
    b_sp0 = jnp.repeat(b_spatial[0][:, 0], hd_a).reshape(1, d_a)
    full = lambda *shape: _const_spec(shape, True)
    whole = lambda *shape: _const_spec(shape)
    out_a_s, v_s, ckv_s, kr_s, qlat_h, qrope_h = pl.pallas_call(
        functools.partial(_sample_front_kernel, d_a=d_a, q_rank=q_rank, kv_rank=kv_rank, n_heads=n_heads,
                          nope_dim=nope_dim, rope_dim=rope_dim),
        grid=(1,),
        in_specs=[full(DB, D), full(1, D), full(D, n_in), full(1, d_a), full(1, q_rank), full(1, kv_rank),
                  full(q_rank, 2 * nq), full(n_heads, HEAD_SLOT, kv_rank), full(1, d_a), full(1, d_a),
                  full(1, d_a), full(1, LANE), full(1, LANE), full(1, LANE)],
        out_specs=[whole(DB, d_a), whole(DB, d_a), whole(DB, kv_rank), whole(DB, rope_dim),
                   whole(n_heads, DB, kv_rank), whole(n_heads, DB, rope_dim)],
        out_shape=[jax.ShapeDtypeStruct((DB, d_a), BF16), jax.ShapeDtypeStruct((DB, d_a), F32),
                   jax.ShapeDtypeStruct((DB, kv_rank), F32), jax.ShapeDtypeStruct((DB, rope_dim), F32),
                   jax.ShapeDtypeStruct((n_heads, DB, kv_rank), F32),
                   jax.ShapeDtypeStruct((n_heads, DB, rope_dim), F32)],
        compiler_params=_params(("arbitrary",)),
        name="sample_front",
    )(xs, gm, w_in_p, gs, gq, gkv, w_q12, w_ukt, w_sp0, b_sp0, goa, sqc, sqs, sk)

    P = PAGES_PER_CHUNK
    chunk_keys = P * page_size
    n_steps = n_rows // tm
    seqs = DB // n_steps
    assert DB % n_steps == 0 and (seqs * (n_pages // P)) % SAMPLE_SLOTS == 0 and SAMPLE_DMA_LEAD < n_pages // P
    cache_krt = jnp.swapaxes(cache_k_rope, 2, 3)
    qrope_s = qrope_h.transpose(1, 0, 2)
    q_bd = jnp.einsum("ij,bhd->bihjd", jnp.eye(P, dtype=F32), qrope_s).reshape(
        DB, P * n_heads, P * rope_dim).astype(BF16)
    any_space = pl.BlockSpec(memory_space=pl.ANY)
    tile = lambda w: pl.BlockSpec((tm, w), lambda i, pt: (i, 0))
    per_step = lambda *tail: pl.BlockSpec((seqs,) + tail, lambda i, pt: (i,) + (0,) * len(tail))
    y_p, conv_p, o_lat = pl.pallas_call(
        functools.partial(_prompt_back_decode_kernel, tiles_per_seq=tiles_per_seq, d_ff=d_ff, conv_w=conv_w,
                          scale=scale, pages_per_chunk=P, page_size=page_size),
        grid_spec=pltpu.PrefetchScalarGridSpec(
            num_scalar_prefetch=1,
            grid=(n_steps,),
            in_specs=[tile(D), tile(d_a), tile(d_b),
                      _const_spec((1, d_b)), _const_spec((d_a + d_b, D), True), _const_spec((1, D)),
                      _const_spec((D, 2 * d_ff), True), _const_spec((conv_w, 2 * d_ff)),
                      _const_spec((1, 2 * d_ff)), _const_spec((d_ff, D), True), _const_spec((1, D)),
                      whole(DB, n_heads, kv_rank), whole(DB, n_heads, rope_dim),
                      per_step(P * n_heads, P * rope_dim),
                      pl.BlockSpec((None, P * n_heads, P * rope_dim),
                                   lambda i, pt: (jnp.minimum((i + 1) * seqs, DB - 1), 0, 0)),
                      per_step(1, kv_rank), per_step(1, rope_dim), any_space, any_space],
            out_specs=[tile(D),
                       pl.BlockSpec((None, conv_w - 1, 2 * d_ff), lambda i, pt: (i // tiles_per_seq, 0, 0)),
                       per_step(n_heads, kv_rank)],
            scratch_shapes=[pltpu.VMEM((SUBLANE, 2 * d_ff), F32),
                            pltpu.VMEM((CONV_STAGE_BUFFERS, tm + SUBLANE, FFN_CHUNK), F32),
                            pltpu.VMEM((SAMPLE_SLOTS, P, page_size, kv_rank), F32),
                            pltpu.VMEM((SAMPLE_SLOTS, P, rope_dim, page_size), F32),
                            pltpu.VMEM((n_heads, chunk_keys), F32),
                            pltpu.SemaphoreType.DMA((SAMPLE_SLOTS,)), pltpu.SemaphoreType.DMA((SAMPLE_SLOTS,))],
        ),
        out_shape=[jax.ShapeDtypeStruct((n_rows, D), F32),
                   jax.ShapeDtypeStruct((B, conv_w - 1, 2 * d_ff), F32),
                   jax.ShapeDtypeStruct((DB, n_heads, kv_rank), F32)],
        compiler_params=_params(("arbitrary",)),
        name="prompt_back_sample_attn",
    )(page_table, xp, out_a_p, o_p, gob, w_out_bf, gf, w_up_bf, wc, bc, w_down_bf, gfin,
      qlat_h.transpose(1, 0, 2), qrope_s, q_bd, q_bd, ckv_s.reshape(DB, 1, kv_rank),
      kr_s.reshape(DB, 1, rope_dim), cache_kv_latent, cache_krt)

    prev = state_ffn_conv[0].transpose(1, 0, 2)
    y_s, up_s = pl.pallas_call(
        functools.partial(_sample_back_kernel, n_heads=n_heads, d_ff=d_ff),
        grid=(1,),
        in_specs=[full(DB, D), full(DB, d_a), full(n_heads, DB, kv_rank), full(n_heads, kv_rank, d_b),
                  full(conv_w - 1, DB, 2 * d_ff), full(1, d_b), full(d_a + d_b, D), full(1, D),
                  full(D, 2 * d_ff), full(conv_w, 2 * d_ff), full(1, 2 * d_ff), full(d_ff, D), full(1, D)],
        out_specs=[whole(DB, D), whole(DB, 2 * d_ff)],
        out_shape=[jax.ShapeDtypeStruct((DB, D), F32), jax.ShapeDtypeStruct((DB, 2 * d_ff), F32)],
        compiler_params=_params(("arbitrary",)),
        name="sample_back",
    )(xs, out_a_s, o_lat.transpose(1, 0, 2), w_uvp, prev, gob, w_out_bf, gf, w_up_bf, wc, bc, w_down_bf, gfin)

    conv_s = jnp.concatenate([state_ffn_conv[0][:, 1:, :], up_s[:, None, :]], axis=1)
    return (y_p.reshape(B, S, D), y_s.reshape(DB, T, D),
            ckv_p.reshape(1, B, S, kv_rank), kr_p.reshape(1, B, S, rope_dim),
            conv_p.reshape(1, B, conv_w - 1, 2 * d_ff),
            ckv_s.reshape(1, DB, T, kv_rank), kr_s.reshape(1, DB, T, rope_dim),
            v_s.reshape(1, DB, T, d_a), conv_s.reshape(1, DB, conv_w - 1, 2 * d_ff))
```

```python
import functools

import jax
import jax.numpy as jnp
import numpy as np
from jax import lax
from jax.experimental import pallas as pl
from jax.experimental.pallas import tpu as pltpu

F32 = jnp.float32
BF16 = jnp.bfloat16

EPS = 1e-6
ROPE_THETA = 10000.0
LANE = 128
SUBLANE = 8
BF16_SUBLANES = 16
HEAD_SLOT = LANE
PROMPT_TILE = 512
ATTN_Q_TILE = 512
ATTN_K_TILE = 256
ATTN_QK_LEAD = 2
FFN_CHUNK = 256
CONV_STAGE_BUFFERS = 4
FFN_UP_LEAD = 2
PAGES_PER_CHUNK = 16
SAMPLE_DMA_LEAD = 2
SAMPLE_SLOTS = SAMPLE_DMA_LEAD + 2
VMEM_LIMIT = 56 * 1024 * 1024
LOG2E = float(np.log2(np.e))


def _rms(x, g):
    r = lax.rsqrt(jnp.mean(x * x, axis=-1, keepdims=True) + EPS)
    return (x * r) * g


def _gelu(x):
    return 0.5 * x * (1.0 + lax.erf(x * np.float32(np.sqrt(0.5))))


def _silu(x):
    return x * (1.0 / (1.0 + jnp.exp(-x)))


def _dot(a, b):
    return jnp.dot(a, b, preferred_element_type=F32)


def _dot_nt(a, b):
    return lax.dot_general(a, b, (((1,), (1,)), ((), ())), preferred_element_type=F32)


def _const_spec(shape, single_buffer=False):
    nd = len(shape)
    mode = pl.Buffered(1) if single_buffer else None
    return pl.BlockSpec(shape, lambda *_: (0,) * nd, pipeline_mode=mode)


def _params(semantics):
    return pltpu.CompilerParams(dimension_semantics=semantics, vmem_limit_bytes=VMEM_LIMIT)


def _front_common(x, g_mix, w_in, g_sgu, g_q, g_kv, w_q12, tqc, tqs, tk, d_a, q_rank, kv_rank, n_heads):
    h = _rms(x, g_mix).astype(BF16)
    proj = _dot(h, w_in)
    o1, o2 = d_a, 2 * d_a
    o3 = o2 + q_rank
    o4 = o3 + kv_rank
    u = _gelu(proj[:, :o1])
    v = _rms(_gelu(proj[:, o1:o2]), g_sgu)
    c_q = _rms(proj[:, o2:o3], g_q)
    c_kv = _rms(proj[:, o3:o4], g_kv)
    y = proj[:, o4:o4 + LANE] * tk
    kr = y + pltpu.roll(y, LANE // 2, 1)
    q12 = _dot(c_q.astype(BF16), w_q12)
    nq = n_heads * HEAD_SLOT
    q_parts = []
    for hd in range(n_heads):
        a = q12[:, hd * HEAD_SLOT:(hd + 1) * HEAD_SLOT]
        b = q12[:, nq + hd * HEAD_SLOT:nq + (hd + 1) * HEAD_SLOT]
        q_parts.append(a * tqc + b * tqs)
    return u, v, jnp.concatenate(q_parts, axis=1), c_kv, kr


def _prompt_front_kernel(x_ref, g_mix_ref, w_in_ref, g_sgu_ref, g_q_ref, g_kv_ref, w_q12_ref,
                         w_ukp_ref, w_uvt_ref, w_sp_ref, b_sp_ref, g_out_a_ref,
                         tqc_ref, tqs_ref, tk_ref,
                         out_a_ref, q_ref, k_ref, vt_ref, ckv_ref, kr_ref,
                         *, d_a, q_rank, kv_rank, n_heads, n_heads_a, chunk, rope_dim):
    tm = x_ref.shape[0]
    u, v, q_full, c_kv, kr = _front_common(
        x_ref[...], g_mix_ref[...], w_in_ref[...], g_sgu_ref[...], g_q_ref[...], g_kv_ref[...],
        w_q12_ref[...], tqc_ref[...], tqs_ref[...], tk_ref[...], d_a, q_rank, kv_rank, n_heads)
    q_ref[...] = q_full.astype(BF16)
    ckv_ref[...] = c_kv
    kr_ref[...] = kr.T[:rope_dim, :]

    c_bf = c_kv.astype(BF16)
    lane = lax.broadcasted_iota(jnp.int32, (tm, LANE), 1)
    kr_slot = jnp.where(lane >= LANE // 2, kr, 0.0)
    k_nope = _dot(c_bf, w_ukp_ref[...])
    k_parts = [k_nope[:, hd * HEAD_SLOT:(hd + 1) * HEAD_SLOT] + kr_slot for hd in range(n_heads)]
    k_ref[...] = jnp.concatenate(k_parts, axis=1).astype(BF16)
    v_t = _dot_nt(w_uvt_ref[...], c_bf).astype(BF16)
    tk = vt_ref.shape[2]
    for c in range(tm // tk):
        vt_ref[c] = v_t[:, c * tk:(c + 1) * tk]

    hd_a = d_a // n_heads_a
    heads_per_slab = LANE // hd_a
    row = lax.broadcasted_iota(jnp.int32, (chunk, chunk), 0)
    col = lax.broadcasted_iota(jnp.int32, (chunk, chunk), 1)
    w_causal = [jnp.where(row >= col, w_sp_ref[hd], 0.0).astype(BF16) for hd in range(n_heads_a)]
    v_bf = v.astype(BF16)
    lane_c = lax.broadcasted_iota(jnp.int32, (chunk, LANE), 1)
    gate_rows = []
    for c in range(tm // chunk):
        slabs = []
        for j in range(d_a // LANE):
            x_slab = v_bf[c * chunk:(c + 1) * chunk, j * LANE:(j + 1) * LANE]
            mixed = None
            for t in range(heads_per_slab):
                s_t = _dot(w_causal[j * heads_per_slab + t], x_slab)
                in_head = (lane_c >= t * hd_a) & (lane_c < (t + 1) * hd_a)
                mixed = s_t if mixed is None else jnp.where(in_head, s_t, mixed)
            slabs.append(mixed)
        gate_rows.append(jnp.concatenate(slabs, axis=1) + b_sp_ref[...])
    s = jnp.concatenate(gate_rows, axis=0)
    out_a_ref[...] = _rms(u * s, g_out_a_ref[...]).astype(BF16)


def _prompt_attn_kernel(q_ref, k_ref, vt_ref, o_ref, *, scale, v_head_dim):
    tq = q_ref.shape[0]
    tk = vt_ref.shape[2]
    i = pl.program_id(1)
    n_heads = q_ref.shape[1] // HEAD_SLOT
    c2 = scale * LOG2E
    q_t = [q_ref[:, t * HEAD_SLOT:(t + 1) * HEAD_SLOT].astype(F32).T.astype(BF16)
           for t in range(n_heads)]
    qpos = i * tq + lax.broadcasted_iota(jnp.int32, (tk, tq), 1)
    kloc = lax.broadcasted_iota(jnp.int32, (tk, tq), 0)

    blocks_per_step = tq // tk

    def step(g, carry, masked):
        units = [(g * blocks_per_step + d, t) for d in range(blocks_per_step) for t in range(n_heads)]

        def scores(j, t):
            start = pl.multiple_of(j * tk, tk)
            ks = k_ref[pl.ds(start, tk), t * HEAD_SLOT:(t + 1) * HEAD_SLOT]
            s_t = _dot(ks, q_t[t])
            return jnp.where(start + kloc <= qpos, s_t, -jnp.inf) if masked else s_t

        s_ahead = [scores(*u) for u in units[:ATTN_QK_LEAD]]
        state = list(carry)
        for n, (j, t) in enumerate(units):
            m, acc = state[t]
            if n + ATTN_QK_LEAD < len(units):
                s_ahead.append(scores(*units[n + ATTN_QK_LEAD]))
            s_t, s_ahead[n] = s_ahead[n], None
            m_new = jnp.maximum(m, jnp.max(s_t, axis=0, keepdims=True))
            p_t = jnp.exp2((s_t - m_new) * c2).astype(BF16)
            alpha = jnp.exp2((m - m_new) * c2)
            v_t = jnp.concatenate([vt_ref[j, t * v_head_dim:(t + 1) * v_head_dim, :], ones_rows], axis=0)
            state[t] = (m_new, alpha * acc + _dot(v_t, p_t))
        return tuple(state)

    ones_rows = jnp.ones((BF16_SUBLANES, tk), BF16)
    init = tuple((jnp.full((1, tq), -jnp.inf, F32), jnp.zeros((v_head_dim + BF16_SUBLANES, tq), F32))
                 for _ in range(n_heads))
    carry = lax.fori_loop(0, i, functools.partial(step, masked=False), init)
    carry = step(i, carry, True)
    o_t = jnp.concatenate([acc[:v_head_dim] / acc[v_head_dim:v_head_dim + 1] for _, acc in carry], axis=0)
    o_ref[...] = o_t.T


def _back_common(x, out_a, o_raw, g_out_b, w_out_ref, g_ffn, w_up_ref, w_conv_ref, b_conv_ref,
                 w_down_ref, g_final, conv_inputs, d_ff, after_chunk=None):
    d_a = out_a.shape[1]
    ob = _rms(o_raw, g_out_b).astype(BF16)
    mix = _dot(out_a, w_out_ref[:d_a, :]) + _dot(ob, w_out_ref[d_a:, :])
    x1 = x + mix
    h2 = _rms(x1, g_ffn).astype(BF16)
    n_chunks = d_ff // FFN_CHUNK

    def up_proj(j):
        cols = [slice(base + j * FFN_CHUNK, base + (j + 1) * FFN_CHUNK) for base in (0, d_ff)]
        return [(_dot(h2, w_up_ref[:, c]), c) for c in cols]

    f = None
    pending = [up_proj(j) for j in range(min(FFN_UP_LEAD, n_chunks))]
    for j in range(n_chunks):
        if j + FFN_UP_LEAD < n_chunks:
            pending.append(up_proj(j + FFN_UP_LEAD))
        conv = []
        for half, (up, cols) in enumerate(pending[j]):
            r2, r1 = conv_inputs(up, cols, 2 * j + half)
            conv.append(b_conv_ref[:, cols] + r2 * w_conv_ref[0:1, cols] + r1 * w_conv_ref[1:2, cols]
                        + up * w_conv_ref[2:3, cols])
        act = (_silu(conv[0]) * conv[1]).astype(BF16)
        part = _dot(act, w_down_ref[j * FFN_CHUNK:(j + 1) * FFN_CHUNK, :])
        f = part if f is None else f + part
        pending[j] = None
        if after_chunk is not None:
            after_chunk(j, n_chunks)
    return _rms(x1 + f, g_final)


def _sample_front_kernel(x_ref, g_mix_ref, w_in_ref, g_sgu_ref, g_q_ref, g_kv_ref, w_q12_ref,
                         w_ukt_ref, w_sp0_ref, b_sp0_ref, g_out_a_ref, tqc_ref, tqs_ref, tk_ref,
                         out_a_ref, v_ref, ckv_ref, kr_ref, qlat_ref, qrope_ref,
                         *, d_a, q_rank, kv_rank, n_heads, nope_dim, rope_dim):
    u, v, q_full, c_kv, kr = _front_common(
        x_ref[...], g_mix_ref[...], w_in_ref[...], g_sgu_ref[...], g_q_ref[...], g_kv_ref[...],
        w_q12_ref[...], tqc_ref[...], tqs_ref[...], tk_ref[...], d_a, q_rank, kv_rank, n_heads)
    v_ref[...] = v
    ckv_ref[...] = c_kv
    kr_ref[...] = kr[:, :rope_dim]
    s = v * w_sp0_ref[...] + b_sp0_ref[...]
    out_a_ref[...] = _rms(u * s, g_out_a_ref[...]).astype(BF16)
    q_bf = q_full.astype(BF16)
    for hd in range(n_heads):
        q_h = q_bf[:, hd * HEAD_SLOT:(hd + 1) * HEAD_SLOT]
        qlat_ref[hd] = _dot(q_h, w_ukt_ref[hd])
        qrope_ref[hd] = q_full[:, hd * HEAD_SLOT + nope_dim:hd * HEAD_SLOT + nope_dim + rope_dim]


def _paged_decoder(step, pt_ref, qlat_ref, qrope_ref, qbd_ref, qbd_next_ref, cnew_ref, krnew_ref, c_hbm, krt_hbm,
                   o_ref, cbuf, krbuf, s_ref, sem_c, sem_k, *, scale, pages_per_chunk, page_size):
    seqs = o_ref.shape[0]
    n_seq = qlat_ref.shape[0]
    chunks = pt_ref.shape[1] // pages_per_chunk
    n_units = seqs * chunks
    n_slots = cbuf.shape[0]
    n_heads = qlat_ref.shape[1]
    rope_dim = krt_hbm.shape[2]
    chunk_keys = pages_per_chunk * page_size
    c2 = scale * LOG2E
    first = step * seqs
    state = {}

    def owner(u):
        k = u // chunks
        if u < n_units:
            return first + k, u % chunks, qbd_ref.at[k]
        return jnp.minimum(first + k, n_seq - 1), u % chunks, qbd_next_ref

    def start_chunk(seq, ch, slot):
        for i in range(pages_per_chunk):
            page = pt_ref[seq, ch * pages_per_chunk + i]
            queue = i % 2
            pltpu.make_async_copy(c_hbm.at[0, page], cbuf.at[slot, i], sem_c.at[slot]).start(priority=queue)
            pltpu.make_async_copy(krt_hbm.at[0, page], krbuf.at[slot, i], sem_k.at[slot]).start(priority=queue)

    def wait_chunk(slot):
        first_pages = pl.ds(0, pages_per_chunk)
        pltpu.make_async_copy(c_hbm.at[0, first_pages], cbuf.at[slot], sem_c.at[slot]).wait()
        pltpu.make_async_copy(krt_hbm.at[0, first_pages], krbuf.at[slot], sem_k.at[slot]).wait()

    def latent(slot):
        return cbuf[slot].reshape(chunk_keys, cbuf.shape[3]).astype(BF16)

    def scores(seq, qbd, slot):
        s_lat = _dot_nt(qlat_ref[seq].astype(BF16), latent(slot))
        kr = krbuf[slot].reshape(pages_per_chunk * rope_dim, page_size).astype(BF16)
        blocks = _dot(qbd[...], kr)
        s_rope = jnp.concatenate(
            [blocks[i * n_heads:(i + 1) * n_heads, :] for i in range(pages_per_chunk)], axis=1)
        return s_lat + s_rope

    def prologue():
        for u in range(SAMPLE_DMA_LEAD):
            seq, ch, _ = owner(u)
            start_chunk(seq, ch, u % n_slots)
        wait_chunk(0)
        seq, _, qbd = owner(0)
        s_ref[...] = scores(seq, qbd, 0)

    def begin_sequence(k):
        q_lat = qlat_ref[first + k]
        c_new = cnew_ref[k]
        state["m"] = (jnp.sum(q_lat * c_new, axis=-1, keepdims=True)
                      + jnp.sum(qrope_ref[first + k] * krnew_ref[k], axis=-1, keepdims=True))
        state["l"] = jnp.ones_like(state["m"])
        state["acc"] = jnp.broadcast_to(c_new, q_lat.shape)

    unit_scores = {}

    def score_next(u):
        if u == 0:
            unit_scores[0] = s_ref[...]
        ahead = u + SAMPLE_DMA_LEAD
        seq_a, ch_a, _ = owner(ahead)
        start_chunk(seq_a, ch_a, ahead % n_slots)
        seq_q, _, qbd_q = owner(u + 1)
        wait_chunk((u + 1) % n_slots)
        unit_scores[u + 1] = scores(seq_q, qbd_q, (u + 1) % n_slots)
        if u == n_units - 1:
            s_ref[...] = unit_scores.pop(n_units)

    def update(u):
        if u % chunks == 0:
            begin_sequence(u // chunks)
        m, l, acc, s = state["m"], state["l"], state["acc"], unit_scores.pop(u)
        m_new = jnp.maximum(m, jnp.max(s, axis=-1, keepdims=True))
        p = jnp.exp2((s - m_new) * c2)
        alpha = jnp.exp2((m - m_new) * c2)
        state["l"] = alpha * l + jnp.sum(p, axis=-1, keepdims=True)
        state["acc"] = alpha * acc + _dot(p.astype(BF16), latent(u % n_slots))
        state["m"] = m_new
        if u % chunks == chunks - 1:
            o_ref[u // chunks] = state["acc"] / state["l"]

    def epilogue():
        for d in range(1, SAMPLE_DMA_LEAD):
            wait_chunk((n_units + d) % n_slots)

    return prologue, score_next, update, epilogue, n_units


def _prompt_back_decode_kernel(pt_ref, x_ref, out_a_ref, o_ref, g_out_b_ref, w_out_ref, g_ffn_ref, w_up_ref,
                               w_conv_ref, b_conv_ref, w_down_ref, g_final_ref,
                               qlat_ref, qrope_ref, qbd_ref, qbd_next_ref, cnew_ref, krnew_ref, c_hbm, krt_hbm,
                               y_ref, conv_ref, olat_ref,
                               tail_ref, stage_ref, cbuf, krbuf, s_ref, sem_c, sem_k,
                               *, tiles_per_seq, d_ff, conv_w, scale, pages_per_chunk, page_size):
    tm = x_ref.shape[0]
    keep = conv_w - 1
    step = pl.program_id(0)
    prologue, score_next, update, epilogue, n_units = _paged_decoder(
        step, pt_ref, qlat_ref, qrope_ref, qbd_ref, qbd_next_ref, cnew_ref, krnew_ref, c_hbm, krt_hbm,
        olat_ref, cbuf, krbuf, s_ref, sem_c, sem_k, scale=scale, pages_per_chunk=pages_per_chunk,
        page_size=page_size)
    pl.when(step == 0)(prologue)

    @pl.when(step % tiles_per_seq == 0)
    def _():
        tail_ref[...] = jnp.zeros_like(tail_ref)

    def conv_inputs(up, cols, k):
        stage = stage_ref.at[k % stage_ref.shape[0]]
        stage[0:SUBLANE, :] = tail_ref[:, cols]
        stage[SUBLANE:, :] = up
        tail_ref[:, cols] = up[tm - SUBLANE:, :]
        conv_ref[:, cols] = up[tm - keep:, :]
        return stage[pl.ds(SUBLANE - 2, tm), :], stage[pl.ds(SUBLANE - 1, tm), :]

    def units_of(j, n_chunks):
        return range(j * n_units // n_chunks, (j + 1) * n_units // n_chunks)

    def after_chunk(j, n_chunks):
        for u in units_of(j, n_chunks):
            score_next(u)
            update(u)

    y_ref[...] = _back_common(
        x_ref[...], out_a_ref[...], o_ref[...], g_out_b_ref[...], w_out_ref, g_ffn_ref[...], w_up_ref,
        w_conv_ref, b_conv_ref, w_down_ref, g_final_ref[...], conv_inputs, d_ff, after_chunk=after_chunk)
    pl.when(step == pl.num_programs(0) - 1)(epilogue)


def _sample_back_kernel(x_ref, out_a_ref, olat_ref, w_uvp_ref, prev_ref, g_out_b_ref, w_out_ref, g_ffn_ref,
                        w_up_ref, w_conv_ref, b_conv_ref, w_down_ref, g_final_ref,
                        y_ref, up_ref, *, n_heads, d_ff):
    o_raw = None
    for hd in range(n_heads):
        part = _dot(olat_ref[hd].astype(BF16), w_uvp_ref[hd])
        o_raw = part if o_raw is None else o_raw + part

    def conv_inputs(up, cols, k):
        del k
        up_ref[:, cols] = up
        return prev_ref[0, :, cols], prev_ref[1, :, cols]

    y_ref[...] = _back_common(
        x_ref[...], out_a_ref[...], o_raw, g_out_b_ref[...], w_out_ref, g_ffn_ref[...], w_up_ref,
        w_conv_ref, b_conv_ref, w_down_ref, g_final_ref[...], conv_inputs, d_ff)


def _rope_tables(pos, rope_dim, nope_dim):
    half = rope_dim // 2
    inv = ROPE_THETA ** (-jnp.arange(half, dtype=F32) / half)
    ang = pos.astype(F32)[:, None] * inv
    cos, sin = jnp.cos(ang), jnp.sin(ang)
    n = pos.shape[0]
    z = lambda w: jnp.zeros((n, w), F32)
    pad = LANE - nope_dim - rope_dim
    tqc = jnp.concatenate([jnp.ones((n, nope_dim), F32), cos, cos, z(pad)], axis=1)
    tqs = jnp.concatenate([z(nope_dim), -sin, sin, z(pad)], axis=1)
    tk = jnp.concatenate([-sin, sin, z(LANE // 2 - rope_dim), cos, cos, z(LANE // 2 - rope_dim)], axis=1)
    return tqc, tqs, tk


def _pack_weights(w_in, w_uq, w_uk, w_uv, d_a, q_rank, kv_rank, rope_dim, nope_dim):
    half = rope_dim // 2
    d_model = w_in.shape[0]
    n_heads = w_uq.shape[1]
    o4 = 2 * d_a + q_rank + kv_rank
    x1, x2 = w_in[:, o4:o4 + half], w_in[:, o4 + half:o4 + rope_dim]
    zc = lambda rows, w: jnp.zeros((rows, w), w_in.dtype)
    gap = LANE // 2 - rope_dim
    w_in_p = jnp.concatenate([w_in[:, :o4], x2, x1, zc(d_model, gap), x1, x2, zc(d_model, gap)], axis=1)
    pad = HEAD_SLOT - nope_dim - rope_dim
    main, swap = [], []
    for hd in range(n_heads):
        nope = w_uq[:, hd, :nope_dim]
        r1, r2 = w_uq[:, hd, nope_dim:nope_dim + half], w_uq[:, hd, nope_dim + half:]
        main.append(jnp.concatenate([nope, r1, r2, zc(q_rank, pad)], axis=1))
        swap.append(jnp.concatenate([zc(q_rank, nope_dim), r2, r1, zc(q_rank, pad)], axis=1))
    w_q12 = jnp.concatenate(main + swap, axis=1)
    w_ukp = jnp.concatenate(
        [jnp.concatenate([w_uk[:, hd, :], zc(kv_rank, HEAD_SLOT - nope_dim)], axis=1) for hd in range(n_heads)],
        axis=1)
    w_ukt = jnp.stack(
        [jnp.concatenate([w_uk[:, hd, :].T, jnp.zeros((HEAD_SLOT - nope_dim, kv_rank), w_uk.dtype)], axis=0)
         for hd in range(n_heads)])
    v_dim = w_uv.shape[2]
    w_uv_flat = w_uv.reshape(kv_rank, n_heads * v_dim)
    w_uvp = jnp.stack(
        [jnp.pad(w_uv[:, hd, :], ((0, 0), (hd * v_dim, (n_heads - 1 - hd) * v_dim))) for hd in range(n_heads)])
    cast = lambda a: a.astype(BF16)
    return cast(w_in_p), cast(w_q12), cast(w_ukp), cast(w_ukt), cast(w_uv_flat), cast(w_uvp)


def kernel(x_prompt, x_sample, cache_kv_latent, cache_k_rope, state_ffn_conv, page_table,
           g_mix, w_in, g_sgu, w_spatial, b_spatial, g_q, w_uq, g_kv, w_uk, w_uv,
           g_out_a, g_out_b, w_out, g_ffn, w_up, w_conv, b_conv, w_down, g_final):
    B, S, D = x_prompt.shape
    DB, T, _ = x_sample.shape
    depth = w_in.shape[0]
    assert depth == 1 and T == 1
    n_heads_a, chunk = w_spatial.shape[1], w_spatial.shape[2]
    d_a = g_sgu.shape[1]
    q_rank, kv_rank = g_q.shape[1], g_kv.shape[1]
    n_heads, nope_dim, v_dim = w_uk.shape[2], w_uk.shape[3], w_uv.shape[3]
    rope_dim = w_uq.shape[3] - nope_dim
    d_b = n_heads * v_dim
    d_ff = w_down.shape[1]
    conv_w = w_conv.shape[1]
    n_pool, page_size = cache_kv_latent.shape[1], cache_kv_latent.shape[2]
    n_pages = page_table.shape[1]
    past_len = n_pages * page_size
    scale = float((nope_dim + rope_dim) ** -0.5)
    assert d_ff % FFN_CHUNK == 0 and S % PROMPT_TILE == 0 and PROMPT_TILE % chunk == 0
    assert n_pages % (SAMPLE_SLOTS * PAGES_PER_CHUNK) == 0 and conv_w == 3
    assert S % ATTN_Q_TILE == 0 and ATTN_Q_TILE % ATTN_K_TILE == 0 and PROMPT_TILE % ATTN_K_TILE == 0

    row2 = lambda a: a.reshape(1, -1)
    w_in_p, w_q12, w_ukp, w_ukt, w_uv_flat, w_uvp = _pack_weights(
        w_in[0], w_uq[0], w_uk[0], w_uv[0], d_a, q_rank, kv_rank, rope_dim, nope_dim)
    w_out_bf, w_up_bf, w_down_bf = w_out[0].astype(BF16), w_up[0].astype(BF16), w_down[0].astype(BF16)
    gm, gs, gq, gkv = row2(g_mix[0]), row2(g_sgu[0]), row2(g_q[0]), row2(g_kv[0])
    goa, gob, gf, gfin = row2(g_out_a[0]), row2(g_out_b[0]), row2(g_ffn[0]), row2(g_final)
    wc, bc = w_conv[0], row2(b_conv[0])
    hd_a = d_a // n_heads_a

    n_rows = B * S
    tm = PROMPT_TILE
    tkb = ATTN_K_TILE
    tiles_per_seq = S // tm
    xp = x_prompt.reshape(n_rows, D)
    tqc, tqs, tk = _rope_tables(jnp.arange(S, dtype=jnp.int32), rope_dim, nope_dim)
    b_sp = jnp.repeat(b_spatial[0][:, :chunk].T, hd_a, axis=1)
    n_in = w_in_p.shape[1]
    nq = n_heads * HEAD_SLOT
    rows = lambda w: pl.BlockSpec((tm, w), lambda i: (i, 0))
    table = pl.BlockSpec((tm, LANE), lambda i: (i % tiles_per_seq, 0))
    front = pl.pallas_call(
        functools.partial(_prompt_front_kernel, d_a=d_a, q_rank=q_rank, kv_rank=kv_rank, n_heads=n_heads,
                          n_heads_a=n_heads_a, chunk=chunk, rope_dim=rope_dim),
        grid=(n_rows // tm,),
        in_specs=[rows(D), _const_spec((1, D)), _const_spec((D, n_in), True), _const_spec((1, d_a)),
                  _const_spec((1, q_rank)), _const_spec((1, kv_rank)), _const_spec((q_rank, 2 * nq), True),
                  _const_spec((kv_rank, nq), True), _const_spec((d_b, kv_rank), True),
                  _const_spec((n_heads_a, chunk, chunk), True), _const_spec((chunk, d_a)), _const_spec((1, d_a)),
                  table, table, table],
        out_specs=[rows(d_a), rows(nq), rows(nq), pl.BlockSpec((tm // tkb, d_b, tkb), lambda i: (i, 0, 0)),
                   rows(kv_rank),
                   pl.BlockSpec((None, rope_dim, tm), lambda i: (i // tiles_per_seq, 0, i % tiles_per_seq))],
        out_shape=[jax.ShapeDtypeStruct((n_rows, d_a), BF16), jax.ShapeDtypeStruct((n_rows, nq), BF16),
                   jax.ShapeDtypeStruct((n_rows, nq), BF16), jax.ShapeDtypeStruct((n_rows // tkb, d_b, tkb), BF16),
                   jax.ShapeDtypeStruct((n_rows, kv_rank), F32), jax.ShapeDtypeStruct((B, rope_dim, S), F32)],
        compiler_params=_params(("arbitrary",)),
        name="prompt_front",
    )
    out_a_p, q_p, k_p, vt_p, ckv_p, kr_p = front(
        xp, gm, w_in_p, gs, gq, gkv, w_q12, w_ukp, w_uv_flat.T, w_spatial[0][:, :chunk, :chunk], b_sp, goa,
        tqc, tqs, tk)

    tq = ATTN_Q_TILE
    nqb = S // tq
    o_p = pl.pallas_call(
        functools.partial(_prompt_attn_kernel, scale=scale, v_head_dim=v_dim),
        grid=(B, nqb),
        in_specs=[pl.BlockSpec((tq, nq), lambda b, i: (b * nqb + i, 0)),
                  pl.BlockSpec((S, nq), lambda b, i: (b, 0)),
                  pl.BlockSpec((S // tkb, d_b, tkb), lambda b, i: (b, 0, 0))],
        out_specs=pl.BlockSpec((tq, d_b), lambda b, i: (b * nqb + i, 0)),
        out_shape=jax.ShapeDtypeStruct((n_rows, d_b), F32),
        compiler_params=_params(("arbitrary", "arbitrary")),
        name="prompt_attn",
    )(q_p, k_p, vt_p)

    xs = x_sample.reshape(DB, D)
    sqc, sqs, sk = _rope_tables(past_len + jnp.arange(T, dtype=jnp.int32), rope_dim, nope_dim)
    w_sp0 = jnp.repeat(w_spatial[0][:, 0, 0], hd_a).reshape(1, d_a)
    b_sp0 = jnp.repeat(b_spatial[0][:, 0], hd_a).reshape(1, d_a)
    full = lambda *shape: _const_spec(shape, True)
    whole = lambda *shape: _const_spec(shape)
    out_a_s, v_s, ckv_s, kr_s, qlat_h, qrope_h = pl.pallas_call(
        functools.partial(_sample_front_kernel, d_a=d_a, q_rank=q_rank, kv_rank=kv_rank, n_heads=n_heads,
                          nope_dim=nope_dim, rope_dim=rope_dim),
        grid=(1,),
        in_specs=[full(DB, D), full(1, D), full(D, n_in), full(1, d_a), full(1, q_rank), full(1, kv_rank),
                  full(q_rank, 2 * nq), full(n_heads, HEAD_SLOT, kv_rank), full(1, d_a), full(1, d_a),
                  full(1, d_a), full(1, LANE), full(1, LANE), full(1, LANE)],
        out_specs=[whole(DB, d_a), whole(DB, d_a), whole(DB, kv_rank), whole(DB, rope_dim),
                   whole(n_heads, DB, kv_rank), whole(n_heads, DB, rope_dim)],
        out_shape=[jax.ShapeDtypeStruct((DB, d_a), BF16), jax.ShapeDtypeStruct((DB, d_a), F32),
                   jax.ShapeDtypeStruct((DB, kv_rank), F32), jax.ShapeDtypeStruct((DB, rope_dim), F32),
                   jax.ShapeDtypeStruct((n_heads, DB, kv_rank), F32),
                   jax.ShapeDtypeStruct((n_heads, DB, rope_dim), F32)],
        compiler_params=_params(("arbitrary",)),
        name="sample_front",
    )(xs, gm, w_in_p, gs, gq, gkv, w_q12, w_ukt, w_sp0, b_sp0, goa, sqc, sqs, sk)

    P = PAGES_PER_CHUNK
    chunk_keys = P * page_size
    n_steps = n_rows // tm
    seqs = DB // n_steps
    assert DB % n_steps == 0 and (seqs * (n_pages // P)) % SAMPLE_SLOTS == 0 and SAMPLE_DMA_LEAD < n_pages // P
    cache_krt = jnp.swapaxes(cache_k_rope, 2, 3)
    qrope_s = qrope_h.transpose(1, 0, 2)
    page_block = jnp.kron(jnp.eye(P, dtype=F32), jnp.ones((n_heads, rope_dim), F32))
    q_bd = (jnp.tile(qrope_s, (1, P, P)) * page_block).astype(BF16)
    any_space = pl.BlockSpec(memory_space=pl.ANY)
    tile = lambda w: pl.BlockSpec((tm, w), lambda i, pt: (i, 0))
    per_step = lambda *tail: pl.BlockSpec((seqs,) + tail, lambda i, pt: (i,) + (0,) * len(tail))
    y_p, conv_p, o_lat = pl.pallas_call(
        functools.partial(_prompt_back_decode_kernel, tiles_per_seq=tiles_per_seq, d_ff=d_ff, conv_w=conv_w,
                          scale=scale, pages_per_chunk=P, page_size=page_size),
        grid_spec=pltpu.PrefetchScalarGridSpec(
            num_scalar_prefetch=1,
            grid=(n_steps,),
            in_specs=[tile(D), tile(d_a), tile(d_b),
                      _const_spec((1, d_b)), _const_spec((d_a + d_b, D), True), _const_spec((1, D)),
                      _const_spec((D, 2 * d_ff), True), _const_spec((conv_w, 2 * d_ff)),
                      _const_spec((1, 2 * d_ff)), _const_spec((d_ff, D), True), _const_spec((1, D)),
                      whole(DB, n_heads, kv_rank), whole(DB, n_heads, rope_dim),
                      per_step(P * n_heads, P * rope_dim),
                      pl.BlockSpec((None, P * n_heads, P * rope_dim),
                                   lambda i, pt: (jnp.minimum((i + 1) * seqs, DB - 1), 0, 0)),
                      per_step(1, kv_rank), per_step(1, rope_dim), any_space, any_space],
            out_specs=[tile(D),
                       pl.BlockSpec((None, conv_w - 1, 2 * d_ff), lambda i, pt: (i // tiles_per_seq, 0, 0)),
                       per_step(n_heads, kv_rank)],
            scratch_shapes=[pltpu.VMEM((SUBLANE, 2 * d_ff), F32),
                            pltpu.VMEM((CONV_STAGE_BUFFERS, tm + SUBLANE, FFN_CHUNK), F32),
                            pltpu.VMEM((SAMPLE_SLOTS, P, page_size, kv_rank), F32),
                            pltpu.VMEM((SAMPLE_SLOTS, P, rope_dim, page_size), F32),
                            pltpu.VMEM((n_heads, chunk_keys), F32),
                            pltpu.SemaphoreType.DMA((SAMPLE_SLOTS,)), pltpu.SemaphoreType.DMA((SAMPLE_SLOTS,))],
        ),
        out_shape=[jax.ShapeDtypeStruct((n_rows, D), F32),
                   jax.ShapeDtypeStruct((B, conv_w - 1, 2 * d_ff), F32),
                   jax.ShapeDtypeStruct((DB, n_heads, kv_rank), F32)],
        compiler_params=_params(("arbitrary",)),
        name="prompt_back_sample_attn",
    )(page_table, xp, out_a_p, o_p, gob, w_out_bf, gf, w_up_bf, wc, bc, w_down_bf, gfin,
      qlat_h.transpose(1, 0, 2), qrope_s, q_bd, q_bd, ckv_s.reshape(DB, 1, kv_rank),
      kr_s.reshape(DB, 1, rope_dim), cache_kv_latent, cache_krt)

    prev = state_ffn_conv[0].transpose(1, 0, 2)
    y_s, up_s = pl.pallas_call(
        functools.partial(_sample_back_kernel, n_heads=n_heads, d_ff=d_ff),
        grid=(1,),
        in_specs=[full(DB, D), full(DB, d_a), full(n_heads, DB, kv_rank), full(n_heads, kv_rank, d_b),
                  full(conv_w - 1, DB, 2 * d_ff), full(1, d_b), full(d_a + d_b, D), full(1, D),
                  full(D, 2 * d_ff), full(conv_w, 2 * d_ff), full(1, 2 * d_ff), full(d_ff, D), full(1, D)],
        out_specs=[whole(DB, D), whole(DB, 2 * d_ff)],
        out_shape=[jax.ShapeDtypeStruct((DB, D), F32), jax.ShapeDtypeStruct((DB, 2 * d_ff), F32)],
        compiler_params=_params(("arbitrary",)),
        name="sample_back",
    )(xs, out_a_s, o_lat.transpose(1, 0, 2), w_uvp, prev, gob, w_out_bf, gf, w_up_bf, wc, bc, w_down_bf, gfin)

    conv_s = jnp.concatenate([state_ffn_conv[0][:, 1:, :], up_s[:, None, :]], axis=1)
    return (y_p.reshape(B, S, D), y_s.reshape(DB, T, D),
            ckv_p.reshape(1, B, S, kv_rank), jnp.swapaxes(kr_p, 1, 2).reshape(1, B, S, rope_dim),
            conv_p.reshape(1, B, conv_w - 1, 2 * d_ff),
            ckv_s.reshape(1, DB, T, kv_rank), kr_s.reshape(1, DB, T, rope_dim),
            v_s.reshape(1, DB, T, d_a), conv_s.reshape(1, DB, conv_w - 1, 2 * d_ff))
```

```python
import functools

import jax
import jax.numpy as jnp
import numpy as np
from jax import lax
from jax.experimental import pallas as pl
from jax.experimental.pallas import tpu as pltpu

F32 = jnp.float32
BF16 = jnp.bfloat16

EPS = 1e-6
ROPE_THETA = 10000.0
LANE = 128
SUBLANE = 8
BF16_SUBLANES = 16
HEAD_SLOT = LANE
PROMPT_TILE = 512
ATTN_Q_TILE = 512
ATTN_K_TILE = 256
ATTN_QK_LEAD = 2
FFN_CHUNK = 256
CONV_STAGE_BUFFERS = 4
FFN_UP_LEAD = 2
PAGES_PER_CHUNK = 16
SAMPLE_DMA_LEAD = 2
SAMPLE_SLOTS = SAMPLE_DMA_LEAD + 2
VMEM_LIMIT = 56 * 1024 * 1024
LOG2E = float(np.log2(np.e))


def _rms(x, g):
    r = lax.rsqrt(jnp.mean(x * x, axis=-1, keepdims=True) + EPS)
    return (x * r) * g


def _gelu(x):
    return 0.5 * x * (1.0 + lax.erf(x * np.float32(np.sqrt(0.5))))


def _silu(x):
    return x * (1.0 / (1.0 + jnp.exp(-x)))


def _dot(a, b):
    return jnp.dot(a, b, preferred_element_type=F32)


def _dot_nt(a, b):
    return lax.dot_general(a, b, (((1,), (1,)), ((), ())), preferred_element_type=F32)


def _const_spec(shape, single_buffer=False):
    nd = len(shape)
    mode = pl.Buffered(1) if single_buffer else None
    return pl.BlockSpec(shape, lambda *_: (0,) * nd, pipeline_mode=mode)


def _params(semantics):
    return pltpu.CompilerParams(dimension_semantics=semantics, vmem_limit_bytes=VMEM_LIMIT)


def _front_common(x, g_mix, w_in, g_sgu, g_q, g_kv, w_q12, tqc, tqs, tk, d_a, q_rank, kv_rank, n_heads):
    h = _rms(x, g_mix).astype(BF16)
    proj = _dot(h, w_in)
    o1, o2 = d_a, 2 * d_a
    o3 = o2 + q_rank
    o4 = o3 + kv_rank
    u = _gelu(proj[:, :o1])
    v = _rms(_gelu(proj[:, o1:o2]), g_sgu)
    c_q = _rms(proj[:, o2:o3], g_q)
    c_kv = _rms(proj[:, o3:o4], g_kv)
    y = proj[:, o4:o4 + LANE] * tk
    kr = y + pltpu.roll(y, LANE // 2, 1)
    q12 = _dot(c_q.astype(BF16), w_q12)
    nq = n_heads * HEAD_SLOT
    q_parts = []
    for hd in range(n_heads):
        a = q12[:, hd * HEAD_SLOT:(hd + 1) * HEAD_SLOT]
        b = q12[:, nq + hd * HEAD_SLOT:nq + (hd + 1) * HEAD_SLOT]
        q_parts.append(a * tqc + b * tqs)
    return u, v, jnp.concatenate(q_parts, axis=1), c_kv, kr


def _prompt_front_kernel(x_ref, g_mix_ref, w_in_ref, g_sgu_ref, g_q_ref, g_kv_ref, w_q12_ref,
                         w_ukp_ref, w_uvt_ref, w_sp_ref, b_sp_ref, g_out_a_ref,
                         tqc_ref, tqs_ref, tk_ref,
                         out_a_ref, q_ref, k_ref, vt_ref, ckv_ref, kr_ref,
                         *, d_a, q_rank, kv_rank, n_heads, n_heads_a, chunk, rope_dim):
    tm = x_ref.shape[0]
    u, v, q_full, c_kv, kr = _front_common(
        x_ref[...], g_mix_ref[...], w_in_ref[...], g_sgu_ref[...], g_q_ref[...], g_kv_ref[...],
        w_q12_ref[...], tqc_ref[...], tqs_ref[...], tk_ref[...], d_a, q_rank, kv_rank, n_heads)
    q_ref[...] = q_full.astype(BF16)
    ckv_ref[...] = c_kv
    kr_ref[...] = kr.T[:rope_dim, :]

    c_bf = c_kv.astype(BF16)
    lane = lax.broadcasted_iota(jnp.int32, (tm, LANE), 1)
    kr_slot = jnp.where(lane >= LANE // 2, kr, 0.0)
    k_nope = _dot(c_bf, w_ukp_ref[...])
    k_parts = [k_nope[:, hd * HEAD_SLOT:(hd + 1) * HEAD_SLOT] + kr_slot for hd in range(n_heads)]
    k_ref[...] = jnp.concatenate(k_parts, axis=1).astype(BF16)
    v_t = _dot_nt(w_uvt_ref[...], c_bf).astype(BF16)
    tk = vt_ref.shape[2]
    for c in range(tm // tk):
        vt_ref[c] = v_t[:, c * tk:(c + 1) * tk]

    hd_a = d_a // n_heads_a
    heads_per_slab = LANE // hd_a
    row = lax.broadcasted_iota(jnp.int32, (chunk, chunk), 0)
    col = lax.broadcasted_iota(jnp.int32, (chunk, chunk), 1)
    w_causal = [jnp.where(row >= col, w_sp_ref[hd], 0.0).astype(BF16) for hd in range(n_heads_a)]
    v_bf = v.astype(BF16)
    lane_c = lax.broadcasted_iota(jnp.int32, (chunk, LANE), 1)
    gate_rows = []
    for c in range(tm // chunk):
        slabs = []
        for j in range(d_a // LANE):
            x_slab = v_bf[c * chunk:(c + 1) * chunk, j * LANE:(j + 1) * LANE]
            mixed = None
            for t in range(heads_per_slab):
                s_t = _dot(w_causal[j * heads_per_slab + t], x_slab)
                in_head = (lane_c >= t * hd_a) & (lane_c < (t + 1) * hd_a)
                mixed = s_t if mixed is None else jnp.where(in_head, s_t, mixed)
            slabs.append(mixed)
        gate_rows.append(jnp.concatenate(slabs, axis=1) + b_sp_ref[...])
    s = jnp.concatenate(gate_rows, axis=0)
    out_a_ref[...] = _rms(u * s, g_out_a_ref[...]).astype(BF16)


def _prompt_attn_kernel(q_ref, k_ref, vt_ref, o_ref, *, scale, v_head_dim):
    tq = q_ref.shape[0]
    tk = vt_ref.shape[2]
    i = pl.program_id(1)
    n_heads = q_ref.shape[1] // HEAD_SLOT
    c2 = scale * LOG2E
    q_t = [q_ref[:, t * HEAD_SLOT:(t + 1) * HEAD_SLOT].astype(F32).T.astype(BF16)
           for t in range(n_heads)]
    blocks_per_step = tq // tk

    def step(g, carry, masked):
        units = [(d, t) for d in range(blocks_per_step) for t in range(n_heads)]

        def scores(d, t):
            lo = d * tk if masked else 0
            start = pl.multiple_of((g * blocks_per_step + d) * tk, tk)
            ks = k_ref[pl.ds(start, tk), t * HEAD_SLOT:(t + 1) * HEAD_SLOT]
            s_t = _dot(ks, q_t[t][:, lo:])
            if not masked:
                return s_t
            kpos = start + lax.broadcasted_iota(jnp.int32, s_t.shape, 0)
            qpos = i * tq + lo + lax.broadcasted_iota(jnp.int32, s_t.shape, 1)
            return jnp.where(kpos <= qpos, s_t, -jnp.inf)

        s_ahead = [scores(*u) for u in units[:ATTN_QK_LEAD]]
        state = list(carry)
        for n, (d, t) in enumerate(units):
            lo = d * tk if masked else 0
            m_all, acc_all = state[t]
            m, acc = m_all[0:1, lo:], acc_all[:, lo:]
            if n + ATTN_QK_LEAD < len(units):
                s_ahead.append(scores(*units[n + ATTN_QK_LEAD]))
            s_t, s_ahead[n] = s_ahead[n], None
            m_new = jnp.maximum(m, jnp.max(s_t, axis=0, keepdims=True))
            p_t = jnp.exp2((s_t - m_new) * c2).astype(BF16)
            alpha = jnp.exp2((m - m_new) * c2)
            m_new = jnp.broadcast_to(m_new, (SUBLANE, m_new.shape[1]))
            v_t = jnp.concatenate(
                [vt_ref[g * blocks_per_step + d, t * v_head_dim:(t + 1) * v_head_dim, :], ones_rows], axis=0)
            acc_new = alpha * acc + _dot(v_t, p_t)
            if lo:
                m_new = jnp.concatenate([m_all[:, :lo], m_new], axis=1)
                acc_new = jnp.concatenate([acc_all[:, :lo], acc_new], axis=1)
            state[t] = (m_new, acc_new)
        return tuple(state)

    ones_rows = jnp.ones((BF16_SUBLANES, tk), BF16)
    init = tuple((jnp.full((SUBLANE, tq), -jnp.inf, F32), jnp.zeros((v_head_dim + BF16_SUBLANES, tq), F32))
                 for _ in range(n_heads))
    carry = lax.fori_loop(0, i, functools.partial(step, masked=False), init)
    carry = step(i, carry, True)
    o_t = jnp.concatenate([acc[:v_head_dim] / acc[v_head_dim:v_head_dim + 1] for _, acc in carry], axis=0)
    o_ref[...] = o_t.T


def _back_common(x, out_a, o_raw, g_out_b, w_out_ref, g_ffn, w_up_ref, w_conv_ref, b_conv_ref,
                 w_down_ref, g_final, conv_inputs, d_ff, after_chunk=None):
    d_a = out_a.shape[1]
    ob = _rms(o_raw, g_out_b).astype(BF16)
    mix = _dot(out_a, w_out_ref[:d_a, :]) + _dot(ob, w_out_ref[d_a:, :])
    x1 = x + mix
    h2 = _rms(x1, g_ffn).astype(BF16)
    n_chunks = d_ff // FFN_CHUNK

    def up_proj(j):
        cols = [slice(base + j * FFN_CHUNK, base + (j + 1) * FFN_CHUNK) for base in (0, d_ff)]
        return [(_dot(h2, w_up_ref[:, c]), c) for c in cols]

    f = None
    pending = [up_proj(j) for j in range(min(FFN_UP_LEAD, n_chunks))]
    for j in range(n_chunks):
        if j + FFN_UP_LEAD < n_chunks:
            pending.append(up_proj(j + FFN_UP_LEAD))
        conv = []
        for half, (up, cols) in enumerate(pending[j]):
            r2, r1 = conv_inputs(up, cols, 2 * j + half)
            conv.append(b_conv_ref[:, cols] + r2 * w_conv_ref[0:1, cols] + r1 * w_conv_ref[1:2, cols]
                        + up * w_conv_ref[2:3, cols])
        act = (_silu(conv[0]) * conv[1]).astype(BF16)
        part = _dot(act, w_down_ref[j * FFN_CHUNK:(j + 1) * FFN_CHUNK, :])
        f = part if f is None else f + part
        pending[j] = None
        if after_chunk is not None:
            after_chunk(j, n_chunks)
    return _rms(x1 + f, g_final)


def _sample_front_kernel(x_ref, g_mix_ref, w_in_ref, g_sgu_ref, g_q_ref, g_kv_ref, w_q12_ref,
                         w_ukt_ref, w_sp0_ref, b_sp0_ref, g_out_a_ref, tqc_ref, tqs_ref, tk_ref,
                         out_a_ref, v_ref, ckv_ref, kr_ref, qlat_ref, qrope_ref,
                         *, d_a, q_rank, kv_rank, n_heads, nope_dim, rope_dim):
    u, v, q_full, c_kv, kr = _front_common(
        x_ref[...], g_mix_ref[...], w_in_ref[...], g_sgu_ref[...], g_q_ref[...], g_kv_ref[...],
        w_q12_ref[...], tqc_ref[...], tqs_ref[...], tk_ref[...], d_a, q_rank, kv_rank, n_heads)
    v_ref[...] = v
    ckv_ref[...] = c_kv
    kr_ref[...] = kr[:, :rope_dim]
    s = v * w_sp0_ref[...] + b_sp0_ref[...]
    out_a_ref[...] = _rms(u * s, g_out_a_ref[...]).astype(BF16)
    q_bf = q_full.astype(BF16)
    for hd in range(n_heads):
        q_h = q_bf[:, hd * HEAD_SLOT:(hd + 1) * HEAD_SLOT]
        qlat_ref[hd] = _dot(q_h, w_ukt_ref[hd])
        qrope_ref[hd] = q_full[:, hd * HEAD_SLOT + nope_dim:hd * HEAD_SLOT + nope_dim + rope_dim]


def _paged_decoder(step, pt_ref, qlat_ref, qrope_ref, qbd_ref, qbd_next_ref, cnew_ref, krnew_ref, c_hbm, krt_hbm,
                   o_ref, cbuf, krbuf, s_ref, sem_c, sem_k, *, scale, pages_per_chunk, page_size):
    seqs = o_ref.shape[0]
    n_seq = qlat_ref.shape[0]
    chunks = pt_ref.shape[1] // pages_per_chunk
    n_units = seqs * chunks
    n_slots = cbuf.shape[0]
    n_heads = qlat_ref.shape[1]
    rope_dim = krt_hbm.shape[2]
    chunk_keys = pages_per_chunk * page_size
    c2 = scale * LOG2E
    first = step * seqs
    state = {}

    def owner(u):
        k = u // chunks
        if u < n_units:
            return first + k, u % chunks, qbd_ref.at[k]
        return jnp.minimum(first + k, n_seq - 1), u % chunks, qbd_next_ref

    def start_chunk(seq, ch, slot):
        for i in range(pages_per_chunk):
            page = pt_ref[seq, ch * pages_per_chunk + i]
            queue = i % 2
            pltpu.make_async_copy(c_hbm.at[0, page], cbuf.at[slot, i], sem_c.at[slot]).start(priority=queue)
            pltpu.make_async_copy(krt_hbm.at[0, page], krbuf.at[slot, i], sem_k.at[slot]).start(priority=queue)

    def wait_chunk(slot):
        first_pages = pl.ds(0, pages_per_chunk)
        pltpu.make_async_copy(c_hbm.at[0, first_pages], cbuf.at[slot], sem_c.at[slot]).wait()
        pltpu.make_async_copy(krt_hbm.at[0, first_pages], krbuf.at[slot], sem_k.at[slot]).wait()

    def latent(slot):
        return cbuf[slot].reshape(chunk_keys, cbuf.shape[3]).astype(BF16)

    def scores(seq, qbd, slot):
        s_lat = _dot_nt(qlat_ref[seq].astype(BF16), latent(slot))
        kr = krbuf[slot].reshape(pages_per_chunk * rope_dim, page_size).astype(BF16)
        blocks = _dot(qbd[...], kr)
        s_rope = jnp.concatenate(
            [blocks[i * n_heads:(i + 1) * n_heads, :] for i in range(pages_per_chunk)], axis=1)
        return s_lat + s_rope

    def prologue():
        for u in range(SAMPLE_DMA_LEAD):
            seq, ch, _ = owner(u)
            start_chunk(seq, ch, u % n_slots)
        wait_chunk(0)
        seq, _, qbd = owner(0)
        s_ref[...] = scores(seq, qbd, 0)

    def begin_sequence(k):
        q_lat = qlat_ref[first + k]
        c_new = cnew_ref[k]
        state["m"] = (jnp.sum(q_lat * c_new, axis=-1, keepdims=True)
                      + jnp.sum(qrope_ref[first + k] * krnew_ref[k], axis=-1, keepdims=True))
        state["l"] = jnp.ones_like(state["m"])
        state["acc"] = jnp.broadcast_to(c_new, q_lat.shape)

    unit_scores = {}

    def score_next(u):
        if u == 0:
            unit_scores[0] = s_ref[...]
        ahead = u + SAMPLE_DMA_LEAD
        seq_a, ch_a, _ = owner(ahead)
        start_chunk(seq_a, ch_a, ahead % n_slots)
        seq_q, _, qbd_q = owner(u + 1)
        wait_chunk((u + 1) % n_slots)
        unit_scores[u + 1] = scores(seq_q, qbd_q, (u + 1) % n_slots)
        if u == n_units - 1:
            s_ref[...] = unit_scores.pop(n_units)

    def update(u):
        if u % chunks == 0:
            begin_sequence(u // chunks)
        m, l, acc, s = state["m"], state["l"], state["acc"], unit_scores.pop(u)
        m_new = jnp.maximum(m, jnp.max(s, axis=-1, keepdims=True))
        p = jnp.exp2((s - m_new) * c2)
        alpha = jnp.exp2((m - m_new) * c2)
        state["l"] = alpha * l + jnp.sum(p, axis=-1, keepdims=True)
        state["acc"] = alpha * acc + _dot(p.astype(BF16), latent(u % n_slots))
        state["m"] = m_new
        if u % chunks == chunks - 1:
            o_ref[u // chunks] = state["acc"] / state["l"]

    def epilogue():
        for d in range(1, SAMPLE_DMA_LEAD):
            wait_chunk((n_units + d) % n_slots)

    return prologue, score_next, update, epilogue, n_units


def _prompt_back_decode_kernel(pt_ref, x_ref, out_a_ref, o_ref, g_out_b_ref, w_out_ref, g_ffn_ref, w_up_ref,
                               w_conv_ref, b_conv_ref, w_down_ref, g_final_ref,
                               qlat_ref, qrope_ref, qbd_ref, qbd_next_ref, cnew_ref, krnew_ref, c_hbm, krt_hbm,
                               y_ref, conv_ref, olat_ref,
                               tail_ref, stage_ref, cbuf, krbuf, s_ref, sem_c, sem_k,
                               *, tiles_per_seq, d_ff, conv_w, scale, pages_per_chunk, page_size):
    tm = x_ref.shape[0]
    keep = conv_w - 1
    step = pl.program_id(0)
    prologue, score_next, update, epilogue, n_units = _paged_decoder(
        step, pt_ref, qlat_ref, qrope_ref, qbd_ref, qbd_next_ref, cnew_ref, krnew_ref, c_hbm, krt_hbm,
        olat_ref, cbuf, krbuf, s_ref, sem_c, sem_k, scale=scale, pages_per_chunk=pages_per_chunk,
        page_size=page_size)
    pl.when(step == 0)(prologue)

    @pl.when(step % tiles_per_seq == 0)
    def _():
        tail_ref[...] = jnp.zeros_like(tail_ref)

    def conv_inputs(up, cols, k):
        stage = stage_ref.at[k % stage_ref.shape[0]]
        stage[0:SUBLANE, :] = tail_ref[:, cols]
        stage[SUBLANE:, :] = up
        tail_ref[:, cols] = up[tm - SUBLANE:, :]
        conv_ref[:, cols] = up[tm - keep:, :]
        return stage[pl.ds(SUBLANE - 2, tm), :], stage[pl.ds(SUBLANE - 1, tm), :]

    def units_of(j, n_chunks):
        return range(j * n_units // n_chunks, (j + 1) * n_units // n_chunks)

    def after_chunk(j, n_chunks):
        for u in units_of(j, n_chunks):
            score_next(u)
            update(u)

    y_ref[...] = _back_common(
        x_ref[...], out_a_ref[...], o_ref[...], g_out_b_ref[...], w_out_ref, g_ffn_ref[...], w_up_ref,
        w_conv_ref, b_conv_ref, w_down_ref, g_final_ref[...], conv_inputs, d_ff, after_chunk=after_chunk)
    pl.when(step == pl.num_programs(0) - 1)(epilogue)


def _sample_back_kernel(x_ref, out_a_ref, olat_ref, w_uvp_ref, prev_ref, g_out_b_ref, w_out_ref, g_ffn_ref,
                        w_up_ref, w_conv_ref, b_conv_ref, w_down_ref, g_final_ref,
                        y_ref, up_ref, *, n_heads, d_ff):
    o_raw = None
    for hd in range(n_heads):
        part = _dot(olat_ref[hd].astype(BF16), w_uvp_ref[hd])
        o_raw = part if o_raw is None else o_raw + part

    def conv_inputs(up, cols, k):
        del k
        up_ref[:, cols] = up
        return prev_ref[0, :, cols], prev_ref[1, :, cols]

    y_ref[...] = _back_common(
        x_ref[...], out_a_ref[...], o_raw, g_out_b_ref[...], w_out_ref, g_ffn_ref[...], w_up_ref,
        w_conv_ref, b_conv_ref, w_down_ref, g_final_ref[...], conv_inputs, d_ff)


def _rope_tables(pos, rope_dim, nope_dim):
    half = rope_dim // 2
    inv = ROPE_THETA ** (-jnp.arange(half, dtype=F32) / half)
    ang = pos.astype(F32)[:, None] * inv
    cos, sin = jnp.cos(ang), jnp.sin(ang)
    n = pos.shape[0]
    z = lambda w: jnp.zeros((n, w), F32)
    pad = LANE - nope_dim - rope_dim
    tqc = jnp.concatenate([jnp.ones((n, nope_dim), F32), cos, cos, z(pad)], axis=1)
    tqs = jnp.concatenate([z(nope_dim), -sin, sin, z(pad)], axis=1)
    tk = jnp.concatenate([-sin, sin, z(LANE // 2 - rope_dim), cos, cos, z(LANE // 2 - rope_dim)], axis=1)
    return tqc, tqs, tk


def _pack_weights(w_in, w_uq, w_uk, w_uv, d_a, q_rank, kv_rank, rope_dim, nope_dim):
    half = rope_dim // 2
    d_model = w_in.shape[0]
    n_heads = w_uq.shape[1]
    o4 = 2 * d_a + q_rank + kv_rank
    x1, x2 = w_in[:, o4:o4 + half], w_in[:, o4 + half:o4 + rope_dim]
    zc = lambda rows, w: jnp.zeros((rows, w), w_in.dtype)
    gap = LANE // 2 - rope_dim
    w_in_p = jnp.concatenate([w_in[:, :o4], x2, x1, zc(d_model, gap), x1, x2, zc(d_model, gap)], axis=1)
    pad = HEAD_SLOT - nope_dim - rope_dim
    main, swap = [], []
    for hd in range(n_heads):
        nope = w_uq[:, hd, :nope_dim]
        r1, r2 = w_uq[:, hd, nope_dim:nope_dim + half], w_uq[:, hd, nope_dim + half:]
        main.append(jnp.concatenate([nope, r1, r2, zc(q_rank, pad)], axis=1))
        swap.append(jnp.concatenate([zc(q_rank, nope_dim), r2, r1, zc(q_rank, pad)], axis=1))
    w_q12 = jnp.concatenate(main + swap, axis=1)
    w_ukp = jnp.concatenate(
        [jnp.concatenate([w_uk[:, hd, :], zc(kv_rank, HEAD_SLOT - nope_dim)], axis=1) for hd in range(n_heads)],
        axis=1)
    w_ukt = jnp.stack(
        [jnp.concatenate([w_uk[:, hd, :].T, jnp.zeros((HEAD_SLOT - nope_dim, kv_rank), w_uk.dtype)], axis=0)
         for hd in range(n_heads)])
    v_dim = w_uv.shape[2]
    w_uv_flat = w_uv.reshape(kv_rank, n_heads * v_dim)
    w_uvp = jnp.stack(
        [jnp.pad(w_uv[:, hd, :], ((0, 0), (hd * v_dim, (n_heads - 1 - hd) * v_dim))) for hd in range(n_heads)])
    cast = lambda a: a.astype(BF16)
    return cast(w_in_p), cast(w_q12), cast(w_ukp), cast(w_ukt), cast(w_uv_flat), cast(w_uvp)


def kernel(x_prompt, x_sample, cache_kv_latent, cache_k_rope, state_ffn_conv, page_table,
           g_mix, w_in, g_sgu, w_spatial, b_spatial, g_q, w_uq, g_kv, w_uk, w_uv,
           g_out_a, g_out_b, w_out, g_ffn, w_up, w_conv, b_conv, w_down, g_final):
    B, S, D = x_prompt.shape
    DB, T, _ = x_sample.shape
    depth = w_in.shape[0]
    assert depth == 1 and T == 1
    n_heads_a, chunk = w_spatial.shape[1], w_spatial.shape[2]
    d_a = g_sgu.shape[1]
    q_rank, kv_rank = g_q.shape[1], g_kv.shape[1]
    n_heads, nope_dim, v_dim = w_uk.shape[2], w_uk.shape[3], w_uv.shape[3]
    rope_dim = w_uq.shape[3] - nope_dim
    d_b = n_heads * v_dim
    d_ff = w_down.shape[1]
    conv_w = w_conv.shape[1]
    n_pool, page_size = cache_kv_latent.shape[1], cache_kv_latent.shape[2]
    n_pages = page_table.shape[1]
    past_len = n_pages * page_size
    scale = float((nope_dim + rope_dim) ** -0.5)
    assert d_ff % FFN_CHUNK == 0 and S % PROMPT_TILE == 0 and PROMPT_TILE % chunk == 0
    assert n_pages % (SAMPLE_SLOTS * PAGES_PER_CHUNK) == 0 and conv_w == 3
    assert S % ATTN_Q_TILE == 0 and ATTN_Q_TILE % ATTN_K_TILE == 0 and PROMPT_TILE % ATTN_K_TILE == 0

    row2 = lambda a: a.reshape(1, -1)
    w_in_p, w_q12, w_ukp, w_ukt, w_uv_flat, w_uvp = _pack_weights(
        w_in[0], w_uq[0], w_uk[0], w_uv[0], d_a, q_rank, kv_rank, rope_dim, nope_dim)
    w_out_bf, w_up_bf, w_down_bf = w_out[0].astype(BF16), w_up[0].astype(BF16), w_down[0].astype(BF16)
    gm, gs, gq, gkv = row2(g_mix[0]), row2(g_sgu[0]), row2(g_q[0]), row2(g_kv[0])
    goa, gob, gf, gfin = row2(g_out_a[0]), row2(g_out_b[0]), row2(g_ffn[0]), row2(g_final)
    wc, bc = w_conv[0], row2(b_conv[0])
    hd_a = d_a // n_heads_a

    n_rows = B * S
    tm = PROMPT_TILE
    tkb = ATTN_K_TILE
    tiles_per_seq = S // tm
    xp = x_prompt.reshape(n_rows, D)
    tqc, tqs, tk = _rope_tables(jnp.arange(S, dtype=jnp.int32), rope_dim, nope_dim)
    b_sp = jnp.repeat(b_spatial[0][:, :chunk].T, hd_a, axis=1)
    n_in = w_in_p.shape[1]
    nq = n_heads * HEAD_SLOT
    rows = lambda w: pl.BlockSpec((tm, w), lambda i: (i, 0))
    table = pl.BlockSpec((tm, LANE), lambda i: (i % tiles_per_seq, 0))
    front = pl.pallas_call(
        functools.partial(_prompt_front_kernel, d_a=d_a, q_rank=q_rank, kv_rank=kv_rank, n_heads=n_heads,
                          n_heads_a=n_heads_a, chunk=chunk, rope_dim=rope_dim),
        grid=(n_rows // tm,),
        in_specs=[rows(D), _const_spec((1, D)), _const_spec((D, n_in), True), _const_spec((1, d_a)),
                  _const_spec((1, q_rank)), _const_spec((1, kv_rank)), _const_spec((q_rank, 2 * nq), True),
                  _const_spec((kv_rank, nq), True), _const_spec((d_b, kv_rank), True),
                  _const_spec((n_heads_a, chunk, chunk), True), _const_spec((chunk, d_a)), _const_spec((1, d_a)),
                  table, table, table],
        out_specs=[rows(d_a), rows(nq), rows(nq), pl.BlockSpec((tm // tkb, d_b, tkb), lambda i: (i, 0, 0)),
                   rows(kv_rank),
                   pl.BlockSpec((None, rope_dim, tm), lambda i: (i // tiles_per_seq, 0, i % tiles_per_seq))],
        out_shape=[jax.ShapeDtypeStruct((n_rows, d_a), BF16), jax.ShapeDtypeStruct((n_rows, nq), BF16),
                   jax.ShapeDtypeStruct((n_rows, nq), BF16), jax.ShapeDtypeStruct((n_rows // tkb, d_b, tkb), BF16),
                   jax.ShapeDtypeStruct((n_rows, kv_rank), F32), jax.ShapeDtypeStruct((B, rope_dim, S), F32)],
        compiler_params=_params(("arbitrary",)),
        name="prompt_front",
    )
    out_a_p, q_p, k_p, vt_p, ckv_p, kr_p = front(
        xp, gm, w_in_p, gs, gq, gkv, w_q12, w_ukp, w_uv_flat.T, w_spatial[0][:, :chunk, :chunk], b_sp, goa,
        tqc, tqs, tk)

    tq = ATTN_Q_TILE
    nqb = S // tq
    o_p = pl.pallas_call(
        functools.partial(_prompt_attn_kernel, scale=scale, v_head_dim=v_dim),
        grid=(B, nqb),
        in_specs=[pl.BlockSpec((tq, nq), lambda b, i: (b * nqb + i, 0)),
                  pl.BlockSpec((S, nq), lambda b, i: (b, 0)),
                  pl.BlockSpec((S // tkb, d_b, tkb), lambda b, i: (b, 0, 0))],
        out_specs=pl.BlockSpec((tq, d_b), lambda b, i: (b * nqb + i, 0)),
        out_shape=jax.ShapeDtypeStruct((n_rows, d_b), F32),
        compiler_params=_params(("arbitrary", "arbitrary")),
        name="prompt_attn",
    )(q_p, k_p, vt_p)

    xs = x_sample.reshape(DB, D)
    sqc, sqs, sk = _rope_tables(past_len + jnp.arange(T, dtype=jnp.int32), rope_dim, nope_dim)
    w_sp0 = jnp.repeat(w_spatial[0][:, 0, 0], hd_a).reshape(1, d_a)
    b_sp0 = jnp.repeat(b_spatial[0][:, 0], hd_a).reshape(1, d_a)
    full = lambda *shape: _const_spec(shape, True)
    whole = lambda *shape: _const_spec(shape)
    out_a_s, v_s, ckv_s, kr_s, qlat_h, qrope_h = pl.pallas_call(
        functools.partial(_sample_front_kernel, d_a=d_a, q_rank=q_rank, kv_rank=kv_rank, n_heads=n_heads,
                          nope_dim=nope_dim, rope_dim=rope_dim),
        grid=(1,),
        in_specs=[full(DB, D), full(1, D), full(D, n_in), full(1, d_a), full(1, q_rank), full(1, kv_rank),
                  full(q_rank, 2 * nq), full(n_heads, HEAD_SLOT, kv_rank), full(1, d_a), full(1, d_a),
                  full(1, d_a), full(1, LANE), full(1, LANE), full(1, LANE)],
        out_specs=[whole(DB, d_a), whole(DB, d_a), whole(DB, kv_rank), whole(DB, rope_dim),
                   whole(n_heads, DB, kv_rank), whole(n_heads, DB, rope_dim)],
        out_shape=[jax.ShapeDtypeStruct((DB, d_a), BF16), jax.ShapeDtypeStruct((DB, d_a), F32),
                   jax.ShapeDtypeStruct((DB, kv_rank), F32), jax.ShapeDtypeStruct((DB, rope_dim), F32),
                   jax.ShapeDtypeStruct((n_heads, DB, kv_rank), F32),
                   jax.ShapeDtypeStruct((n_heads, DB, rope_dim), F32)],
        compiler_params=_params(("arbitrary",)),
        name="sample_front",
    )(xs, gm, w_in_p, gs, gq, gkv, w_q12, w_ukt, w_sp0, b_sp0, goa, sqc, sqs, sk)

    P = PAGES_PER_CHUNK
    chunk_keys = P * page_size
    n_steps = n_rows // tm
    seqs = DB // n_steps
    assert DB % n_steps == 0 and (seqs * (n_pages // P)) % SAMPLE_SLOTS == 0 and SAMPLE_DMA_LEAD < n_pages // P
    cache_krt = jnp.swapaxes(cache_k_rope, 2, 3)
    qrope_s = qrope_h.transpose(1, 0, 2)
    page_block = np.kron(np.eye(P, dtype=np.float32), np.ones((n_heads, rope_dim), np.float32))
    q_bd = (jnp.tile(qrope_s, (1, P, P)) * page_block).astype(BF16)
    any_space = pl.BlockSpec(memory_space=pl.ANY)
    tile = lambda w: pl.BlockSpec((tm, w), lambda i, pt: (i, 0))
    per_step = lambda *tail: pl.BlockSpec((seqs,) + tail, lambda i, pt: (i,) + (0,) * len(tail))
    y_p, conv_p, o_lat = pl.pallas_call(
        functools.partial(_prompt_back_decode_kernel, tiles_per_seq=tiles_per_seq, d_ff=d_ff, conv_w=conv_w,
                          scale=scale, pages_per_chunk=P, page_size=page_size),
        grid_spec=pltpu.PrefetchScalarGridSpec(
            num_scalar_prefetch=1,
            grid=(n_steps,),
            in_specs=[tile(D), tile(d_a), tile(d_b),
                      _const_spec((1, d_b)), _const_spec((d_a + d_b, D), True), _const_spec((1, D)),
                      _const_spec((D, 2 * d_ff), True), _const_spec((conv_w, 2 * d_ff)),
                      _const_spec((1, 2 * d_ff)), _const_spec((d_ff, D), True), _const_spec((1, D)),
                      whole(DB, n_heads, kv_rank), whole(DB, n_heads, rope_dim),
                      per_step(P * n_heads, P * rope_dim),
                      pl.BlockSpec((None, P * n_heads, P * rope_dim),
                                   lambda i, pt: (jnp.minimum((i + 1) * seqs, DB - 1), 0, 0)),
                      per_step(1, kv_rank), per_step(1, rope_dim), any_space, any_space],
            out_specs=[tile(D),
                       pl.BlockSpec((None, conv_w - 1, 2 * d_ff), lambda i, pt: (i // tiles_per_seq, 0, 0)),
                       per_step(n_heads, kv_rank)],
            scratch_shapes=[pltpu.VMEM((SUBLANE, 2 * d_ff), F32),
                            pltpu.VMEM((CONV_STAGE_BUFFERS, tm + SUBLANE, FFN_CHUNK), F32),
                            pltpu.VMEM((SAMPLE_SLOTS, P, page_size, kv_rank), F32),
                            pltpu.VMEM((SAMPLE_SLOTS, P, rope_dim, page_size), F32),
                            pltpu.VMEM((n_heads, chunk_keys), F32),
                            pltpu.SemaphoreType.DMA((SAMPLE_SLOTS,)), pltpu.SemaphoreType.DMA((SAMPLE_SLOTS,))],
        ),
        out_shape=[jax.ShapeDtypeStruct((n_rows, D), F32),
                   jax.ShapeDtypeStruct((B, conv_w - 1, 2 * d_ff), F32),
                   jax.ShapeDtypeStruct((DB, n_heads, kv_rank), F32)],
        compiler_params=_params(("arbitrary",)),
        name="prompt_back_sample_attn",
    )(page_table, xp, out_a_p, o_p, gob, w_out_bf, gf, w_up_bf, wc, bc, w_down_bf, gfin,
      qlat_h.transpose(1, 0, 2), qrope_s, q_bd, q_bd, ckv_s.reshape(DB, 1, kv_rank),
      kr_s.reshape(DB, 1, rope_dim), cache_kv_latent, cache_krt)

    prev = state_ffn_conv[0].transpose(1, 0, 2)
    y_s, up_s = pl.pallas_call(
        functools.partial(_sample_back_kernel, n_heads=n_heads, d_ff=d_ff),
        grid=(1,),
        in_specs=[full(DB, D), full(DB, d_a), full(n_heads, DB, kv_rank), full(n_heads, kv_rank, d_b),
                  full(conv_w - 1, DB, 2 * d_ff), full(1, d_b), full(d_a + d_b, D), full(1, D),
                  full(D, 2 * d_ff), full(conv_w, 2 * d_ff), full(1, 2 * d_ff), full(d_ff, D), full(1, D)],
        out_specs=[whole(DB, D), whole(DB, 2 * d_ff)],
        out_shape=[jax.ShapeDtypeStruct((DB, D), F32), jax.ShapeDtypeStruct((DB, 2 * d_ff), F32)],
        compiler_params=_params(("arbitrary",)),
        name="sample_back",
    )(xs, out_a_s, o_lat.transpose(1, 0, 2), w_uvp, prev, gob, w_out_bf, gf, w_up_bf, wc, bc, w_down_bf, gfin)

    conv_s = jnp.concatenate([state_ffn_conv[0][:, 1:, :], up_s[:, None, :]], axis=1)
    return (y_p.reshape(B, S, D), y_s.reshape(DB, T, D),
            ckv_p.reshape(1, B, S, kv_rank), jnp.swapaxes(kr_p, 1, 2).reshape(1, B, S, rope_dim),
            conv_p.reshape(1, B, conv_w - 1, 2 * d_ff),
            ckv_s.reshape(1, DB, T, kv_rank), kr_s.reshape(1, DB, T, rope_dim),
            v_s.reshape(1, DB, T, d_a), conv_s.reshape(1, DB, conv_w - 1, 2 * d_ff))
```

```python
import functools

import jax
import jax.numpy as jnp
import numpy as np
from jax import lax
from jax.experimental import pallas as pl
from jax.experimental.pallas import tpu as pltpu

F32 = jnp.float32
BF16 = jnp.bfloat16

EPS = 1e-6
ROPE_THETA = 10000.0
LANE = 128
SUBLANE = 8
BF16_SUBLANES = 16
HEAD_SLOT = LANE
PROMPT_TILE = 512
ATTN_Q_TILE = 512
ATTN_K_TILE = 256
ATTN_QK_LEAD = 2
FFN_CHUNK = 256
CONV_STAGE_BUFFERS = 4
FFN_UP_LEAD = 2
PAGES_PER_CHUNK = 16
SAMPLE_DMA_LEAD = 2
SAMPLE_SLOTS = SAMPLE_DMA_LEAD + 2
VMEM_LIMIT = 56 * 1024 * 1024
LOG2E = float(np.log2(np.e))


def _rms(x, g):
    r = lax.rsqrt(jnp.mean(x * x, axis=-1, keepdims=True) + EPS)
    return (x * r) * g


def _gelu(x):
    return 0.5 * x * (1.0 + lax.erf(x * np.float32(np.sqrt(0.5))))


def _silu(x):
    return x * (1.0 / (1.0 + jnp.exp(-x)))


def _dot(a, b):
    return jnp.dot(a, b, preferred_element_type=F32)


def _dot_nt(a, b):
    return lax.dot_general(a, b, (((1,), (1,)), ((), ())), preferred_element_type=F32)


def _const_spec(shape, single_buffer=False):
    nd = len(shape)
    mode = pl.Buffered(1) if single_buffer else None
    return pl.BlockSpec(shape, lambda *_: (0,) * nd, pipeline_mode=mode)


def _params(semantics):
    return pltpu.CompilerParams(dimension_semantics=semantics, vmem_limit_bytes=VMEM_LIMIT)


def _front_common(x, g_mix, w_in, g_sgu, g_q, g_kv, w_q12, tqc, tqs, tk, d_a, q_rank, kv_rank, n_heads):
    h = _rms(x, g_mix).astype(BF16)
    proj = _dot(h, w_in)
    o1, o2 = d_a, 2 * d_a
    o3 = o2 + q_rank
    o4 = o3 + kv_rank
    u = _gelu(proj[:, :o1])
    v = _rms(_gelu(proj[:, o1:o2]), g_sgu)
    c_q = _rms(proj[:, o2:o3], g_q)
    c_kv = _rms(proj[:, o3:o4], g_kv)
    y = proj[:, o4:o4 + LANE] * tk
    kr = y + pltpu.roll(y, LANE // 2, 1)
    q12 = _dot(c_q.astype(BF16), w_q12)
    nq = n_heads * HEAD_SLOT
    q_parts = []
    for hd in range(n_heads):
        a = q12[:, hd * HEAD_SLOT:(hd + 1) * HEAD_SLOT]
        b = q12[:, nq + hd * HEAD_SLOT:nq + (hd + 1) * HEAD_SLOT]
        q_parts.append(a * tqc + b * tqs)
    return u, v, jnp.concatenate(q_parts, axis=1), c_kv, kr


def _prompt_front_kernel(x_ref, g_mix_ref, w_in_ref, g_sgu_ref, g_q_ref, g_kv_ref, w_q12_ref,
                         w_ukp_ref, w_uvt_ref, w_sp_ref, b_sp_ref, g_out_a_ref,
                         tqc_ref, tqs_ref, tk_ref,
                         out_a_ref, q_ref, k_ref, vt_ref, ckv_ref, kr_ref,
                         *, d_a, q_rank, kv_rank, n_heads, n_heads_a, chunk, rope_dim):
    tm = x_ref.shape[0]
    u, v, q_full, c_kv, kr = _front_common(
        x_ref[...], g_mix_ref[...], w_in_ref[...], g_sgu_ref[...], g_q_ref[...], g_kv_ref[...],
        w_q12_ref[...], tqc_ref[...], tqs_ref[...], tk_ref[...], d_a, q_rank, kv_rank, n_heads)
    q_ref[...] = q_full.astype(BF16)
    ckv_ref[...] = c_kv
    kr_ref[...] = kr.T[:rope_dim, :]

    c_bf = c_kv.astype(BF16)
    lane = lax.broadcasted_iota(jnp.int32, (tm, LANE), 1)
    kr_slot = jnp.where(lane >= LANE // 2, kr, 0.0)
    k_nope = _dot(c_bf, w_ukp_ref[...])
    k_parts = [k_nope[:, hd * HEAD_SLOT:(hd + 1) * HEAD_SLOT] + kr_slot for hd in range(n_heads)]
    k_ref[...] = jnp.concatenate(k_parts, axis=1).astype(BF16)
    v_t = _dot_nt(w_uvt_ref[...], c_bf).astype(BF16)
    tk = vt_ref.shape[2]
    for c in range(tm // tk):
        vt_ref[c] = v_t[:, c * tk:(c + 1) * tk]

    hd_a = d_a // n_heads_a
    heads_per_slab = LANE // hd_a
    row = lax.broadcasted_iota(jnp.int32, (chunk, chunk), 0)
    col = lax.broadcasted_iota(jnp.int32, (chunk, chunk), 1)
    w_causal = [jnp.where(row >= col, w_sp_ref[hd], 0.0).astype(BF16) for hd in range(n_heads_a)]
    v_bf = v.astype(BF16)
    lane_c = lax.broadcasted_iota(jnp.int32, (chunk, LANE), 1)
    gate_rows = []
    for c in range(tm // chunk):
        slabs = []
        for j in range(d_a // LANE):
            x_slab = v_bf[c * chunk:(c + 1) * chunk, j * LANE:(j + 1) * LANE]
            mixed = None
            for t in range(heads_per_slab):
                s_t = _dot(w_causal[j * heads_per_slab + t], x_slab)
                in_head = (lane_c >= t * hd_a) & (lane_c < (t + 1) * hd_a)
                mixed = s_t if mixed is None else jnp.where(in_head, s_t, mixed)
            slabs.append(mixed)
        gate_rows.append(jnp.concatenate(slabs, axis=1) + b_sp_ref[...])
    s = jnp.concatenate(gate_rows, axis=0)
    out_a_ref[...] = _rms(u * s, g_out_a_ref[...]).astype(BF16)


def _prompt_attn_kernel(q_ref, k_ref, vt_ref, o_ref, *, scale, v_head_dim):
    tq = q_ref.shape[0]
    tk = vt_ref.shape[2]
    i = pl.program_id(1)
    n_heads = q_ref.shape[1] // HEAD_SLOT
    c2 = scale * LOG2E
    q_t = [q_ref[:, t * HEAD_SLOT:(t + 1) * HEAD_SLOT].astype(F32).T.astype(BF16)
           for t in range(n_heads)]
    blocks_per_step = tq // tk

    def step(g, carry, masked):
        units = [(d, t) for d in range(blocks_per_step) for t in range(n_heads)]

        def scores(d, t):
            lo = d * tk if masked else 0
            start = pl.multiple_of((g * blocks_per_step + d) * tk, tk)
            ks = k_ref[pl.ds(start, tk), t * HEAD_SLOT:(t + 1) * HEAD_SLOT]
            s_t = _dot(ks, q_t[t][:, lo:])
            if not masked:
                return s_t
            kpos = start + lax.broadcasted_iota(jnp.int32, s_t.shape, 0)
            qpos = i * tq + lo + lax.broadcasted_iota(jnp.int32, s_t.shape, 1)
            return jnp.where(kpos <= qpos, s_t, -jnp.inf)

        s_ahead = [scores(*u) for u in units[:ATTN_QK_LEAD]]
        state = list(carry)
        for n, (d, t) in enumerate(units):
            lo = d * tk if masked else 0
            m_all, acc_all = state[t]
            m, acc = m_all[0:1, lo:], acc_all[:, lo:]
            if n + ATTN_QK_LEAD < len(units):
                s_ahead.append(scores(*units[n + ATTN_QK_LEAD]))
            s_t, s_ahead[n] = s_ahead[n], None
            m_new = jnp.maximum(m, jnp.max(s_t, axis=0, keepdims=True))
            p_t = jnp.exp2((s_t - m_new) * c2).astype(BF16)
            alpha = jnp.exp2((m - m_new) * c2)
            m_new = jnp.broadcast_to(m_new, (SUBLANE, m_new.shape[1]))
            v_t = jnp.concatenate(
                [vt_ref[g * blocks_per_step + d, t * v_head_dim:(t + 1) * v_head_dim, :], ones_rows], axis=0)
            acc_new = alpha * acc + _dot(v_t, p_t)
            if lo:
                m_new = jnp.concatenate([m_all[:, :lo], m_new], axis=1)
                acc_new = jnp.concatenate([acc_all[:, :lo], acc_new], axis=1)
            state[t] = (m_new, acc_new)
        return tuple(state)

    ones_rows = jnp.ones((BF16_SUBLANES, tk), BF16)
    init = tuple((jnp.full((SUBLANE, tq), -jnp.inf, F32), jnp.zeros((v_head_dim + BF16_SUBLANES, tq), F32))
                 for _ in range(n_heads))
    carry = lax.fori_loop(0, i, functools.partial(step, masked=False), init)
    carry = step(i, carry, True)
    o_t = jnp.concatenate([acc[:v_head_dim] / acc[v_head_dim:v_head_dim + 1] for _, acc in carry], axis=0)
    o_ref[...] = o_t.T


def _back_common(x, out_a, o_raw, g_out_b, w_out_ref, g_ffn, w_up_ref, w_conv_ref, b_conv_ref,
                 w_down_ref, g_final, conv_inputs, d_ff, after_chunk=None):
    d_a = out_a.shape[1]
    ob = _rms(o_raw, g_out_b).astype(BF16)
    mix = _dot(out_a, w_out_ref[:d_a, :]) + _dot(ob, w_out_ref[d_a:, :])
    x1 = x + mix
    h2 = _rms(x1, g_ffn).astype(BF16)
    n_chunks = d_ff // FFN_CHUNK

    def up_proj(j):
        cols = [slice(base + j * FFN_CHUNK, base + (j + 1) * FFN_CHUNK) for base in (0, d_ff)]
        return [(_dot(h2, w_up_ref[:, c]), c) for c in cols]

    f = None
    pending = [up_proj(j) for j in range(min(FFN_UP_LEAD, n_chunks))]
    for j in range(n_chunks):
        if j + FFN_UP_LEAD < n_chunks:
            pending.append(up_proj(j + FFN_UP_LEAD))
        conv = []
        for half, (up, cols) in enumerate(pending[j]):
            r2, r1 = conv_inputs(up, cols, 2 * j + half)
            conv.append(b_conv_ref[:, cols] + r2 * w_conv_ref[0:1, cols] + r1 * w_conv_ref[1:2, cols]
                        + up * w_conv_ref[2:3, cols])
        act = (_silu(conv[0]) * conv[1]).astype(BF16)
        part = _dot(act, w_down_ref[j * FFN_CHUNK:(j + 1) * FFN_CHUNK, :])
        f = part if f is None else f + part
        pending[j] = None
        if after_chunk is not None:
            after_chunk(j, n_chunks)
    return _rms(x1 + f, g_final)


def _sample_front_kernel(x_ref, g_mix_ref, w_in_ref, g_sgu_ref, g_q_ref, g_kv_ref, w_q12_ref,
                         w_ukt_ref, w_sp0_ref, b_sp0_ref, g_out_a_ref, tqc_ref, tqs_ref, tk_ref,
                         out_a_ref, v_ref, ckv_ref, kr_ref, qlat_ref, qrope_ref,
                         *, d_a, q_rank, kv_rank, n_heads, nope_dim, rope_dim):
    u, v, q_full, c_kv, kr = _front_common(
        x_ref[...], g_mix_ref[...], w_in_ref[...], g_sgu_ref[...], g_q_ref[...], g_kv_ref[...],
        w_q12_ref[...], tqc_ref[...], tqs_ref[...], tk_ref[...], d_a, q_rank, kv_rank, n_heads)
    v_ref[...] = v
    ckv_ref[...] = c_kv
    kr_ref[...] = kr[:, :rope_dim]
    s = v * w_sp0_ref[...] + b_sp0_ref[...]
    out_a_ref[...] = _rms(u * s, g_out_a_ref[...]).astype(BF16)
    q_bf = q_full.astype(BF16)
    for hd in range(n_heads):
        q_h = q_bf[:, hd * HEAD_SLOT:(hd + 1) * HEAD_SLOT]
        qlat_ref[hd] = _dot(q_h, w_ukt_ref[hd])
        qrope_ref[hd] = q_full[:, hd * HEAD_SLOT + nope_dim:hd * HEAD_SLOT + nope_dim + rope_dim]


def _paged_decoder(step, pt_ref, qlat_ref, qrope_ref, cnew_ref, krnew_ref, c_hbm, krt_hbm,
                   o_ref, cbuf, krbuf, s_ref, sem_c, sem_k, *, scale, pages_per_chunk, page_size):
    seqs = o_ref.shape[0]
    n_seq = qlat_ref.shape[0]
    chunks = pt_ref.shape[1] // pages_per_chunk
    n_units = seqs * chunks
    n_slots = cbuf.shape[0]
    chunk_keys = pages_per_chunk * page_size
    c2 = scale * LOG2E
    first = step * seqs
    state = {}

    def owner(u):
        k = u // chunks
        if u < n_units:
            return first + k, u % chunks
        return jnp.minimum(first + k, n_seq - 1), u % chunks

    def start_chunk(seq, ch, slot):
        for i in range(pages_per_chunk):
            page = pt_ref[seq, ch * pages_per_chunk + i]
            queue = i % 2
            pltpu.make_async_copy(c_hbm.at[0, page], cbuf.at[slot, i], sem_c.at[slot]).start(priority=queue)
            pltpu.make_async_copy(krt_hbm.at[0, page], krbuf.at[slot, i], sem_k.at[slot]).start(priority=queue)

    def wait_chunk(slot):
        first_pages = pl.ds(0, pages_per_chunk)
        pltpu.make_async_copy(c_hbm.at[0, first_pages], cbuf.at[slot], sem_c.at[slot]).wait()
        pltpu.make_async_copy(krt_hbm.at[0, first_pages], krbuf.at[slot], sem_k.at[slot]).wait()

    def latent(slot):
        return cbuf[slot].reshape(chunk_keys, cbuf.shape[3]).astype(BF16)

    def scores(seq, slot):
        s_lat = _dot_nt(qlat_ref[seq].astype(BF16), latent(slot))
        q_rope = qrope_ref[seq].astype(BF16)
        s_rope = jnp.concatenate(
            [_dot(q_rope, krbuf[slot, i].astype(BF16)) for i in range(pages_per_chunk)], axis=1)
        return s_lat + s_rope

    def prologue():
        for u in range(SAMPLE_DMA_LEAD):
            start_chunk(*owner(u), u % n_slots)
        wait_chunk(0)
        s_ref[...] = scores(owner(0)[0], 0)

    def begin_sequence(k):
        q_lat = qlat_ref[first + k]
        c_new = cnew_ref[k]
        state["m"] = (jnp.sum(q_lat * c_new, axis=-1, keepdims=True)
                      + jnp.sum(qrope_ref[first + k] * krnew_ref[k], axis=-1, keepdims=True))
        state["l"] = jnp.ones_like(state["m"])
        state["acc"] = jnp.broadcast_to(c_new, q_lat.shape)

    unit_scores = {}

    def score_next(u):
        if u == 0:
            unit_scores[0] = s_ref[...]
        ahead = u + SAMPLE_DMA_LEAD
        start_chunk(*owner(ahead), ahead % n_slots)
        wait_chunk((u + 1) % n_slots)
        unit_scores[u + 1] = scores(owner(u + 1)[0], (u + 1) % n_slots)
        if u == n_units - 1:
            s_ref[...] = unit_scores.pop(n_units)

    def update(u):
        if u % chunks == 0:
            begin_sequence(u // chunks)
        m, l, acc, s = state["m"], state["l"], state["acc"], unit_scores.pop(u)
        m_new = jnp.maximum(m, jnp.max(s, axis=-1, keepdims=True))
        p = jnp.exp2((s - m_new) * c2)
        alpha = jnp.exp2((m - m_new) * c2)
        state["l"] = alpha * l + jnp.sum(p, axis=-1, keepdims=True)
        state["acc"] = alpha * acc + _dot(p.astype(BF16), latent(u % n_slots))
        state["m"] = m_new
        if u % chunks == chunks - 1:
            o_ref[u // chunks] = state["acc"] / state["l"]

    def epilogue():
        for d in range(1, SAMPLE_DMA_LEAD):
            wait_chunk((n_units + d) % n_slots)

    return prologue, score_next, update, epilogue, n_units


def _prompt_back_decode_kernel(pt_ref, x_ref, out_a_ref, o_ref, g_out_b_ref, w_out_ref, g_ffn_ref, w_up_ref,
                               w_conv_ref, b_conv_ref, w_down_ref, g_final_ref,
                               qlat_ref, qrope_ref, cnew_ref, krnew_ref, c_hbm, krt_hbm,
                               y_ref, conv_ref, olat_ref,
                               tail_ref, stage_ref, cbuf, krbuf, s_ref, sem_c, sem_k,
                               *, tiles_per_seq, d_ff, conv_w, scale, pages_per_chunk, page_size):
    tm = x_ref.shape[0]
    keep = conv_w - 1
    step = pl.program_id(0)
    prologue, score_next, update, epilogue, n_units = _paged_decoder(
        step, pt_ref, qlat_ref, qrope_ref, cnew_ref, krnew_ref, c_hbm, krt_hbm,
        olat_ref, cbuf, krbuf, s_ref, sem_c, sem_k, scale=scale, pages_per_chunk=pages_per_chunk,
        page_size=page_size)
    pl.when(step == 0)(prologue)

    @pl.when(step % tiles_per_seq == 0)
    def _():
        tail_ref[...] = jnp.zeros_like(tail_ref)

    def conv_inputs(up, cols, k):
        stage = stage_ref.at[k % stage_ref.shape[0]]
        stage[0:SUBLANE, :] = tail_ref[:, cols]
        stage[SUBLANE:, :] = up
        tail_ref[:, cols] = up[tm - SUBLANE:, :]
        conv_ref[:, cols] = up[tm - keep:, :]
        return stage[pl.ds(SUBLANE - 2, tm), :], stage[pl.ds(SUBLANE - 1, tm), :]

    def units_of(j, n_chunks):
        return range(j * n_units // n_chunks, (j + 1) * n_units // n_chunks)

    def after_chunk(j, n_chunks):
        for u in units_of(j, n_chunks):
            score_next(u)
            update(u)

    y_ref[...] = _back_common(
        x_ref[...], out_a_ref[...], o_ref[...], g_out_b_ref[...], w_out_ref, g_ffn_ref[...], w_up_ref,
        w_conv_ref, b_conv_ref, w_down_ref, g_final_ref[...], conv_inputs, d_ff, after_chunk=after_chunk)
    pl.when(step == pl.num_programs(0) - 1)(epilogue)


def _sample_back_kernel(x_ref, out_a_ref, olat_ref, w_uvp_ref, prev_ref, g_out_b_ref, w_out_ref, g_ffn_ref,
                        w_up_ref, w_conv_ref, b_conv_ref, w_down_ref, g_final_ref,
                        y_ref, up_ref, *, n_heads, d_ff):
    o_raw = None
    for hd in range(n_heads):
        part = _dot(olat_ref[hd].astype(BF16), w_uvp_ref[hd])
        o_raw = part if o_raw is None else o_raw + part

    def conv_inputs(up, cols, k):
        del k
        up_ref[:, cols] = up
        return prev_ref[0, :, cols], prev_ref[1, :, cols]

    y_ref[...] = _back_common(
        x_ref[...], out_a_ref[...], o_raw, g_out_b_ref[...], w_out_ref, g_ffn_ref[...], w_up_ref,
        w_conv_ref, b_conv_ref, w_down_ref, g_final_ref[...], conv_inputs, d_ff)


def _rope_tables(pos, rope_dim, nope_dim):
    half = rope_dim // 2
    inv = ROPE_THETA ** (-jnp.arange(half, dtype=F32) / half)
    ang = pos.astype(F32)[:, None] * inv
    cos, sin = jnp.cos(ang), jnp.sin(ang)
    n = pos.shape[0]
    z = lambda w: jnp.zeros((n, w), F32)
    pad = LANE - nope_dim - rope_dim
    tqc = jnp.concatenate([jnp.ones((n, nope_dim), F32), cos, cos, z(pad)], axis=1)
    tqs = jnp.concatenate([z(nope_dim), -sin, sin, z(pad)], axis=1)
    tk = jnp.concatenate([-sin, sin, z(LANE // 2 - rope_dim), cos, cos, z(LANE // 2 - rope_dim)], axis=1)
    return tqc, tqs, tk


def _pack_weights(w_in, w_uq, w_uk, w_uv, d_a, q_rank, kv_rank, rope_dim, nope_dim):
    half = rope_dim // 2
    d_model = w_in.shape[0]
    n_heads = w_uq.shape[1]
    o4 = 2 * d_a + q_rank + kv_rank
    x1, x2 = w_in[:, o4:o4 + half], w_in[:, o4 + half:o4 + rope_dim]
    zc = lambda rows, w: jnp.zeros((rows, w), w_in.dtype)
    gap = LANE // 2 - rope_dim
    w_in_p = jnp.concatenate([w_in[:, :o4], x2, x1, zc(d_model, gap), x1, x2, zc(d_model, gap)], axis=1)
    pad = HEAD_SLOT - nope_dim - rope_dim
    main, swap = [], []
    for hd in range(n_heads):
        nope = w_uq[:, hd, :nope_dim]
        r1, r2 = w_uq[:, hd, nope_dim:nope_dim + half], w_uq[:, hd, nope_dim + half:]
        main.append(jnp.concatenate([nope, r1, r2, zc(q_rank, pad)], axis=1))
        swap.append(jnp.concatenate([zc(q_rank, nope_dim), r2, r1, zc(q_rank, pad)], axis=1))
    w_q12 = jnp.concatenate(main + swap, axis=1)
    w_ukp = jnp.concatenate(
        [jnp.concatenate([w_uk[:, hd, :], zc(kv_rank, HEAD_SLOT - nope_dim)], axis=1) for hd in range(n_heads)],
        axis=1)
    w_ukt = jnp.stack(
        [jnp.concatenate([w_uk[:, hd, :].T, jnp.zeros((HEAD_SLOT - nope_dim, kv_rank), w_uk.dtype)], axis=0)
         for hd in range(n_heads)])
    v_dim = w_uv.shape[2]
    w_uv_flat = w_uv.reshape(kv_rank, n_heads * v_dim)
    w_uvp = jnp.stack(
        [jnp.pad(w_uv[:, hd, :], ((0, 0), (hd * v_dim, (n_heads - 1 - hd) * v_dim))) for hd in range(n_heads)])
    cast = lambda a: a.astype(BF16)
    return cast(w_in_p), cast(w_q12), cast(w_ukp), cast(w_ukt), cast(w_uv_flat), cast(w_uvp)


def kernel(x_prompt, x_sample, cache_kv_latent, cache_k_rope, state_ffn_conv, page_table,
           g_mix, w_in, g_sgu, w_spatial, b_spatial, g_q, w_uq, g_kv, w_uk, w_uv,
           g_out_a, g_out_b, w_out, g_ffn, w_up, w_conv, b_conv, w_down, g_final):
    B, S, D = x_prompt.shape
    DB, T, _ = x_sample.shape
    depth = w_in.shape[0]
    assert depth == 1 and T == 1
    n_heads_a, chunk = w_spatial.shape[1], w_spatial.shape[2]
    d_a = g_sgu.shape[1]
    q_rank, kv_rank = g_q.shape[1], g_kv.shape[1]
    n_heads, nope_dim, v_dim = w_uk.shape[2], w_uk.shape[3], w_uv.shape[3]
    rope_dim = w_uq.shape[3] - nope_dim
    d_b = n_heads * v_dim
    d_ff = w_down.shape[1]
    conv_w = w_conv.shape[1]
    n_pool, page_size = cache_kv_latent.shape[1], cache_kv_latent.shape[2]
    n_pages = page_table.shape[1]
    past_len = n_pages * page_size
    scale = float((nope_dim + rope_dim) ** -0.5)
    assert d_ff % FFN_CHUNK == 0 and S % PROMPT_TILE == 0 and PROMPT_TILE % chunk == 0
    assert n_pages % (SAMPLE_SLOTS * PAGES_PER_CHUNK) == 0 and conv_w == 3
    assert S % ATTN_Q_TILE == 0 and ATTN_Q_TILE % ATTN_K_TILE == 0 and PROMPT_TILE % ATTN_K_TILE == 0

    row2 = lambda a: a.reshape(1, -1)
    w_in_p, w_q12, w_ukp, w_ukt, w_uv_flat, w_uvp = _pack_weights(
        w_in[0], w_uq[0], w_uk[0], w_uv[0], d_a, q_rank, kv_rank, rope_dim, nope_dim)
    w_out_bf, w_up_bf, w_down_bf = w_out[0].astype(BF16), w_up[0].astype(BF16), w_down[0].astype(BF16)
    gm, gs, gq, gkv = row2(g_mix[0]), row2(g_sgu[0]), row2(g_q[0]), row2(g_kv[0])
    goa, gob, gf, gfin = row2(g_out_a[0]), row2(g_out_b[0]), row2(g_ffn[0]), row2(g_final)
    wc, bc = w_conv[0], row2(b_conv[0])
    hd_a = d_a // n_heads_a

    n_rows = B * S
    tm = PROMPT_TILE
    tkb = ATTN_K_TILE
    tiles_per_seq = S // tm
    xp = x_prompt.reshape(n_rows, D)
    tqc, tqs, tk = _rope_tables(jnp.arange(S, dtype=jnp.int32), rope_dim, nope_dim)
    b_sp = jnp.repeat(b_spatial[0][:, :chunk].T, hd_a, axis=1)
    n_in = w_in_p.shape[1]
    nq = n_heads * HEAD_SLOT
    rows = lambda w: pl.BlockSpec((tm, w), lambda i: (i, 0))
    table = pl.BlockSpec((tm, LANE), lambda i: (i % tiles_per_seq, 0))
    front = pl.pallas_call(
        functools.partial(_prompt_front_kernel, d_a=d_a, q_rank=q_rank, kv_rank=kv_rank, n_heads=n_heads,
                          n_heads_a=n_heads_a, chunk=chunk, rope_dim=rope_dim),
        grid=(n_rows // tm,),
        in_specs=[rows(D), _const_spec((1, D)), _const_spec((D, n_in), True), _const_spec((1, d_a)),
                  _const_spec((1, q_rank)), _const_spec((1, kv_rank)), _const_spec((q_rank, 2 * nq), True),
                  _const_spec((kv_rank, nq), True), _const_spec((d_b, kv_rank), True),
                  _const_spec((n_heads_a, chunk, chunk), True), _const_spec((chunk, d_a)), _const_spec((1, d_a)),
                  table, table, table],
        out_specs=[rows(d_a), rows(nq), rows(nq), pl.BlockSpec((tm // tkb, d_b, tkb), lambda i: (i, 0, 0)),
                   rows(kv_rank),
                   pl.BlockSpec((None, rope_dim, tm), lambda i: (i // tiles_per_seq, 0, i % tiles_per_seq))],
        out_shape=[jax.ShapeDtypeStruct((n_rows, d_a), BF16), jax.ShapeDtypeStruct((n_rows, nq), BF16),
                   jax.ShapeDtypeStruct((n_rows, nq), BF16), jax.ShapeDtypeStruct((n_rows // tkb, d_b, tkb), BF16),
                   jax.ShapeDtypeStruct((n_rows, kv_rank), F32), jax.ShapeDtypeStruct((B, rope_dim, S), F32)],
        compiler_params=_params(("arbitrary",)),
        name="prompt_front",
    )
    out_a_p, q_p, k_p, vt_p, ckv_p, kr_p = front(
        xp, gm, w_in_p, gs, gq, gkv, w_q12, w_ukp, w_uv_flat.T, w_spatial[0][:, :chunk, :chunk], b_sp, goa,
        tqc, tqs, tk)

    tq = ATTN_Q_TILE
    nqb = S // tq
    o_p = pl.pallas_call(
        functools.partial(_prompt_attn_kernel, scale=scale, v_head_dim=v_dim),
        grid=(B, nqb),
        in_specs=[pl.BlockSpec((tq, nq), lambda b, i: (b * nqb + i, 0)),
                  pl.BlockSpec((S, nq), lambda b, i: (b, 0)),
                  pl.BlockSpec((S // tkb, d_b, tkb), lambda b, i: (b, 0, 0))],
        out_specs=pl.BlockSpec((tq, d_b), lambda b, i: (b * nqb + i, 0)),
        out_shape=jax.ShapeDtypeStruct((n_rows, d_b), F32),
        compiler_params=_params(("arbitrary", "arbitrary")),
        name="prompt_attn",
    )(q_p, k_p, vt_p)

    xs = x_sample.reshape(DB, D)
    sqc, sqs, sk = _rope_tables(past_len + jnp.arange(T, dtype=jnp.int32), rope_dim, nope_dim)
    w_sp0 = jnp.repeat(w_spatial[0][:, 0, 0], hd_a).reshape(1, d_a)
    b_sp0 = jnp.repeat(b_spatial[0][:, 0], hd_a).reshape(1, d_a)
    full = lambda *shape: _const_spec(shape, True)
    whole = lambda *shape: _const_spec(shape)
    out_a_s, v_s, ckv_s, kr_s, qlat_h, qrope_h = pl.pallas_call(
        functools.partial(_sample_front_kernel, d_a=d_a, q_rank=q_rank, kv_rank=kv_rank, n_heads=n_heads,
                          nope_dim=nope_dim, rope_dim=rope_dim),
        grid=(1,),
        in_specs=[full(DB, D), full(1, D), full(D, n_in), full(1, d_a), full(1, q_rank), full(1, kv_rank),
                  full(q_rank, 2 * nq), full(n_heads, HEAD_SLOT, kv_rank), full(1, d_a), full(1, d_a),
                  full(1, d_a), full(1, LANE), full(1, LANE), full(1, LANE)],
        out_specs=[whole(DB, d_a), whole(DB, d_a), whole(DB, kv_rank), whole(DB, rope_dim),
                   whole(n_heads, DB, kv_rank), whole(n_heads, DB, rope_dim)],
        out_shape=[jax.ShapeDtypeStruct((DB, d_a), BF16), jax.ShapeDtypeStruct((DB, d_a), F32),
                   jax.ShapeDtypeStruct((DB, kv_rank), F32), jax.ShapeDtypeStruct((DB, rope_dim), F32),
                   jax.ShapeDtypeStruct((n_heads, DB, kv_rank), F32),
                   jax.ShapeDtypeStruct((n_heads, DB, rope_dim), F32)],
        compiler_params=_params(("arbitrary",)),
        name="sample_front",
    )(xs, gm, w_in_p, gs, gq, gkv, w_q12, w_ukt, w_sp0, b_sp0, goa, sqc, sqs, sk)

    P = PAGES_PER_CHUNK
    chunk_keys = P * page_size
    n_steps = n_rows // tm
    seqs = DB // n_steps
    assert DB % n_steps == 0 and (seqs * (n_pages // P)) % SAMPLE_SLOTS == 0 and SAMPLE_DMA_LEAD < n_pages // P
    cache_krt = jnp.swapaxes(cache_k_rope, 2, 3)
    any_space = pl.BlockSpec(memory_space=pl.ANY)
    tile = lambda w: pl.BlockSpec((tm, w), lambda i, pt: (i, 0))
    per_step = lambda *tail: pl.BlockSpec((seqs,) + tail, lambda i, pt: (i,) + (0,) * len(tail))
    y_p, conv_p, o_lat = pl.pallas_call(
        functools.partial(_prompt_back_decode_kernel, tiles_per_seq=tiles_per_seq, d_ff=d_ff, conv_w=conv_w,
                          scale=scale, pages_per_chunk=P, page_size=page_size),
        grid_spec=pltpu.PrefetchScalarGridSpec(
            num_scalar_prefetch=1,
            grid=(n_steps,),
            in_specs=[tile(D), tile(d_a), tile(d_b),
                      _const_spec((1, d_b)), _const_spec((d_a + d_b, D), True), _const_spec((1, D)),
                      _const_spec((D, 2 * d_ff), True), _const_spec((conv_w, 2 * d_ff)),
                      _const_spec((1, 2 * d_ff)), _const_spec((d_ff, D), True), _const_spec((1, D)),
                      whole(DB, n_heads, kv_rank), whole(DB, n_heads, rope_dim),
                      per_step(1, kv_rank), per_step(1, rope_dim), any_space, any_space],
            out_specs=[tile(D),
                       pl.BlockSpec((None, conv_w - 1, 2 * d_ff), lambda i, pt: (i // tiles_per_seq, 0, 0)),
                       per_step(n_heads, kv_rank)],
            scratch_shapes=[pltpu.VMEM((SUBLANE, 2 * d_ff), F32),
                            pltpu.VMEM((CONV_STAGE_BUFFERS, tm + SUBLANE, FFN_CHUNK), F32),
                            pltpu.VMEM((SAMPLE_SLOTS, P, page_size, kv_rank), F32),
                            pltpu.VMEM((SAMPLE_SLOTS, P, rope_dim, page_size), F32),
                            pltpu.VMEM((n_heads, chunk_keys), F32),
                            pltpu.SemaphoreType.DMA((SAMPLE_SLOTS,)), pltpu.SemaphoreType.DMA((SAMPLE_SLOTS,))],
        ),
        out_shape=[jax.ShapeDtypeStruct((n_rows, D), F32),
                   jax.ShapeDtypeStruct((B, conv_w - 1, 2 * d_ff), F32),
                   jax.ShapeDtypeStruct((DB, n_heads, kv_rank), F32)],
        compiler_params=_params(("arbitrary",)),
        name="prompt_back_sample_attn",
    )(page_table, xp, out_a_p, o_p, gob, w_out_bf, gf, w_up_bf, wc, bc, w_down_bf, gfin,
      qlat_h.transpose(1, 0, 2), qrope_h.transpose(1, 0, 2), ckv_s.reshape(DB, 1, kv_rank),
      kr_s.reshape(DB, 1, rope_dim), cache_kv_latent, cache_krt)

    prev = state_ffn_conv[0].transpose(1, 0, 2)
    y_s, up_s = pl.pallas_call(
        functools.partial(_sample_back_kernel, n_heads=n_heads, d_ff=d_ff),
        grid=(1,),
        in_specs=[full(DB, D), full(DB, d_a), full(n_heads, DB, kv_rank), full(n_heads, kv_rank, d_b),
                  full(conv_w - 1, DB, 2 * d_ff), full(1, d_b), full(d_a + d_b, D), full(1, D),
                  full(D, 2 * d_ff), full(conv_w, 2 * d_ff), full(1, 2 * d_ff), full(d_ff, D), full(1, D)],
        out_specs=[whole(DB, D), whole(DB, 2 * d_ff)],
        out_shape=[jax.ShapeDtypeStruct((DB, D), F32), jax.ShapeDtypeStruct((DB, 2 * d_ff), F32)],
        compiler_params=_params(("arbitrary",)),
        name="sample_back",
    )(xs, out_a_s, o_lat.transpose(1, 0, 2), w_uvp, prev, gob, w_out_bf, gf, w_up_bf, wc, bc, w_down_bf, gfin)

    conv_s = jnp.concatenate([state_ffn_conv[0][:, 1:, :], up_s[:, None, :]], axis=1)
    return (y_p.reshape(B, S, D), y_s.reshape(DB, T, D),
            ckv_p.reshape(1, B, S, kv_rank), jnp.swapaxes(kr_p, 1, 2).reshape(1, B, S, rope_dim),
            conv_p.reshape(1, B, conv_w - 1, 2 * d_ff),
            ckv_s.reshape(1, DB, T, kv_rank), kr_s.reshape(1, DB, T, rope_dim),
            v_s.reshape(1, DB, T, d_a), conv_s.reshape(1, DB, conv_w - 1, 2 * d_ff))
```

```python
import functools

import jax
import jax.numpy as jnp
import numpy as np
from jax import lax
from jax.experimental import pallas as pl
from jax.experimental.pallas import tpu as pltpu

F32 = jnp.float32
BF16 = jnp.bfloat16

EPS = 1e-6
ROPE_THETA = 10000.0
LANE = 128
SUBLANE = 8
BF16_SUBLANES = 16
HEAD_SLOT = LANE
FRONT_TILE = 1024
PROMPT_TILE = 512
ATTN_Q_TILE = 512
ATTN_K_TILE = 256
ATTN_QK_LEAD = 2
FFN_CHUNK = 256
CONV_STAGE_BUFFERS = 4
FFN_UP_LEAD = 2
PAGES_PER_CHUNK = 16
SAMPLE_DMA_LEAD = 2
SAMPLE_SLOTS = SAMPLE_DMA_LEAD + 2
VMEM_LIMIT = 56 * 1024 * 1024
LOG2E = float(np.log2(np.e))


def _rms(x, g):
    r = lax.rsqrt(jnp.mean(x * x, axis=-1, keepdims=True) + EPS)
    return (x * r) * g


def _gelu(x):
    return 0.5 * x * (1.0 + lax.erf(x * np.float32(np.sqrt(0.5))))


def _silu(x):
    return x * (1.0 / (1.0 + jnp.exp(-x)))


def _dot(a, b):
    return jnp.dot(a, b, preferred_element_type=F32)


def _dot_nt(a, b):
    return lax.dot_general(a, b, (((1,), (1,)), ((), ())), preferred_element_type=F32)


def _const_spec(shape, single_buffer=False):
    nd = len(shape)
    mode = pl.Buffered(1) if single_buffer else None
    return pl.BlockSpec(shape, lambda *_: (0,) * nd, pipeline_mode=mode)


def _params(semantics):
    return pltpu.CompilerParams(dimension_semantics=semantics, vmem_limit_bytes=VMEM_LIMIT)


def _front_common(x, g_mix, w_in, g_sgu, g_q, g_kv, w_q12, tqc, tqs, tk, d_a, q_rank, kv_rank, n_heads):
    h = _rms(x, g_mix).astype(BF16)
    proj = _dot(h, w_in)
    o1, o2 = d_a, 2 * d_a
    o3 = o2 + q_rank
    o4 = o3 + kv_rank
    u = _gelu(proj[:, :o1])
    v = _rms(_gelu(proj[:, o1:o2]), g_sgu)
    c_q = _rms(proj[:, o2:o3], g_q)
    c_kv = _rms(proj[:, o3:o4], g_kv)
    y = proj[:, o4:o4 + LANE] * tk
    kr = y + pltpu.roll(y, LANE // 2, 1)
    q12 = _dot(c_q.astype(BF16), w_q12)
    nq = n_heads * HEAD_SLOT
    q_parts = []
    for hd in range(n_heads):
        a = q12[:, hd * HEAD_SLOT:(hd + 1) * HEAD_SLOT]
        b = q12[:, nq + hd * HEAD_SLOT:nq + (hd + 1) * HEAD_SLOT]
        q_parts.append(a * tqc + b * tqs)
    return u, v, jnp.concatenate(q_parts, axis=1), c_kv, kr


def _prompt_front_kernel(x_ref, g_mix_ref, w_in_ref, g_sgu_ref, g_q_ref, g_kv_ref, w_q12_ref,
                         w_ukp_ref, w_uvt_ref, w_sp_ref, b_sp_ref, g_out_a_ref,
                         tqc_ref, tqs_ref, tk_ref,
                         out_a_ref, q_ref, k_ref, vt_ref, ckv_ref, kr_ref,
                         *, d_a, q_rank, kv_rank, n_heads, n_heads_a, chunk, rope_dim):
    tm = x_ref.shape[0]
    u, v, q_full, c_kv, kr = _front_common(
        x_ref[...], g_mix_ref[...], w_in_ref[...], g_sgu_ref[...], g_q_ref[...], g_kv_ref[...],
        w_q12_ref[...], tqc_ref[...], tqs_ref[...], tk_ref[...], d_a, q_rank, kv_rank, n_heads)
    q_ref[...] = q_full.astype(BF16)
    ckv_ref[...] = c_kv
    kr_ref[...] = kr.T[:rope_dim, :]

    c_bf = c_kv.astype(BF16)
    lane = lax.broadcasted_iota(jnp.int32, (tm, LANE), 1)
    kr_slot = jnp.where(lane >= LANE // 2, kr, 0.0)
    k_nope = _dot(c_bf, w_ukp_ref[...])
    k_parts = [k_nope[:, hd * HEAD_SLOT:(hd + 1) * HEAD_SLOT] + kr_slot for hd in range(n_heads)]
    k_ref[...] = jnp.concatenate(k_parts, axis=1).astype(BF16)
    v_t = _dot_nt(w_uvt_ref[...], c_bf).astype(BF16)
    tk = vt_ref.shape[2]
    for c in range(tm // tk):
        vt_ref[c] = v_t[:, c * tk:(c + 1) * tk]

    hd_a = d_a // n_heads_a
    heads_per_slab = LANE // hd_a
    row = lax.broadcasted_iota(jnp.int32, (chunk, chunk), 0)
    col = lax.broadcasted_iota(jnp.int32, (chunk, chunk), 1)
    w_causal = [jnp.where(row >= col, w_sp_ref[hd], 0.0).astype(BF16) for hd in range(n_heads_a)]
    v_bf = v.astype(BF16)
    lane_c = lax.broadcasted_iota(jnp.int32, (chunk, LANE), 1)
    gate_rows = []
    for c in range(tm // chunk):
        slabs = []
        for j in range(d_a // LANE):
            x_slab = v_bf[c * chunk:(c + 1) * chunk, j * LANE:(j + 1) * LANE]
            mixed = None
            for t in range(heads_per_slab):
                s_t = _dot(w_causal[j * heads_per_slab + t], x_slab)
                in_head = (lane_c >= t * hd_a) & (lane_c < (t + 1) * hd_a)
                mixed = s_t if mixed is None else jnp.where(in_head, s_t, mixed)
            slabs.append(mixed)
        gate_rows.append(jnp.concatenate(slabs, axis=1) + b_sp_ref[...])
    s = jnp.concatenate(gate_rows, axis=0)
    out_a_ref[...] = _rms(u * s, g_out_a_ref[...]).astype(BF16)


def _prompt_attn_kernel(q_ref, k_ref, vt_ref, o_ref, *, scale, v_head_dim):
    tq = q_ref.shape[0]
    tk = vt_ref.shape[2]
    i = pl.program_id(1)
    n_heads = q_ref.shape[1] // HEAD_SLOT
    c2 = scale * LOG2E
    q_t = [q_ref[:, t * HEAD_SLOT:(t + 1) * HEAD_SLOT].astype(F32).T.astype(BF16)
           for t in range(n_heads)]
    blocks_per_step = tq // tk

    def step(g, carry, masked):
        units = [(d, t) for d in range(blocks_per_step) for t in range(n_heads)]

        def scores(d, t):
            lo = d * tk if masked else 0
            start = pl.multiple_of((g * blocks_per_step + d) * tk, tk)
            ks = k_ref[pl.ds(start, tk), t * HEAD_SLOT:(t + 1) * HEAD_SLOT]
            s_t = _dot(ks, q_t[t][:, lo:])
            if not masked:
                return s_t
            kpos = start + lax.broadcasted_iota(jnp.int32, s_t.shape, 0)
            qpos = i * tq + lo + lax.broadcasted_iota(jnp.int32, s_t.shape, 1)
            return jnp.where(kpos <= qpos, s_t, -jnp.inf)

        s_ahead = [scores(*u) for u in units[:ATTN_QK_LEAD]]
        state = list(carry)
        for n, (d, t) in enumerate(units):
            lo = d * tk if masked else 0
            m_all, acc_all = state[t]
            m, acc = m_all[0:1, lo:], acc_all[:, lo:]
            if n + ATTN_QK_LEAD < len(units):
                s_ahead.append(scores(*units[n + ATTN_QK_LEAD]))
            s_t, s_ahead[n] = s_ahead[n], None
            m_new = jnp.maximum(m, jnp.max(s_t, axis=0, keepdims=True))
            p_t = jnp.exp2((s_t - m_new) * c2).astype(BF16)
            alpha = jnp.exp2((m - m_new) * c2)
            m_new = jnp.broadcast_to(m_new, (SUBLANE, m_new.shape[1]))
            v_t = jnp.concatenate(
                [vt_ref[g * blocks_per_step + d, t * v_head_dim:(t + 1) * v_head_dim, :], ones_rows], axis=0)
            acc_new = alpha * acc + _dot(v_t, p_t)
            if lo:
                m_new = jnp.concatenate([m_all[:, :lo], m_new], axis=1)
                acc_new = jnp.concatenate([acc_all[:, :lo], acc_new], axis=1)
            state[t] = (m_new, acc_new)
        return tuple(state)

    ones_rows = jnp.ones((BF16_SUBLANES, tk), BF16)
    init = tuple((jnp.full((SUBLANE, tq), -jnp.inf, F32), jnp.zeros((v_head_dim + BF16_SUBLANES, tq), F32))
                 for _ in range(n_heads))
    carry = lax.fori_loop(0, i, functools.partial(step, masked=False), init)
    carry = step(i, carry, True)
    o_t = jnp.concatenate([acc[:v_head_dim] / acc[v_head_dim:v_head_dim + 1] for _, acc in carry], axis=0)
    o_ref[...] = o_t.T


def _back_common(x, out_a, o_raw, g_out_b, w_out_ref, g_ffn, w_up_ref, w_conv_ref, b_conv_ref,
                 w_down_ref, g_final, conv_inputs, d_ff, after_chunk=None):
    d_a = out_a.shape[1]
    ob = _rms(o_raw, g_out_b).astype(BF16)
    mix = _dot(out_a, w_out_ref[:d_a, :]) + _dot(ob, w_out_ref[d_a:, :])
    x1 = x + mix
    h2 = _rms(x1, g_ffn).astype(BF16)
    n_chunks = d_ff // FFN_CHUNK

    def up_proj(j):
        cols = [slice(base + j * FFN_CHUNK, base + (j + 1) * FFN_CHUNK) for base in (0, d_ff)]
        return [(_dot(h2, w_up_ref[:, c]), c) for c in cols]

    f = None
    pending = [up_proj(j) for j in range(min(FFN_UP_LEAD, n_chunks))]
    for j in range(n_chunks):
        if j + FFN_UP_LEAD < n_chunks:
            pending.append(up_proj(j + FFN_UP_LEAD))
        conv = []
        for half, (up, cols) in enumerate(pending[j]):
            r2, r1 = conv_inputs(up, cols, 2 * j + half)
            conv.append(b_conv_ref[:, cols] + r2 * w_conv_ref[0:1, cols] + r1 * w_conv_ref[1:2, cols]
                        + up * w_conv_ref[2:3, cols])
        act = (_silu(conv[0]) * conv[1]).astype(BF16)
        part = _dot(act, w_down_ref[j * FFN_CHUNK:(j + 1) * FFN_CHUNK, :])
        f = part if f is None else f + part
        pending[j] = None
        if after_chunk is not None:
            after_chunk(j, n_chunks)
    return _rms(x1 + f, g_final)


def _sample_front_kernel(x_ref, g_mix_ref, w_in_ref, g_sgu_ref, g_q_ref, g_kv_ref, w_q12_ref,
                         w_ukt_ref, w_sp0_ref, b_sp0_ref, g_out_a_ref, tqc_ref, tqs_ref, tk_ref,
                         out_a_ref, v_ref, ckv_ref, kr_ref, qlat_ref, qrope_ref,
                         *, d_a, q_rank, kv_rank, n_heads, nope_dim, rope_dim):
    u, v, q_full, c_kv, kr = _front_common(
        x_ref[...], g_mix_ref[...], w_in_ref[...], g_sgu_ref[...], g_q_ref[...], g_kv_ref[...],
        w_q12_ref[...], tqc_ref[...], tqs_ref[...], tk_ref[...], d_a, q_rank, kv_rank, n_heads)
    v_ref[...] = v
    ckv_ref[...] = c_kv
    kr_ref[...] = kr[:, :rope_dim]
    s = v * w_sp0_ref[...] + b_sp0_ref[...]
    out_a_ref[...] = _rms(u * s, g_out_a_ref[...]).astype(BF16)
    q_bf = q_full.astype(BF16)
    for hd in range(n_heads):
        q_h = q_bf[:, hd * HEAD_SLOT:(hd + 1) * HEAD_SLOT]
        qlat_ref[hd] = _dot(q_h, w_ukt_ref[hd])
        qrope_ref[hd] = q_full[:, hd * HEAD_SLOT + nope_dim:hd * HEAD_SLOT + nope_dim + rope_dim]


def _paged_decoder(step, pt_ref, qlat_ref, qrope_ref, cnew_ref, krnew_ref, c_hbm, krt_hbm,
                   o_ref, cbuf, krbuf, s_ref, sem_c, sem_k, *, scale, pages_per_chunk, page_size):
    seqs = o_ref.shape[0]
    n_seq = qlat_ref.shape[0]
    chunks = pt_ref.shape[1] // pages_per_chunk
    n_units = seqs * chunks
    n_slots = cbuf.shape[0]
    chunk_keys = pages_per_chunk * page_size
    c2 = scale * LOG2E
    first = step * seqs
    state = {}

    def owner(u):
        k = u // chunks
        if u < n_units:
            return first + k, u % chunks
        return jnp.minimum(first + k, n_seq - 1), u % chunks

    def start_chunk(seq, ch, slot):
        for i in range(pages_per_chunk):
            page = pt_ref[seq, ch * pages_per_chunk + i]
            queue = i % 2
            pltpu.make_async_copy(c_hbm.at[0, page], cbuf.at[slot, i], sem_c.at[slot]).start(priority=queue)
            pltpu.make_async_copy(krt_hbm.at[0, page], krbuf.at[slot, i], sem_k.at[slot]).start(priority=queue)

    def wait_chunk(slot):
        first_pages = pl.ds(0, pages_per_chunk)
        pltpu.make_async_copy(c_hbm.at[0, first_pages], cbuf.at[slot], sem_c.at[slot]).wait()
        pltpu.make_async_copy(krt_hbm.at[0, first_pages], krbuf.at[slot], sem_k.at[slot]).wait()

    def latent(slot):
        return cbuf[slot].reshape(chunk_keys, cbuf.shape[3]).astype(BF16)

    def scores(seq, slot):
        s_lat = _dot_nt(qlat_ref[seq].astype(BF16), latent(slot))
        q_rope = qrope_ref[seq].astype(BF16)
        s_rope = jnp.concatenate(
            [_dot(q_rope, krbuf[slot, i].astype(BF16)) for i in range(pages_per_chunk)], axis=1)
        return s_lat + s_rope

    def prologue():
        for u in range(SAMPLE_DMA_LEAD):
            start_chunk(*owner(u), u % n_slots)
        wait_chunk(0)
        s_ref[...] = scores(owner(0)[0], 0)

    def begin_sequence(k):
        q_lat = qlat_ref[first + k]
        c_new = cnew_ref[k]
        state["m"] = (jnp.sum(q_lat * c_new, axis=-1, keepdims=True)
                      + jnp.sum(qrope_ref[first + k] * krnew_ref[k], axis=-1, keepdims=True))
        state["l"] = jnp.ones_like(state["m"])
        state["acc"] = jnp.broadcast_to(c_new, q_lat.shape)

    unit_scores = {}

    def score_next(u):
        if u == 0:
            unit_scores[0] = s_ref[...]
        ahead = u + SAMPLE_DMA_LEAD
        start_chunk(*owner(ahead), ahead % n_slots)
        wait_chunk((u + 1) % n_slots)
        unit_scores[u + 1] = scores(owner(u + 1)[0], (u + 1) % n_slots)
        if u == n_units - 1:
            s_ref[...] = unit_scores.pop(n_units)

    def update(u):
        if u % chunks == 0:
            begin_sequence(u // chunks)
        m, l, acc, s = state["m"], state["l"], state["acc"], unit_scores.pop(u)
        m_new = jnp.maximum(m, jnp.max(s, axis=-1, keepdims=True))
        p = jnp.exp2((s - m_new) * c2)
        alpha = jnp.exp2((m - m_new) * c2)
        state["l"] = alpha * l + jnp.sum(p, axis=-1, keepdims=True)
        state["acc"] = alpha * acc + _dot(p.astype(BF16), latent(u % n_slots))
        state["m"] = m_new
        if u % chunks == chunks - 1:
            o_ref[u // chunks] = state["acc"] / state["l"]

    def epilogue():
        for d in range(1, SAMPLE_DMA_LEAD):
            wait_chunk((n_units + d) % n_slots)

    return prologue, score_next, update, epilogue, n_units


def _prompt_back_decode_kernel(pt_ref, x_ref, out_a_ref, o_ref, g_out_b_ref, w_out_ref, g_ffn_ref, w_up_ref,
                               w_conv_ref, b_conv_ref, w_down_ref, g_final_ref,
                               qlat_ref, qrope_ref, cnew_ref, krnew_ref, c_hbm, krt_hbm,
                               y_ref, conv_ref, olat_ref,
                               tail_ref, stage_ref, cbuf, krbuf, s_ref, sem_c, sem_k,
                               *, tiles_per_seq, d_ff, conv_w, scale, pages_per_chunk, page_size):
    tm = x_ref.shape[0]
    keep = conv_w - 1
    step = pl.program_id(0)
    prologue, score_next, update, epilogue, n_units = _paged_decoder(
        step, pt_ref, qlat_ref, qrope_ref, cnew_ref, krnew_ref, c_hbm, krt_hbm,
        olat_ref, cbuf, krbuf, s_ref, sem_c, sem_k, scale=scale, pages_per_chunk=pages_per_chunk,
        page_size=page_size)
    pl.when(step == 0)(prologue)

    @pl.when(step % tiles_per_seq == 0)
    def _():
        tail_ref[...] = jnp.zeros_like(tail_ref)

    def conv_inputs(up, cols, k):
        stage = stage_ref.at[k % stage_ref.shape[0]]
        stage[0:SUBLANE, :] = tail_ref[:, cols]
        stage[SUBLANE:, :] = up
        tail_ref[:, cols] = up[tm - SUBLANE:, :]
        conv_ref[:, cols] = up[tm - keep:, :]
        return stage[pl.ds(SUBLANE - 2, tm), :], stage[pl.ds(SUBLANE - 1, tm), :]

    def units_of(j, n_chunks):
        return range(j * n_units // n_chunks, (j + 1) * n_units // n_chunks)

    def after_chunk(j, n_chunks):
        for u in units_of(j, n_chunks):
            score_next(u)
            update(u)

    y_ref[...] = _back_common(
        x_ref[...], out_a_ref[...], o_ref[...], g_out_b_ref[...], w_out_ref, g_ffn_ref[...], w_up_ref,
        w_conv_ref, b_conv_ref, w_down_ref, g_final_ref[...], conv_inputs, d_ff, after_chunk=after_chunk)
    pl.when(step == pl.num_programs(0) - 1)(epilogue)


def _sample_back_kernel(x_ref, out_a_ref, olat_ref, w_uvp_ref, prev_ref, g_out_b_ref, w_out_ref, g_ffn_ref,
                        w_up_ref, w_conv_ref, b_conv_ref, w_down_ref, g_final_ref,
                        y_ref, up_ref, *, n_heads, d_ff):
    o_raw = None
    for hd in range(n_heads):
        part = _dot(olat_ref[hd].astype(BF16), w_uvp_ref[hd])
        o_raw = part if o_raw is None else o_raw + part

    def conv_inputs(up, cols, k):
        del k
        up_ref[:, cols] = up
        return prev_ref[0, :, cols], prev_ref[1, :, cols]

    y_ref[...] = _back_common(
        x_ref[...], out_a_ref[...], o_raw, g_out_b_ref[...], w_out_ref, g_ffn_ref[...], w_up_ref,
        w_conv_ref, b_conv_ref, w_down_ref, g_final_ref[...], conv_inputs, d_ff)


def _rope_tables(pos, rope_dim, nope_dim):
    half = rope_dim // 2
    inv = ROPE_THETA ** (-jnp.arange(half, dtype=F32) / half)
    ang = pos.astype(F32)[:, None] * inv
    cos, sin = jnp.cos(ang), jnp.sin(ang)
    n = pos.shape[0]
    z = lambda w: jnp.zeros((n, w), F32)
    pad = LANE - nope_dim - rope_dim
    tqc = jnp.concatenate([jnp.ones((n, nope_dim), F32), cos, cos, z(pad)], axis=1)
    tqs = jnp.concatenate([z(nope_dim), -sin, sin, z(pad)], axis=1)
    tk = jnp.concatenate([-sin, sin, z(LANE // 2 - rope_dim), cos, cos, z(LANE // 2 - rope_dim)], axis=1)
    return tqc, tqs, tk


def _pack_weights(w_in, w_uq, w_uk, w_uv, d_a, q_rank, kv_rank, rope_dim, nope_dim):
    half = rope_dim // 2
    d_model = w_in.shape[0]
    n_heads = w_uq.shape[1]
    o4 = 2 * d_a + q_rank + kv_rank
    x1, x2 = w_in[:, o4:o4 + half], w_in[:, o4 + half:o4 + rope_dim]
    zc = lambda rows, w: jnp.zeros((rows, w), w_in.dtype)
    gap = LANE // 2 - rope_dim
    w_in_p = jnp.concatenate([w_in[:, :o4], x2, x1, zc(d_model, gap), x1, x2, zc(d_model, gap)], axis=1)
    pad = HEAD_SLOT - nope_dim - rope_dim
    main, swap = [], []
    for hd in range(n_heads):
        nope = w_uq[:, hd, :nope_dim]
        r1, r2 = w_uq[:, hd, nope_dim:nope_dim + half], w_uq[:, hd, nope_dim + half:]
        main.append(jnp.concatenate([nope, r1, r2, zc(q_rank, pad)], axis=1))
        swap.append(jnp.concatenate([zc(q_rank, nope_dim), r2, r1, zc(q_rank, pad)], axis=1))
    w_q12 = jnp.concatenate(main + swap, axis=1)
    w_ukp = jnp.concatenate(
        [jnp.concatenate([w_uk[:, hd, :], zc(kv_rank, HEAD_SLOT - nope_dim)], axis=1) for hd in range(n_heads)],
        axis=1)
    w_ukt = jnp.stack(
        [jnp.concatenate([w_uk[:, hd, :].T, jnp.zeros((HEAD_SLOT - nope_dim, kv_rank), w_uk.dtype)], axis=0)
         for hd in range(n_heads)])
    v_dim = w_uv.shape[2]
    w_uv_flat = w_uv.reshape(kv_rank, n_heads * v_dim)
    w_uvp = jnp.stack(
        [jnp.pad(w_uv[:, hd, :], ((0, 0), (hd * v_dim, (n_heads - 1 - hd) * v_dim))) for hd in range(n_heads)])
    cast = lambda a: a.astype(BF16)
    return cast(w_in_p), cast(w_q12), cast(w_ukp), cast(w_ukt), cast(w_uv_flat), cast(w_uvp)


def kernel(x_prompt, x_sample, cache_kv_latent, cache_k_rope, state_ffn_conv, page_table,
           g_mix, w_in, g_sgu, w_spatial, b_spatial, g_q, w_uq, g_kv, w_uk, w_uv,
           g_out_a, g_out_b, w_out, g_ffn, w_up, w_conv, b_conv, w_down, g_final):
    B, S, D = x_prompt.shape
    DB, T, _ = x_sample.shape
    depth = w_in.shape[0]
    assert depth == 1 and T == 1
    n_heads_a, chunk = w_spatial.shape[1], w_spatial.shape[2]
    d_a = g_sgu.shape[1]
    q_rank, kv_rank = g_q.shape[1], g_kv.shape[1]
    n_heads, nope_dim, v_dim = w_uk.shape[2], w_uk.shape[3], w_uv.shape[3]
    rope_dim = w_uq.shape[3] - nope_dim
    d_b = n_heads * v_dim
    d_ff = w_down.shape[1]
    conv_w = w_conv.shape[1]
    n_pool, page_size = cache_kv_latent.shape[1], cache_kv_latent.shape[2]
    n_pages = page_table.shape[1]
    past_len = n_pages * page_size
    scale = float((nope_dim + rope_dim) ** -0.5)
    assert d_ff % FFN_CHUNK == 0 and S % PROMPT_TILE == 0 and S % FRONT_TILE == 0 and FRONT_TILE % chunk == 0
    assert n_pages % (SAMPLE_SLOTS * PAGES_PER_CHUNK) == 0 and conv_w == 3
    assert S % ATTN_Q_TILE == 0 and ATTN_Q_TILE % ATTN_K_TILE == 0 and FRONT_TILE % ATTN_K_TILE == 0

    row2 = lambda a: a.reshape(1, -1)
    w_in_p, w_q12, w_ukp, w_ukt, w_uv_flat, w_uvp = _pack_weights(
        w_in[0], w_uq[0], w_uk[0], w_uv[0], d_a, q_rank, kv_rank, rope_dim, nope_dim)
    w_out_bf, w_up_bf, w_down_bf = w_out[0].astype(BF16), w_up[0].astype(BF16), w_down[0].astype(BF16)
    gm, gs, gq, gkv = row2(g_mix[0]), row2(g_sgu[0]), row2(g_q[0]), row2(g_kv[0])
    goa, gob, gf, gfin = row2(g_out_a[0]), row2(g_out_b[0]), row2(g_ffn[0]), row2(g_final)
    wc, bc = w_conv[0], row2(b_conv[0])
    hd_a = d_a // n_heads_a

    n_rows = B * S
    tm = PROMPT_TILE
    tkb = ATTN_K_TILE
    tiles_per_seq = S // tm
    xp = x_prompt.reshape(n_rows, D)
    tqc, tqs, tk = _rope_tables(jnp.arange(S, dtype=jnp.int32), rope_dim, nope_dim)
    b_sp = jnp.repeat(b_spatial[0][:, :chunk].T, hd_a, axis=1)
    n_in = w_in_p.shape[1]
    nq = n_heads * HEAD_SLOT
    tf = FRONT_TILE
    front_tiles_per_seq = S // tf
    rows = lambda w: pl.BlockSpec((tf, w), lambda i: (i, 0))
    table = pl.BlockSpec((tf, LANE), lambda i: (i % front_tiles_per_seq, 0))
    front = pl.pallas_call(
        functools.partial(_prompt_front_kernel, d_a=d_a, q_rank=q_rank, kv_rank=kv_rank, n_heads=n_heads,
                          n_heads_a=n_heads_a, chunk=chunk, rope_dim=rope_dim),
        grid=(n_rows // tf,),
        in_specs=[rows(D), _const_spec((1, D)), _const_spec((D, n_in), True), _const_spec((1, d_a)),
                  _const_spec((1, q_rank)), _const_spec((1, kv_rank)), _const_spec((q_rank, 2 * nq), True),
                  _const_spec((kv_rank, nq), True), _const_spec((d_b, kv_rank), True),
                  _const_spec((n_heads_a, chunk, chunk), True), _const_spec((chunk, d_a)), _const_spec((1, d_a)),
                  table, table, table],
        out_specs=[rows(d_a), rows(nq), rows(nq), pl.BlockSpec((tf // tkb, d_b, tkb), lambda i: (i, 0, 0)),
                   rows(kv_rank),
                   pl.BlockSpec((None, rope_dim, tf),
                                lambda i: (i // front_tiles_per_seq, 0, i % front_tiles_per_seq))],
        out_shape=[jax.ShapeDtypeStruct((n_rows, d_a), BF16), jax.ShapeDtypeStruct((n_rows, nq), BF16),
                   jax.ShapeDtypeStruct((n_rows, nq), BF16), jax.ShapeDtypeStruct((n_rows // tkb, d_b, tkb), BF16),
                   jax.ShapeDtypeStruct((n_rows, kv_rank), F32), jax.ShapeDtypeStruct((B, rope_dim, S), F32)],
        compiler_params=_params(("arbitrary",)),
        name="prompt_front",
    )
    out_a_p, q_p, k_p, vt_p, ckv_p, kr_p = front(
        xp, gm, w_in_p, gs, gq, gkv, w_q12, w_ukp, w_uv_flat.T, w_spatial[0][:, :chunk, :chunk], b_sp, goa,
        tqc, tqs, tk)

    tq = ATTN_Q_TILE
    nqb = S // tq
    o_p = pl.pallas_call(
        functools.partial(_prompt_attn_kernel, scale=scale, v_head_dim=v_dim),
        grid=(B, nqb),
        in_specs=[pl.BlockSpec((tq, nq), lambda b, i: (b * nqb + i, 0)),
                  pl.BlockSpec((S, nq), lambda b, i: (b, 0)),
                  pl.BlockSpec((S // tkb, d_b, tkb), lambda b, i: (b, 0, 0))],
        out_specs=pl.BlockSpec((tq, d_b), lambda b, i: (b * nqb + i, 0)),
        out_shape=jax.ShapeDtypeStruct((n_rows, d_b), F32),
        compiler_params=_params(("arbitrary", "arbitrary")),
        name="prompt_attn",
    )(q_p, k_p, vt_p)

    xs = x_sample.reshape(DB, D)
    sqc, sqs, sk = _rope_tables(past_len + jnp.arange(T, dtype=jnp.int32), rope_dim, nope_dim)
    w_sp0 = jnp.repeat(w_spatial[0][:, 0, 0], hd_a).reshape(1, d_a)
    b_sp0 = jnp.repeat(b_spatial[0][:, 0], hd_a).reshape(1, d_a)
    full = lambda *shape: _const_spec(shape, True)
    whole = lambda *shape: _const_spec(shape)
    out_a_s, v_s, ckv_s, kr_s, qlat_h, qrope_h = pl.pallas_call(
        functools.partial(_sample_front_kernel, d_a=d_a, q_rank=q_rank, kv_rank=kv_rank, n_heads=n_heads,
                          nope_dim=nope_dim, rope_dim=rope_dim),
        grid=(1,),
        in_specs=[full(DB, D), full(1, D), full(D, n_in), full(1, d_a), full(1, q_rank), full(1, kv_rank),
                  full(q_rank, 2 * nq), full(n_heads, HEAD_SLOT, kv_rank), full(1, d_a), full(1, d_a),
                  full(1, d_a), full(1, LANE), full(1, LANE), full(1, LANE)],
        out_specs=[whole(DB, d_a), whole(DB, d_a), whole(DB, kv_rank), whole(DB, rope_dim),
                   whole(n_heads, DB, kv_rank), whole(n_heads, DB, rope_dim)],
        out_shape=[jax.ShapeDtypeStruct((DB, d_a), BF16), jax.ShapeDtypeStruct((DB, d_a), F32),
                   jax.ShapeDtypeStruct((DB, kv_rank), F32), jax.ShapeDtypeStruct((DB, rope_dim), F32),
                   jax.ShapeDtypeStruct((n_heads, DB, kv_rank), F32),
                   jax.ShapeDtypeStruct((n_heads, DB, rope_dim), F32)],
        compiler_params=_params(("arbitrary",)),
        name="sample_front",
    )(xs, gm, w_in_p, gs, gq, gkv, w_q12, w_ukt, w_sp0, b_sp0, goa, sqc, sqs, sk)

    P = PAGES_PER_CHUNK
    chunk_keys = P * page_size
    n_steps = n_rows // tm
    seqs = DB // n_steps
    assert DB % n_steps == 0 and (seqs * (n_pages // P)) % SAMPLE_SLOTS == 0 and SAMPLE_DMA_LEAD < n_pages // P
    cache_krt = jnp.swapaxes(cache_k_rope, 2, 3)
    any_space = pl.BlockSpec(memory_space=pl.ANY)
    tile = lambda w: pl.BlockSpec((tm, w), lambda i, pt: (i, 0))
    per_step = lambda *tail: pl.BlockSpec((seqs,) + tail, lambda i, pt: (i,) + (0,) * len(tail))
    y_p, conv_p, o_lat = pl.pallas_call(
        functools.partial(_prompt_back_decode_kernel, tiles_per_seq=tiles_per_seq, d_ff=d_ff, conv_w=conv_w,
                          scale=scale, pages_per_chunk=P, page_size=page_size),
        grid_spec=pltpu.PrefetchScalarGridSpec(
            num_scalar_prefetch=1,
            grid=(n_steps,),
            in_specs=[tile(D), tile(d_a), tile(d_b),
                      _const_spec((1, d_b)), _const_spec((d_a + d_b, D), True), _const_spec((1, D)),
                      _const_spec((D, 2 * d_ff), True), _const_spec((conv_w, 2 * d_ff)),
                      _const_spec((1, 2 * d_ff)), _const_spec((d_ff, D), True), _const_spec((1, D)),
                      whole(DB, n_heads, kv_rank), whole(DB, n_heads, rope_dim),
                      per_step(1, kv_rank), per_step(1, rope_dim), any_space, any_space],
            out_specs=[tile(D),
                       pl.BlockSpec((None, conv_w - 1, 2 * d_ff), lambda i, pt: (i // tiles_per_seq, 0, 0)),
                       per_step(n_heads, kv_rank)],
            scratch_shapes=[pltpu.VMEM((SUBLANE, 2 * d_ff), F32),
                            pltpu.VMEM((CONV_STAGE_BUFFERS, tm + SUBLANE, FFN_CHUNK), F32),
                            pltpu.VMEM((SAMPLE_SLOTS, P, page_size, kv_rank), F32),
                            pltpu.VMEM((SAMPLE_SLOTS, P, rope_dim, page_size), F32),
                            pltpu.VMEM((n_heads, chunk_keys), F32),
                            pltpu.SemaphoreType.DMA((SAMPLE_SLOTS,)), pltpu.SemaphoreType.DMA((SAMPLE_SLOTS,))],
        ),
        out_shape=[jax.ShapeDtypeStruct((n_rows, D), F32),
                   jax.ShapeDtypeStruct((B, conv_w - 1, 2 * d_ff), F32),
                   jax.ShapeDtypeStruct((DB, n_heads, kv_rank), F32)],
        compiler_params=_params(("arbitrary",)),
        name="prompt_back_sample_attn",
    )(page_table, xp, out_a_p, o_p, gob, w_out_bf, gf, w_up_bf, wc, bc, w_down_bf, gfin,
      qlat_h.transpose(1, 0, 2), qrope_h.transpose(1, 0, 2), ckv_s.reshape(DB, 1, kv_rank),
      kr_s.reshape(DB, 1, rope_dim), cache_kv_latent, cache_krt)

    prev = state_ffn_conv[0].transpose(1, 0, 2)
    y_s, up_s = pl.pallas_call(
        functools.partial(_sample_back_kernel, n_heads=n_heads, d_ff=d_ff),
        grid=(1,),
        in_specs=[full(DB, D), full(DB, d_a), full(n_heads, DB, kv_rank), full(n_heads, kv_rank, d_b),
                  full(conv_w - 1, DB, 2 * d_ff), full(1, d_b), full(d_a + d_b, D), full(1, D),
                  full(D, 2 * d_ff), full(conv_w, 2 * d_ff), full(1, 2 * d_ff), full(d_ff, D), full(1, D)],
        out_specs=[whole(DB, D), whole(DB, 2 * d_ff)],
        out_shape=[jax.ShapeDtypeStruct((DB, D), F32), jax.ShapeDtypeStruct((DB, 2 * d_ff), F32)],
        compiler_params=_params(("arbitrary",)),
        name="sample_back",
    )(xs, out_a_s, o_lat.transpose(1, 0, 2), w_uvp, prev, gob, w_out_bf, gf, w_up_bf, wc, bc, w_down_bf, gfin)

    conv_s = jnp.concatenate([state_ffn_conv[0][:, 1:, :], up_s[:, None, :]], axis=1)
    return (y_p.reshape(B, S, D), y_s.reshape(DB, T, D),
            ckv_p.reshape(1, B, S, kv_rank), jnp.swapaxes(kr_p, 1, 2).reshape(1, B, S, rope_dim),
            conv_p.reshape(1, B, conv_w - 1, 2 * d_ff),
            ckv_s.reshape(1, DB, T, kv_rank), kr_s.reshape(1, DB, T, rope_dim),
            v_s.reshape(1, DB, T, d_a), conv_s.reshape(1, DB, conv_w - 1, 2 * d_ff))
```

```python
import functools

import jax
import jax.numpy as jnp
import numpy as np
from jax import lax
from jax.experimental import pallas as pl
from jax.experimental.pallas import tpu as pltpu

F32 = jnp.float32
BF16 = jnp.bfloat16

EPS = 1e-6
ROPE_THETA = 10000.0
LANE = 128
SUBLANE = 8
BF16_SUBLANES = 16
HEAD_SLOT = LANE
FRONT_TILE = 1024
PROMPT_TILE = 512
ATTN_Q_TILE = 512
ATTN_K_TILE = 256
ATTN_QK_LEAD = 2
FFN_CHUNK = 256
CONV_STAGE_BUFFERS = 4
FFN_UP_LEAD = 1
PAGES_PER_CHUNK = 16
SAMPLE_DMA_LEAD = 2
SAMPLE_SLOTS = SAMPLE_DMA_LEAD + 2
VMEM_LIMIT = 56 * 1024 * 1024
LOG2E = float(np.log2(np.e))


def _rms(x, g):
    r = lax.rsqrt(jnp.mean(x * x, axis=-1, keepdims=True) + EPS)
    return (x * r) * g


def _gelu(x):
    return 0.5 * x * (1.0 + lax.erf(x * np.float32(np.sqrt(0.5))))


def _silu(x):
    return x * (1.0 / (1.0 + jnp.exp(-x)))


def _dot(a, b):
    return jnp.dot(a, b, preferred_element_type=F32)


def _dot_nt(a, b):
    return lax.dot_general(a, b, (((1,), (1,)), ((), ())), preferred_element_type=F32)


def _const_spec(shape, single_buffer=False):
    nd = len(shape)
    mode = pl.Buffered(1) if single_buffer else None
    return pl.BlockSpec(shape, lambda *_: (0,) * nd, pipeline_mode=mode)


def _params(semantics):
    return pltpu.CompilerParams(dimension_semantics=semantics, vmem_limit_bytes=VMEM_LIMIT)


def _front_common(x, g_mix, w_in, g_sgu, g_q, g_kv, w_q12, tqc, tqs, tk, d_a, q_rank, kv_rank, n_heads):
    h = _rms(x, g_mix).astype(BF16)
    proj = _dot(h, w_in)
    o1, o2 = d_a, 2 * d_a
    o3 = o2 + q_rank
    o4 = o3 + kv_rank
    u = _gelu(proj[:, :o1])
    v = _rms(_gelu(proj[:, o1:o2]), g_sgu)
    c_q = _rms(proj[:, o2:o3], g_q)
    c_kv = _rms(proj[:, o3:o4], g_kv)
    y = proj[:, o4:o4 + LANE] * tk
    kr = y + pltpu.roll(y, LANE // 2, 1)
    q12 = _dot(c_q.astype(BF16), w_q12)
    nq = n_heads * HEAD_SLOT
    q_parts = []
    for hd in range(n_heads):
        a = q12[:, hd * HEAD_SLOT:(hd + 1) * HEAD_SLOT]
        b = q12[:, nq + hd * HEAD_SLOT:nq + (hd + 1) * HEAD_SLOT]
        q_parts.append(a * tqc + b * tqs)
    return u, v, jnp.concatenate(q_parts, axis=1), c_kv, kr


def _prompt_front_kernel(x_ref, g_mix_ref, w_in_ref, g_sgu_ref, g_q_ref, g_kv_ref, w_q12_ref,
                         w_ukp_ref, w_uvt_ref, w_sp_ref, b_sp_ref, g_out_a_ref,
                         tqc_ref, tqs_ref, tk_ref,
                         out_a_ref, q_ref, k_ref, vt_ref, ckv_ref, kr_ref,
                         *, d_a, q_rank, kv_rank, n_heads, n_heads_a, chunk, rope_dim):
    tm = x_ref.shape[0]
    u, v, q_full, c_kv, kr = _front_common(
        x_ref[...], g_mix_ref[...], w_in_ref[...], g_sgu_ref[...], g_q_ref[...], g_kv_ref[...],
        w_q12_ref[...], tqc_ref[...], tqs_ref[...], tk_ref[...], d_a, q_rank, kv_rank, n_heads)
    q_ref[...] = q_full.astype(BF16)
    ckv_ref[...] = c_kv
    kr_ref[...] = kr.T[:rope_dim, :]

    c_bf = c_kv.astype(BF16)
    lane = lax.broadcasted_iota(jnp.int32, (tm, LANE), 1)
    kr_slot = jnp.where(lane >= LANE // 2, kr, 0.0)
    k_nope = _dot(c_bf, w_ukp_ref[...])
    k_parts = [k_nope[:, hd * HEAD_SLOT:(hd + 1) * HEAD_SLOT] + kr_slot for hd in range(n_heads)]
    k_ref[...] = jnp.concatenate(k_parts, axis=1).astype(BF16)
    v_t = _dot_nt(w_uvt_ref[...], c_bf).astype(BF16)
    tk = vt_ref.shape[2]
    for c in range(tm // tk):
        vt_ref[c] = v_t[:, c * tk:(c + 1) * tk]

    hd_a = d_a // n_heads_a
    heads_per_slab = LANE // hd_a
    row = lax.broadcasted_iota(jnp.int32, (chunk, chunk), 0)
    col = lax.broadcasted_iota(jnp.int32, (chunk, chunk), 1)
    w_causal = [jnp.where(row >= col, w_sp_ref[hd], 0.0).astype(BF16) for hd in range(n_heads_a)]
    v_bf = v.astype(BF16)
    lane_c = lax.broadcasted_iota(jnp.int32, (chunk, LANE), 1)
    gate_rows = []
    for c in range(tm // chunk):
        slabs = []
        for j in range(d_a // LANE):
            x_slab = v_bf[c * chunk:(c + 1) * chunk, j * LANE:(j + 1) * LANE]
            mixed = None
            for t in range(heads_per_slab):
                s_t = _dot(w_causal[j * heads_per_slab + t], x_slab)
                in_head = (lane_c >= t * hd_a) & (lane_c < (t + 1) * hd_a)
                mixed = s_t if mixed is None else jnp.where(in_head, s_t, mixed)
            slabs.append(mixed)
        gate_rows.append(jnp.concatenate(slabs, axis=1) + b_sp_ref[...])
    s = jnp.concatenate(gate_rows, axis=0)
    out_a_ref[...] = _rms(u * s, g_out_a_ref[...]).astype(BF16)


def _prompt_attn_kernel(q_ref, k_ref, vt_ref, o_ref, *, scale, v_head_dim):
    tq = q_ref.shape[0]
    tk = vt_ref.shape[2]
    i = pl.program_id(1)
    n_heads = q_ref.shape[1] // HEAD_SLOT
    c2 = scale * LOG2E
    q_t = [q_ref[:, t * HEAD_SLOT:(t + 1) * HEAD_SLOT].astype(F32).T.astype(BF16)
           for t in range(n_heads)]
    blocks_per_step = tq // tk

    def step(g, carry, masked):
        units = [(d, t) for d in range(blocks_per_step) for t in range(n_heads)]

        def scores(d, t):
            lo = d * tk if masked else 0
            start = pl.multiple_of((g * blocks_per_step + d) * tk, tk)
            ks = k_ref[pl.ds(start, tk), t * HEAD_SLOT:(t + 1) * HEAD_SLOT]
            s_t = _dot(ks, q_t[t][:, lo:])
            if not masked:
                return s_t
            kpos = start + lax.broadcasted_iota(jnp.int32, s_t.shape, 0)
            qpos = i * tq + lo + lax.broadcasted_iota(jnp.int32, s_t.shape, 1)
            return jnp.where(kpos <= qpos, s_t, -jnp.inf)

        s_ahead = [scores(*u) for u in units[:ATTN_QK_LEAD]]
        state = list(carry)
        for n, (d, t) in enumerate(units):
            lo = d * tk if masked else 0
            m_all, acc_all = state[t]
            m, acc = m_all[0:1, lo:], acc_all[:, lo:]
            if n + ATTN_QK_LEAD < len(units):
                s_ahead.append(scores(*units[n + ATTN_QK_LEAD]))
            s_t, s_ahead[n] = s_ahead[n], None
            m_new = jnp.maximum(m, jnp.max(s_t, axis=0, keepdims=True))
            p_t = jnp.exp2((s_t - m_new) * c2).astype(BF16)
            alpha = jnp.exp2((m - m_new) * c2)
            m_new = jnp.broadcast_to(m_new, (SUBLANE, m_new.shape[1]))
            v_t = jnp.concatenate(
                [vt_ref[g * blocks_per_step + d, t * v_head_dim:(t + 1) * v_head_dim, :], ones_rows], axis=0)
            acc_new = alpha * acc + _dot(v_t, p_t)
            if lo:
                m_new = jnp.concatenate([m_all[:, :lo], m_new], axis=1)
                acc_new = jnp.concatenate([acc_all[:, :lo], acc_new], axis=1)
            state[t] = (m_new, acc_new)
        return tuple(state)

    ones_rows = jnp.ones((BF16_SUBLANES, tk), BF16)
    init = tuple((jnp.full((SUBLANE, tq), -jnp.inf, F32), jnp.zeros((v_head_dim + BF16_SUBLANES, tq), F32))
                 for _ in range(n_heads))
    carry = lax.fori_loop(0, i, functools.partial(step, masked=False), init)
    carry = step(i, carry, True)
    o_t = jnp.concatenate([acc[:v_head_dim] / acc[v_head_dim:v_head_dim + 1] for _, acc in carry], axis=0)
    o_ref[...] = o_t.T


def _back_common(x, out_a, o_raw, g_out_b, w_out_ref, g_ffn, w_up_ref, w_conv_ref, b_conv_ref,
                 w_down_ref, g_final, conv_inputs, d_ff, after_chunk=None):
    d_a = out_a.shape[1]
    ob = _rms(o_raw, g_out_b).astype(BF16)
    mix = _dot(out_a, w_out_ref[:d_a, :]) + _dot(ob, w_out_ref[d_a:, :])
    x1 = x + mix
    h2 = _rms(x1, g_ffn).astype(BF16)
    n_chunks = d_ff // FFN_CHUNK

    def up_proj(j):
        cols = [slice(base + j * FFN_CHUNK, base + (j + 1) * FFN_CHUNK) for base in (0, d_ff)]
        return [(_dot(h2, w_up_ref[:, c]), c) for c in cols]

    f = None
    pending = [up_proj(j) for j in range(min(FFN_UP_LEAD, n_chunks))]
    for j in range(n_chunks):
        if j + FFN_UP_LEAD < n_chunks:
            pending.append(up_proj(j + FFN_UP_LEAD))
        conv = []
        for half, (up, cols) in enumerate(pending[j]):
            r2, r1 = conv_inputs(up, cols, 2 * j + half)
            conv.append(b_conv_ref[:, cols] + r2 * w_conv_ref[0:1, cols] + r1 * w_conv_ref[1:2, cols]
                        + up * w_conv_ref[2:3, cols])
        act = (_silu(conv[0]) * conv[1]).astype(BF16)
        part = _dot(act, w_down_ref[j * FFN_CHUNK:(j + 1) * FFN_CHUNK, :])
        f = part if f is None else f + part
        pending[j] = None
        if after_chunk is not None:
            after_chunk(j, n_chunks)
    return _rms(x1 + f, g_final)


def _sample_front_kernel(x_ref, g_mix_ref, w_in_ref, g_sgu_ref, g_q_ref, g_kv_ref, w_q12_ref,
                         w_ukt_ref, w_sp0_ref, b_sp0_ref, g_out_a_ref, tqc_ref, tqs_ref, tk_ref,
                         out_a_ref, v_ref, ckv_ref, kr_ref, qlat_ref, qrope_ref,
                         *, d_a, q_rank, kv_rank, n_heads, nope_dim, rope_dim):
    u, v, q_full, c_kv, kr = _front_common(
        x_ref[...], g_mix_ref[...], w_in_ref[...], g_sgu_ref[...], g_q_ref[...], g_kv_ref[...],
        w_q12_ref[...], tqc_ref[...], tqs_ref[...], tk_ref[...], d_a, q_rank, kv_rank, n_heads)
    v_ref[...] = v
    ckv_ref[...] = c_kv
    kr_ref[...] = kr[:, :rope_dim]
    s = v * w_sp0_ref[...] + b_sp0_ref[...]
    out_a_ref[...] = _rms(u * s, g_out_a_ref[...]).astype(BF16)
    q_bf = q_full.astype(BF16)
    for hd in range(n_heads):
        q_h = q_bf[:, hd * HEAD_SLOT:(hd + 1) * HEAD_SLOT]
        qlat_ref[hd] = _dot(q_h, w_ukt_ref[hd])
        qrope_ref[hd] = q_full[:, hd * HEAD_SLOT + nope_dim:hd * HEAD_SLOT + nope_dim + rope_dim]


def _paged_decoder(step, pt_ref, qlat_ref, qrope_ref, cnew_ref, krnew_ref, c_hbm, krt_hbm,
                   o_ref, cbuf, krbuf, s_ref, sem_c, sem_k, *, scale, pages_per_chunk, page_size):
    seqs = o_ref.shape[0]
    n_seq = qlat_ref.shape[0]
    chunks = pt_ref.shape[1] // pages_per_chunk
    n_units = seqs * chunks
    n_slots = cbuf.shape[0]
    chunk_keys = pages_per_chunk * page_size
    c2 = scale * LOG2E
    first = step * seqs
    state = {}

    def owner(u):
        k = u // chunks
        if u < n_units:
            return first + k, u % chunks
        return jnp.minimum(first + k, n_seq - 1), u % chunks

    def start_chunk(seq, ch, slot):
        for i in range(pages_per_chunk):
            page = pt_ref[seq, ch * pages_per_chunk + i]
            queue = i % 2
            pltpu.make_async_copy(c_hbm.at[0, page], cbuf.at[slot, i], sem_c.at[slot]).start(priority=queue)
            pltpu.make_async_copy(krt_hbm.at[0, page], krbuf.at[slot, i], sem_k.at[slot]).start(priority=queue)

    def wait_chunk(slot):
        first_pages = pl.ds(0, pages_per_chunk)
        pltpu.make_async_copy(c_hbm.at[0, first_pages], cbuf.at[slot], sem_c.at[slot]).wait()
        pltpu.make_async_copy(krt_hbm.at[0, first_pages], krbuf.at[slot], sem_k.at[slot]).wait()

    def latent(slot):
        return cbuf[slot].reshape(chunk_keys, cbuf.shape[3]).astype(BF16)

    def scores(seq, slot):
        s_lat = _dot_nt(qlat_ref[seq].astype(BF16), latent(slot))
        q_rope = qrope_ref[seq].astype(BF16)
        s_rope = jnp.concatenate(
            [_dot(q_rope, krbuf[slot, i].astype(BF16)) for i in range(pages_per_chunk)], axis=1)
        return s_lat + s_rope

    def prologue():
        for u in range(SAMPLE_DMA_LEAD):
            start_chunk(*owner(u), u % n_slots)
        wait_chunk(0)
        s_ref[...] = scores(owner(0)[0], 0)

    def begin_sequence(k):
        q_lat = qlat_ref[first + k]
        c_new = cnew_ref[k]
        state["m"] = (jnp.sum(q_lat * c_new, axis=-1, keepdims=True)
                      + jnp.sum(qrope_ref[first + k] * krnew_ref[k], axis=-1, keepdims=True))
        state["l"] = jnp.ones_like(state["m"])
        state["acc"] = jnp.broadcast_to(c_new, q_lat.shape)

    unit_scores = {}

    def score_next(u):
        if u == 0:
            unit_scores[0] = s_ref[...]
        ahead = u + SAMPLE_DMA_LEAD
        start_chunk(*owner(ahead), ahead % n_slots)
        wait_chunk((u + 1) % n_slots)
        unit_scores[u + 1] = scores(owner(u + 1)[0], (u + 1) % n_slots)
        if u == n_units - 1:
            s_ref[...] = unit_scores.pop(n_units)

    def update(u):
        if u % chunks == 0:
            begin_sequence(u // chunks)
        m, l, acc, s = state["m"], state["l"], state["acc"], unit_scores.pop(u)
        m_new = jnp.maximum(m, jnp.max(s, axis=-1, keepdims=True))
        p = jnp.exp2((s - m_new) * c2)
        alpha = jnp.exp2((m - m_new) * c2)
        state["l"] = alpha * l + jnp.sum(p, axis=-1, keepdims=True)
        state["acc"] = alpha * acc + _dot(p.astype(BF16), latent(u % n_slots))
        state["m"] = m_new
        if u % chunks == chunks - 1:
            o_ref[u // chunks] = state["acc"] / state["l"]

    def epilogue():
        for d in range(1, SAMPLE_DMA_LEAD):
            wait_chunk((n_units + d) % n_slots)

    return prologue, score_next, update, epilogue, n_units


def _prompt_back_decode_kernel(pt_ref, x_ref, out_a_ref, o_ref, g_out_b_ref, w_out_ref, g_ffn_ref, w_up_ref,
                               w_conv_ref, b_conv_ref, w_down_ref, g_final_ref,
                               qlat_ref, qrope_ref, cnew_ref, krnew_ref, c_hbm, krt_hbm,
                               y_ref, conv_ref, olat_ref,
                               tail_ref, stage_ref, cbuf, krbuf, s_ref, sem_c, sem_k,
                               *, tiles_per_seq, d_ff, conv_w, scale, pages_per_chunk, page_size):
    tm = x_ref.shape[0]
    keep = conv_w - 1
    step = pl.program_id(0)
    prologue, score_next, update, epilogue, n_units = _paged_decoder(
        step, pt_ref, qlat_ref, qrope_ref, cnew_ref, krnew_ref, c_hbm, krt_hbm,
        olat_ref, cbuf, krbuf, s_ref, sem_c, sem_k, scale=scale, pages_per_chunk=pages_per_chunk,
        page_size=page_size)
    pl.when(step == 0)(prologue)

    @pl.when(step % tiles_per_seq == 0)
    def _():
        tail_ref[...] = jnp.zeros_like(tail_ref)

    def conv_inputs(up, cols, k):
        stage = stage_ref.at[k % stage_ref.shape[0]]
        stage[0:SUBLANE, :] = tail_ref[:, cols]
        stage[SUBLANE:, :] = up
        tail_ref[:, cols] = up[tm - SUBLANE:, :]
        conv_ref[:, cols] = up[tm - keep:, :]
        return stage[pl.ds(SUBLANE - 2, tm), :], stage[pl.ds(SUBLANE - 1, tm), :]

    def units_of(j, n_chunks):
        return range(j * n_units // n_chunks, (j + 1) * n_units // n_chunks)

    def after_chunk(j, n_chunks):
        for u in units_of(j, n_chunks):
            score_next(u)
            update(u)

    y_ref[...] = _back_common(
        x_ref[...], out_a_ref[...], o_ref[...], g_out_b_ref[...], w_out_ref, g_ffn_ref[...], w_up_ref,
        w_conv_ref, b_conv_ref, w_down_ref, g_final_ref[...], conv_inputs, d_ff, after_chunk=after_chunk)
    pl.when(step == pl.num_programs(0) - 1)(epilogue)


def _sample_back_kernel(x_ref, out_a_ref, olat_ref, w_uvp_ref, prev_ref, g_out_b_ref, w_out_ref, g_ffn_ref,
                        w_up_ref, w_conv_ref, b_conv_ref, w_down_ref, g_final_ref,
                        y_ref, up_ref, *, n_heads, d_ff):
    o_raw = None
    for hd in range(n_heads):
        part = _dot(olat_ref[hd].astype(BF16), w_uvp_ref[hd])
        o_raw = part if o_raw is None else o_raw + part

    def conv_inputs(up, cols, k):
        del k
        up_ref[:, cols] = up
        return prev_ref[0, :, cols], prev_ref[1, :, cols]

    y_ref[...] = _back_common(
        x_ref[...], out_a_ref[...], o_raw, g_out_b_ref[...], w_out_ref, g_ffn_ref[...], w_up_ref,
        w_conv_ref, b_conv_ref, w_down_ref, g_final_ref[...], conv_inputs, d_ff)


def _rope_tables(pos, rope_dim, nope_dim):
    half = rope_dim // 2
    inv = ROPE_THETA ** (-jnp.arange(half, dtype=F32) / half)
    ang = pos.astype(F32)[:, None] * inv
    cos, sin = jnp.cos(ang), jnp.sin(ang)
    n = pos.shape[0]
    z = lambda w: jnp.zeros((n, w), F32)
    pad = LANE - nope_dim - rope_dim
    tqc = jnp.concatenate([jnp.ones((n, nope_dim), F32), cos, cos, z(pad)], axis=1)
    tqs = jnp.concatenate([z(nope_dim), -sin, sin, z(pad)], axis=1)
    tk = jnp.concatenate([-sin, sin, z(LANE // 2 - rope_dim), cos, cos, z(LANE // 2 - rope_dim)], axis=1)
    return tqc, tqs, tk


def _pack_weights(w_in, w_uq, w_uk, w_uv, d_a, q_rank, kv_rank, rope_dim, nope_dim):
    half = rope_dim // 2
    d_model = w_in.shape[0]
    n_heads = w_uq.shape[1]
    o4 = 2 * d_a + q_rank + kv_rank
    x1, x2 = w_in[:, o4:o4 + half], w_in[:, o4 + half:o4 + rope_dim]
    zc = lambda rows, w: jnp.zeros((rows, w), w_in.dtype)
    gap = LANE // 2 - rope_dim
    w_in_p = jnp.concatenate([w_in[:, :o4], x2, x1, zc(d_model, gap), x1, x2, zc(d_model, gap)], axis=1)
    pad = HEAD_SLOT - nope_dim - rope_dim
    main, swap = [], []
    for hd in range(n_heads):
        nope = w_uq[:, hd, :nope_dim]
        r1, r2 = w_uq[:, hd, nope_dim:nope_dim + half], w_uq[:, hd, nope_dim + half:]
        main.append(jnp.concatenate([nope, r1, r2, zc(q_rank, pad)], axis=1))
        swap.append(jnp.concatenate([zc(q_rank, nope_dim), r2, r1, zc(q_rank, pad)], axis=1))
    w_q12 = jnp.concatenate(main + swap, axis=1)
    w_ukp = jnp.concatenate(
        [jnp.concatenate([w_uk[:, hd, :], zc(kv_rank, HEAD_SLOT - nope_dim)], axis=1) for hd in range(n_heads)],
        axis=1)
    w_ukt = jnp.stack(
        [jnp.concatenate([w_uk[:, hd, :].T, jnp.zeros((HEAD_SLOT - nope_dim, kv_rank), w_uk.dtype)], axis=0)
         for hd in range(n_heads)])
    v_dim = w_uv.shape[2]
    w_uv_flat = w_uv.reshape(kv_rank, n_heads * v_dim)
    w_uvp = jnp.stack(
        [jnp.pad(w_uv[:, hd, :], ((0, 0), (hd * v_dim, (n_heads - 1 - hd) * v_dim))) for hd in range(n_heads)])
    cast = lambda a: a.astype(BF16)
    return cast(w_in_p), cast(w_q12), cast(w_ukp), cast(w_ukt), cast(w_uv_flat), cast(w_uvp)


def kernel(x_prompt, x_sample, cache_kv_latent, cache_k_rope, state_ffn_conv, page_table,
           g_mix, w_in, g_sgu, w_spatial, b_spatial, g_q, w_uq, g_kv, w_uk, w_uv,
           g_out_a, g_out_b, w_out, g_ffn, w_up, w_conv, b_conv, w_down, g_final):
    B, S, D = x_prompt.shape
    DB, T, _ = x_sample.shape
    depth = w_in.shape[0]
    assert depth == 1 and T == 1
    n_heads_a, chunk = w_spatial.shape[1], w_spatial.shape[2]
    d_a = g_sgu.shape[1]
    q_rank, kv_rank = g_q.shape[1], g_kv.shape[1]
    n_heads, nope_dim, v_dim = w_uk.shape[2], w_uk.shape[3], w_uv.shape[3]
    rope_dim = w_uq.shape[3] - nope_dim
    d_b = n_heads * v_dim
    d_ff = w_down.shape[1]
    conv_w = w_conv.shape[1]
    n_pool, page_size = cache_kv_latent.shape[1], cache_kv_latent.shape[2]
    n_pages = page_table.shape[1]
    past_len = n_pages * page_size
    scale = float((nope_dim + rope_dim) ** -0.5)
    assert d_ff % FFN_CHUNK == 0 and S % PROMPT_TILE == 0 and S % FRONT_TILE == 0 and FRONT_TILE % chunk == 0
    assert n_pages % (SAMPLE_SLOTS * PAGES_PER_CHUNK) == 0 and conv_w == 3
    assert S % ATTN_Q_TILE == 0 and ATTN_Q_TILE % ATTN_K_TILE == 0 and FRONT_TILE % ATTN_K_TILE == 0

    row2 = lambda a: a.reshape(1, -1)
    w_in_p, w_q12, w_ukp, w_ukt, w_uv_flat, w_uvp = _pack_weights(
        w_in[0], w_uq[0], w_uk[0], w_uv[0], d_a, q_rank, kv_rank, rope_dim, nope_dim)
    w_out_bf, w_up_bf, w_down_bf = w_out[0].astype(BF16), w_up[0].astype(BF16), w_down[0].astype(BF16)
    gm, gs, gq, gkv = row2(g_mix[0]), row2(g_sgu[0]), row2(g_q[0]), row2(g_kv[0])
    goa, gob, gf, gfin = row2(g_out_a[0]), row2(g_out_b[0]), row2(g_ffn[0]), row2(g_final)
    wc, bc = w_conv[0], row2(b_conv[0])
    hd_a = d_a // n_heads_a

    n_rows = B * S
    tm = PROMPT_TILE
    tkb = ATTN_K_TILE
    tiles_per_seq = S // tm
    xp = x_prompt.reshape(n_rows, D)
    tqc, tqs, tk = _rope_tables(jnp.arange(S, dtype=jnp.int32), rope_dim, nope_dim)
    b_sp = jnp.repeat(b_spatial[0][:, :chunk].T, hd_a, axis=1)
    n_in = w_in_p.shape[1]
    nq = n_heads * HEAD_SLOT
    tf = FRONT_TILE
    front_tiles_per_seq = S // tf
    rows = lambda w: pl.BlockSpec((tf, w), lambda i: (i, 0))
    table = pl.BlockSpec((tf, LANE), lambda i: (i % front_tiles_per_seq, 0))
    front = pl.pallas_call(
        functools.partial(_prompt_front_kernel, d_a=d_a, q_rank=q_rank, kv_rank=kv_rank, n_heads=n_heads,
                          n_heads_a=n_heads_a, chunk=chunk, rope_dim=rope_dim),
        grid=(n_rows // tf,),
        in_specs=[rows(D), _const_spec((1, D)), _const_spec((D, n_in), True), _const_spec((1, d_a)),
                  _const_spec((1, q_rank)), _const_spec((1, kv_rank)), _const_spec((q_rank, 2 * nq), True),
                  _const_spec((kv_rank, nq), True), _const_spec((d_b, kv_rank), True),
                  _const_spec((n_heads_a, chunk, chunk), True), _const_spec((chunk, d_a)), _const_spec((1, d_a)),
                  table, table, table],
        out_specs=[rows(d_a), rows(nq), rows(nq), pl.BlockSpec((tf // tkb, d_b, tkb), lambda i: (i, 0, 0)),
                   rows(kv_rank),
                   pl.BlockSpec((None, rope_dim, tf),
                                lambda i: (i // front_tiles_per_seq, 0, i % front_tiles_per_seq))],
        out_shape=[jax.ShapeDtypeStruct((n_rows, d_a), BF16), jax.ShapeDtypeStruct((n_rows, nq), BF16),
                   jax.ShapeDtypeStruct((n_rows, nq), BF16), jax.ShapeDtypeStruct((n_rows // tkb, d_b, tkb), BF16),
                   jax.ShapeDtypeStruct((n_rows, kv_rank), F32), jax.ShapeDtypeStruct((B, rope_dim, S), F32)],
        compiler_params=_params(("arbitrary",)),
        name="prompt_front",
    )
    out_a_p, q_p, k_p, vt_p, ckv_p, kr_p = front(
        xp, gm, w_in_p, gs, gq, gkv, w_q12, w_ukp, w_uv_flat.T, w_spatial[0][:, :chunk, :chunk], b_sp, goa,
        tqc, tqs, tk)

    tq = ATTN_Q_TILE
    nqb = S // tq
    o_p = pl.pallas_call(
        functools.partial(_prompt_attn_kernel, scale=scale, v_head_dim=v_dim),
        grid=(B, nqb),
        in_specs=[pl.BlockSpec((tq, nq), lambda b, i: (b * nqb + i, 0)),
                  pl.BlockSpec((S, nq), lambda b, i: (b, 0)),
                  pl.BlockSpec((S // tkb, d_b, tkb), lambda b, i: (b, 0, 0))],
        out_specs=pl.BlockSpec((tq, d_b), lambda b, i: (b * nqb + i, 0)),
        out_shape=jax.ShapeDtypeStruct((n_rows, d_b), F32),
        compiler_params=_params(("arbitrary", "arbitrary")),
        name="prompt_attn",
    )(q_p, k_p, vt_p)

    xs = x_sample.reshape(DB, D)
    sqc, sqs, sk = _rope_tables(past_len + jnp.arange(T, dtype=jnp.int32), rope_dim, nope_dim)
    w_sp0 = jnp.repeat(w_spatial[0][:, 0, 0], hd_a).reshape(1, d_a)
    b_sp0 = jnp.repeat(b_spatial[0][:, 0], hd_a).reshape(1, d_a)
    full = lambda *shape: _const_spec(shape, True)
    whole = lambda *shape: _const_spec(shape)
    out_a_s, v_s, ckv_s, kr_s, qlat_h, qrope_h = pl.pallas_call(
        functools.partial(_sample_front_kernel, d_a=d_a, q_rank=q_rank, kv_rank=kv_rank, n_heads=n_heads,
                          nope_dim=nope_dim, rope_dim=rope_dim),
        grid=(1,),
        in_specs=[full(DB, D), full(1, D), full(D, n_in), full(1, d_a), full(1, q_rank), full(1, kv_rank),
                  full(q_rank, 2 * nq), full(n_heads, HEAD_SLOT, kv_rank), full(1, d_a), full(1, d_a),
                  full(1, d_a), full(1, LANE), full(1, LANE), full(1, LANE)],
        out_specs=[whole(DB, d_a), whole(DB, d_a), whole(DB, kv_rank), whole(DB, rope_dim),
                   whole(n_heads, DB, kv_rank), whole(n_heads, DB, rope_dim)],
        out_shape=[jax.ShapeDtypeStruct((DB, d_a), BF16), jax.ShapeDtypeStruct((DB, d_a), F32),
                   jax.ShapeDtypeStruct((DB, kv_rank), F32), jax.ShapeDtypeStruct((DB, rope_dim), F32),
                   jax.ShapeDtypeStruct((n_heads, DB, kv_rank), F32),
                   jax.ShapeDtypeStruct((n_heads, DB, rope_dim), F32)],
        compiler_params=_params(("arbitrary",)),
        name="sample_front",
    )(xs, gm, w_in_p, gs, gq, gkv, w_q12, w_ukt, w_sp0, b_sp0, goa, sqc, sqs, sk)

    P = PAGES_PER_CHUNK
    chunk_keys = P * page_size
    n_steps = n_rows // tm
    seqs = DB // n_steps
    assert DB % n_steps == 0 and (seqs * (n_pages // P)) % SAMPLE_SLOTS == 0 and SAMPLE_DMA_LEAD < n_pages // P
    cache_krt = jnp.swapaxes(cache_k_rope, 2, 3)
    any_space = pl.BlockSpec(memory_space=pl.ANY)
    tile = lambda w: pl.BlockSpec((tm, w), lambda i, pt: (i, 0))
    per_step = lambda *tail: pl.BlockSpec((seqs,) + tail, lambda i, pt: (i,) + (0,) * len(tail))
    y_p, conv_p, o_lat = pl.pallas_call(
        functools.partial(_prompt_back_decode_kernel, tiles_per_seq=tiles_per_seq, d_ff=d_ff, conv_w=conv_w,
                          scale=scale, pages_per_chunk=P, page_size=page_size),
        grid_spec=pltpu.PrefetchScalarGridSpec(
            num_scalar_prefetch=1,
            grid=(n_steps,),
            in_specs=[tile(D), tile(d_a), tile(d_b),
                      _const_spec((1, d_b)), _const_spec((d_a + d_b, D), True), _const_spec((1, D)),
                      _const_spec((D, 2 * d_ff), True), _const_spec((conv_w, 2 * d_ff)),
                      _const_spec((1, 2 * d_ff)), _const_spec((d_ff, D), True), _const_spec((1, D)),
                      whole(DB, n_heads, kv_rank), whole(DB, n_heads, rope_dim),
                      per_step(1, kv_rank), per_step(1, rope_dim), any_space, any_space],
            out_specs=[tile(D),
                       pl.BlockSpec((None, conv_w - 1, 2 * d_ff), lambda i, pt: (i // tiles_per_seq, 0, 0)),
                       per_step(n_heads, kv_rank)],
            scratch_shapes=[pltpu.VMEM((SUBLANE, 2 * d_ff), F32),
                            pltpu.VMEM((CONV_STAGE_BUFFERS, tm + SUBLANE, FFN_CHUNK), F32),
                            pltpu.VMEM((SAMPLE_SLOTS, P, page_size, kv_rank), F32),
                            pltpu.VMEM((SAMPLE_SLOTS, P, rope_dim, page_size), F32),
                            pltpu.VMEM((n_heads, chunk_keys), F32),
                            pltpu.SemaphoreType.DMA((SAMPLE_SLOTS,)), pltpu.SemaphoreType.DMA((SAMPLE_SLOTS,))],
        ),
        out_shape=[jax.ShapeDtypeStruct((n_rows, D), F32),
                   jax.ShapeDtypeStruct((B, conv_w - 1, 2 * d_ff), F32),
                   jax.ShapeDtypeStruct((DB, n_heads, kv_rank), F32)],
        compiler_params=_params(("arbitrary",)),
        name="prompt_back_sample_attn",
    )(page_table, xp, out_a_p, o_p, gob, w_out_bf, gf, w_up_bf, wc, bc, w_down_bf, gfin,
      qlat_h.transpose(1, 0, 2), qrope_h.transpose(1, 0, 2), ckv_s.reshape(DB, 1, kv_rank),
      kr_s.reshape(DB, 1, rope_dim), cache_kv_latent, cache_krt)

    prev = state_ffn_conv[0].transpose(1, 0, 2)
    y_s, up_s = pl.pallas_call(
        functools.partial(_sample_back_kernel, n_heads=n_heads, d_ff=d_ff),
        grid=(1,),
        in_specs=[full(DB, D), full(DB, d_a), full(n_heads, DB, kv_rank), full(n_heads, kv_rank, d_b),
                  full(conv_w - 1, DB, 2 * d_ff), full(1, d_b), full(d_a + d_b, D), full(1, D),
                  full(D, 2 * d_ff), full(conv_w, 2 * d_ff), full(1, 2 * d_ff), full(d_ff, D), full(1, D)],
        out_specs=[whole(DB, D), whole(DB, 2 * d_ff)],
        out_shape=[jax.ShapeDtypeStruct((DB, D), F32), jax.ShapeDtypeStruct((DB, 2 * d_ff), F32)],
        compiler_params=_params(("arbitrary",)),
        name="sample_back",
    )(xs, out_a_s, o_lat.transpose(1, 0, 2), w_uvp, prev, gob, w_out_bf, gf, w_up_bf, wc, bc, w_down_bf, gfin)

    conv_s = jnp.concatenate([state_ffn_conv[0][:, 1:, :], up_s[:, None, :]], axis=1)
    return (y_p.reshape(B, S, D), y_s.reshape(DB, T, D),
            ckv_p.reshape(1, B, S, kv_rank), jnp.swapaxes(kr_p, 1, 2).reshape(1, B, S, rope_dim),
            conv_p.reshape(1, B, conv_w - 1, 2 * d_ff),
            ckv_s.reshape(1, DB, T, kv_rank), kr_s.reshape(1, DB, T, rope_dim),
            v_s.reshape(1, DB, T, d_a), conv_s.reshape(1, DB, conv_w - 1, 2 * d_ff))
```
